```python
import math
import functools
import jax
import jax.numpy as jnp
from jax import lax
import numpy as np

D_MODEL = 1024
BATCH = 8
SEQ = 4096
DEPTH = 2

CTX_LEN = 256
GRID_W = 64
ROPE_BASE = 10000.0
NORM_EPS = 1e-6
NEG_INF = -1e30
Q_BLOCK = 128
N_MOD = 6

CONV_WIDTH = 3
CONV_DIM = 512
DIFF_HEADS = 4
DIFF_HEAD_DIM = 64
DIFF_V_DIM = 2 * DIFF_HEAD_DIM
DIFF_QK_WIDTH = DIFF_HEADS * 2 * DIFF_HEAD_DIM
DIFF_WIDTH = DIFF_HEADS * DIFF_V_DIM
AB_IN_WIDTH = 3 * CONV_DIM + 2 * DIFF_QK_WIDTH + DIFF_WIDTH
AB_SPLITS = (CONV_DIM, 2 * CONV_DIM, 3 * CONV_DIM, 3 * CONV_DIM + DIFF_QK_WIDTH, 3 * CONV_DIM + 2 * DIFF_QK_WIDTH)
AB_OUT_WIDTH = CONV_DIM + DIFF_WIDTH

SWA_HEADS = 16
SWA_KV_HEADS = 4
SWA_GROUP = SWA_HEADS // SWA_KV_HEADS
SWA_HEAD_DIM = 64
SWA_WINDOW = 128
SWA_Q_WIDTH = SWA_HEADS * SWA_HEAD_DIM
SWA_KV_WIDTH = SWA_KV_HEADS * SWA_HEAD_DIM
SWA_SPLITS = (SWA_Q_WIDTH, SWA_Q_WIDTH + SWA_KV_WIDTH)

N_EXPERTS = 16
EXPERT_FF = 1024
CAPACITY_FACTOR = 2

kernel_name = 'hybrid_diffusion_conv_diffattn_swa_ecmoe'


def rms_norm(x, gain):
    xf = x.astype(jnp.float32)
    y = xf * lax.rsqrt(jnp.mean(xf * xf, axis=-1, keepdims=True) + NORM_EPS)
    return (y * gain.astype(jnp.float32)).astype(x.dtype)


def ada_params(cond, w, b):
    m = jax.nn.silu(cond) @ w + b
    return jnp.split(m[..., None, :], N_MOD, axis=-1)


def modulate(x, gain, shift, scale):
    return rms_norm(x, gain) * (1.0 + scale) + shift


def diff_lambda_init(layer_idx):
    return 0.8 - 0.6 * math.exp(-0.3 * layer_idx)


def axial_rope(rows, rot_dim):
    n_freq = rot_dim // 4
    inv_freq = ROPE_BASE ** (-jnp.arange(n_freq, dtype=jnp.float32) / n_freq)
    row = jnp.repeat(jnp.arange(rows, dtype=jnp.float32), GRID_W)
    col = jnp.tile(jnp.arange(GRID_W, dtype=jnp.float32), rows)
    ang = jnp.stack([row, col], axis=-1)[:, :, None] * inv_freq
    return jnp.cos(ang), jnp.sin(ang)


def apply_rope(x, cos, sin):
    n_freq = cos.shape[-1]
    bshape = (x.shape[1],) + (1,) * (x.ndim - 3) + (2, n_freq)
    cs = cos.reshape(bshape).astype(x.dtype)
    sn = sin.reshape(bshape).astype(x.dtype)
    xr = x.reshape(x.shape[:-1] + (2, 2, n_freq))
    x1, x2 = xr[..., 0, :], xr[..., 1, :]
    out = jnp.stack([x1 * cs - x2 * sn, x2 * cs + x1 * sn], axis=-2)
    return out.reshape(x.shape)


def short_conv(u, w):
    half = CONV_WIDTH // 2
    return lax.conv_general_dilated(
        u, w[:, None, :], window_strides=(1,), padding=[(half, half)],
        dimension_numbers=('NWC', 'WIO', 'NWC'), feature_group_count=u.shape[-1])


def blocked_queries(q, fn):
    b, t = q.shape[:2]
    nb = t // Q_BLOCK
    qb = jnp.moveaxis(q.reshape((b, nb, Q_BLOCK) + q.shape[2:]), 1, 0)
    out = lax.map(lambda args: fn(*args), (qb, jnp.arange(nb)))
    return jnp.moveaxis(out, 0, 1).reshape((b, t) + out.shape[3:])


def diff_attend(q, k, v, lam):
    s = jnp.einsum('bqhmd,bkhmd->bhmqk', q, k).astype(jnp.float32) * DIFF_HEAD_DIM ** -0.5
    p = jax.nn.softmax(s, axis=-1)
    a = (p[:, :, 0] - lam * p[:, :, 1]).astype(v.dtype)
    return jnp.einsum('bhqk,bkhe->bqhe', a, v)


def sink_attend(q, k, v, sink, mask):
    s = jnp.einsum('bqngd,bknd->bngqk', q, k).astype(jnp.float32) * SWA_HEAD_DIM ** -0.5
    s = jnp.where(mask, s, NEG_INF)
    sk = jnp.broadcast_to(sink.astype(jnp.float32)[None, :, :, None, None], s.shape[:-1] + (1,))
    p = jax.nn.softmax(jnp.concatenate([s, sk], axis=-1), axis=-1)[..., :-1]
    return jnp.einsum('bngqk,bknd->bqngd', p.astype(v.dtype), v)


def window_attend_latent(q, k_lat, v_lat, k_ctx, v_ctx, sink):
    t = q.shape[1]
    pad = ((0, 0), (SWA_WINDOW, SWA_WINDOW), (0, 0), (0, 0))
    k_pad = jnp.pad(k_lat, pad)
    v_pad = jnp.pad(v_lat, pad)
    span = Q_BLOCK + 2 * SWA_WINDOW
    qi = jnp.arange(Q_BLOCK)[:, None]
    kj = jnp.arange(span)[None, :]
    band = (kj >= qi) & (kj <= qi + 2 * SWA_WINDOW)
    ctx_ok = jnp.ones((Q_BLOCK, k_ctx.shape[1]), dtype=bool)

    def block(qb, blk):
        start = blk * Q_BLOCK
        kb = lax.dynamic_slice_in_dim(k_pad, start, span, axis=1)
        vb = lax.dynamic_slice_in_dim(v_pad, start, span, axis=1)
        pos = start - SWA_WINDOW + kj
        mask = jnp.concatenate([ctx_ok, band & (pos >= 0) & (pos < t)], axis=1)
        return sink_attend(qb, jnp.concatenate([k_ctx, kb], axis=1),
                           jnp.concatenate([v_ctx, vb], axis=1), sink, mask)

    return blocked_queries(q, block)


def mixer_conv_diff(h, h_ctx, w_in, conv_w, lam_q1, lam_k1, lam_q2, lam_k2, subln_g, w_out,
                    cos, sin, lam_init, ctx_out):
    f32 = jnp.float32
    lam = (jnp.exp(jnp.sum(lam_q1.astype(f32) * lam_k1.astype(f32)))
           - jnp.exp(jnp.sum(lam_q2.astype(f32) * lam_k2.astype(f32))) + lam_init)

    def project(u):
        b, t = u.shape[:2]
        bg, cg, xv, q, k, v = jnp.split(u @ w_in, AB_SPLITS, axis=-1)
        conv = bg * short_conv(cg * xv, conv_w)
        q = q.reshape(b, t, DIFF_HEADS, 2, DIFF_HEAD_DIM)
        k = k.reshape(b, t, DIFF_HEADS, 2, DIFF_HEAD_DIM)
        v = v.reshape(b, t, DIFF_HEADS, DIFF_V_DIM)
        return conv, q, k, v

    def merge(conv, o):
        b, t = o.shape[:2]
        o = rms_norm(o, subln_g) * (1.0 - lam_init)
        return jnp.concatenate([conv, o.reshape(b, t, DIFF_WIDTH)], axis=-1) @ w_out

    conv_c, q_c, k_c, v_c = project(h_ctx)
    conv_l, q_l, k_l, v_l = project(h)
    q_l = apply_rope(q_l, cos, sin)
    k_l = apply_rope(k_l, cos, sin)
    k_all = jnp.concatenate([k_c, k_l], axis=1)
    v_all = jnp.concatenate([v_c, v_l], axis=1)
    o_l = blocked_queries(q_l, lambda qb, blk: diff_attend(qb, k_all, v_all, lam))
    y = merge(conv_l, o_l)
    y_ctx = merge(conv_c, diff_attend(q_c, k_c, v_c, lam)) if ctx_out else None
    return y, y_ctx


def mixer_window_gqa(h, h_ctx, w_qkv, sink, w_out, cos, sin, ctx_out):
    sink = sink.reshape(SWA_KV_HEADS, SWA_GROUP)

    def project(u):
        b, t = u.shape[:2]
        q, k, v = jnp.split(u @ w_qkv, SWA_SPLITS, axis=-1)
        return (q.reshape(b, t, SWA_KV_HEADS, SWA_GROUP, SWA_HEAD_DIM),
                k.reshape(b, t, SWA_KV_HEADS, SWA_HEAD_DIM),
                v.reshape(b, t, SWA_KV_HEADS, SWA_HEAD_DIM))

    def out(o):
        b, t = o.shape[:2]
        return o.reshape(b, t, SWA_Q_WIDTH) @ w_out

    q_c, k_c, v_c = project(h_ctx)
    q_l, k_l, v_l = project(h)
    q_l = apply_rope(q_l, cos, sin)
    k_l = apply_rope(k_l, cos, sin)
    y = out(window_attend_latent(q_l, k_l, v_l, k_c, v_c, sink))
    n_ctx = h_ctx.shape[1]
    y_ctx = out(sink_attend(q_c, k_c, v_c, sink, jnp.ones((n_ctx, n_ctx), dtype=bool))) if ctx_out else None
    return y, y_ctx


def expert_choice_ffn(h, router_w, w_gate, w_up, w_down):
    n, d = h.shape[1], h.shape[2]
    cap = CAPACITY_FACTOR * n // N_EXPERTS
    aff = jax.nn.softmax((h @ router_w).astype(jnp.float32), axis=-1)
    g, idx = lax.top_k(jnp.swapaxes(aff, 1, 2), cap)
    hs = jax.vmap(lambda hb, ib: hb[ib])(h, idx)
    a = jnp.einsum('becd,edf->becf', hs, w_gate)
    u = jnp.einsum('becd,edf->becf', hs, w_up)
    y = jnp.einsum('becf,efd->becd', jax.nn.silu(a) * u, w_down) * g[..., None].astype(h.dtype)
    return jax.vmap(lambda yb, ib: jax.ops.segment_sum(
        yb.reshape(-1, d), ib.reshape(-1), num_segments=n))(y, idx)


def trunk_layer(x, ctx, c, c_ctx, ada_w, ada_b, norm1_g, norm2_g, mixer, moe, ctx_out):
    sh1, sc1, g1, sh2, sc2, g2 = ada_params(c, ada_w, ada_b)
    csh1, csc1, cg1, csh2, csc2, cg2 = ada_params(c_ctx, ada_w, ada_b)
    y, y_ctx = mixer(modulate(x, norm1_g, sh1, sc1), modulate(ctx, norm1_g, csh1, csc1), ctx_out=ctx_out)
    x = x + g1 * y
    x = x + g2 * expert_choice_ffn(modulate(x, norm2_g, sh2, sc2), *moe)
    if ctx_out:
        ctx = ctx + cg1 * y_ctx
        ctx = ctx + cg2 * expert_choice_ffn(modulate(ctx, norm2_g, csh2, csc2), *moe)
    return x, ctx


def setup_inputs(seed: int = 0) -> dict:
    key = jax.random.key(seed)
    keys = iter(jax.random.split(key, 64))

    def nrm(shape, scale):
        return jax.random.normal(next(keys), shape, jnp.float32) * scale

    def gain(n):
        return 1.0 + nrm((n,), 0.01)

    d = D_MODEL
    inp = {}
    inp['x'] = nrm((BATCH, SEQ, d), 1.0)
    inp['c'] = nrm((BATCH, d), 1.0)
    inp['ctx'] = nrm((BATCH, CTX_LEN, d), 1.0)
    inp['c_ctx'] = nrm((d,), 1.0)
    inp['l0_ada_w'] = nrm((d, N_MOD * d), 0.5 * d ** -0.5)
    inp['l0_ada_b'] = nrm((N_MOD * d,), 0.01)
    inp['l0_norm1_g'] = gain(d)
    inp['l0_w_in'] = nrm((d, AB_IN_WIDTH), d ** -0.5)
    inp['l0_conv_w'] = nrm((CONV_WIDTH, CONV_DIM), CONV_WIDTH ** -0.5)
    inp['l0_lambda_q1'] = nrm((DIFF_HEAD_DIM,), 0.1)
    inp['l0_lambda_k1'] = nrm((DIFF_HEAD_DIM,), 0.1)
    inp['l0_lambda_q2'] = nrm((DIFF_HEAD_DIM,), 0.1)
    inp['l0_lambda_k2'] = nrm((DIFF_HEAD_DIM,), 0.1)
    inp['l0_subln_g'] = gain(DIFF_V_DIM)
    inp['l0_w_out'] = nrm((AB_OUT_WIDTH, d), AB_OUT_WIDTH ** -0.5)
    inp['l0_norm2_g'] = gain(d)
    inp['l0_router_w'] = nrm((d, N_EXPERTS), d ** -0.5)
    inp['l0_exp_w_gate'] = nrm((N_EXPERTS, d, EXPERT_FF), d ** -0.5)
    inp['l0_exp_w_up'] = nrm((N_EXPERTS, d, EXPERT_FF), d ** -0.5)
    inp['l0_exp_w_down'] = nrm((N_EXPERTS, EXPERT_FF, d), EXPERT_FF ** -0.5)
    inp['l1_ada_w'] = nrm((d, N_MOD * d), 0.5 * d ** -0.5)
    inp['l1_ada_b'] = nrm((N_MOD * d,), 0.01)
    inp['l1_norm1_g'] = gain(d)
    inp['l1_w_qkv'] = nrm((d, SWA_Q_WIDTH + 2 * SWA_KV_WIDTH), d ** -0.5)
    inp['l1_sink'] = nrm((SWA_HEADS,), 1.0)
    inp['l1_w_out'] = nrm((SWA_Q_WIDTH, d), SWA_Q_WIDTH ** -0.5)
    inp['l1_norm2_g'] = gain(d)
    inp['l1_router_w'] = nrm((d, N_EXPERTS), d ** -0.5)
    inp['l1_exp_w_gate'] = nrm((N_EXPERTS, d, EXPERT_FF), d ** -0.5)
    inp['l1_exp_w_up'] = nrm((N_EXPERTS, d, EXPERT_FF), d ** -0.5)
    inp['l1_exp_w_down'] = nrm((N_EXPERTS, EXPERT_FF, d), EXPERT_FF ** -0.5)
    inp['final_norm_g'] = gain(d)
    return inp


def reference(x, c, ctx, c_ctx,
              l0_ada_w, l0_ada_b, l0_norm1_g, l0_w_in, l0_conv_w, l0_lambda_q1, l0_lambda_k1,
              l0_lambda_q2, l0_lambda_k2, l0_subln_g, l0_w_out, l0_norm2_g, l0_router_w,
              l0_exp_w_gate, l0_exp_w_up, l0_exp_w_down,
              l1_ada_w, l1_ada_b, l1_norm1_g, l1_w_qkv, l1_sink, l1_w_out, l1_norm2_g, l1_router_w,
              l1_exp_w_gate, l1_exp_w_up, l1_exp_w_down,
              final_norm_g):
    rows = x.shape[1] // GRID_W
    cos_d, sin_d = axial_rope(rows, DIFF_HEAD_DIM)
    cos_w, sin_w = axial_rope(rows, SWA_HEAD_DIM)
    layers = [
        (l0_ada_w, l0_ada_b, l0_norm1_g, l0_norm2_g,
         functools.partial(mixer_conv_diff, w_in=l0_w_in, conv_w=l0_conv_w,
                           lam_q1=l0_lambda_q1, lam_k1=l0_lambda_k1,
                           lam_q2=l0_lambda_q2, lam_k2=l0_lambda_k2,
                           subln_g=l0_subln_g, w_out=l0_w_out, cos=cos_d, sin=sin_d,
                           lam_init=diff_lambda_init(0)),
         (l0_router_w, l0_exp_w_gate, l0_exp_w_up, l0_exp_w_down)),
        (l1_ada_w, l1_ada_b, l1_norm1_g, l1_norm2_g,
         functools.partial(mixer_window_gqa, w_qkv=l1_w_qkv, sink=l1_sink, w_out=l1_w_out,
                           cos=cos_w, sin=sin_w),
         (l1_router_w, l1_exp_w_gate, l1_exp_w_up, l1_exp_w_down)),
    ]
    for i in range(DEPTH):
        ada_w, ada_b, n1, n2, mixer, moe = layers[i]
        x, ctx = trunk_layer(x, ctx, c, c_ctx, ada_w, ada_b, n1, n2, mixer, moe,
                             ctx_out=i < DEPTH - 1)
    return rms_norm(x, final_norm_g)
```

```python
import functools
import math

import jax
import jax.numpy as jnp
from jax import lax
from jax.experimental import pallas as pl
from jax.experimental.pallas import tpu as pltpu

F32 = jnp.float32
BF16 = jnp.bfloat16
I32 = jnp.int32

GRID_W = 64
ROPE_BASE = 10000.0
NORM_EPS = 1e-6
NEG_INF = -1e30
N_MOD = 6

CONV_DIM = 512
DIFF_HEADS = 4
DIFF_HEAD_DIM = 64
DIFF_V_DIM = 128
LAM_INIT0 = 0.8 - 0.6 * math.exp(-0.3 * 0)

SWA_HEADS = 16
SWA_KV_HEADS = 4
SWA_HEAD_DIM = 64
SWA_WINDOW = 128

N_EXPERTS = 16
CAPACITY_FACTOR = 2

LANES = 128
MOD_ROWS = 16
AUG = LANES
ROW_TILE = 256
COMB_W = 64
VMEM_LIMIT = 56 * 1024 * 1024


def _cparams(sem):
    return pltpu.CompilerParams(dimension_semantics=sem, vmem_limit_bytes=VMEM_LIMIT)


def _dot(a, b):
    return jnp.dot(a, b, preferred_element_type=F32)


def _dot_nt(a, b):
    return lax.dot_general(a, b, (((1,), (1,)), ((), ())), preferred_element_type=F32)


def _modulate(x, gain, shift, scale):
    ms = jnp.mean(x * x, axis=-1, keepdims=True)
    return x * lax.rsqrt(ms + NORM_EPS) * (gain * (1.0 + scale)) + shift


def _ada_kernel(c_ref, w_ref, b_ref, o_ref):
    c = c_ref[...]
    s = (c * jax.nn.sigmoid(c)).astype(BF16)
    o_ref[...] = _dot(s, w_ref[...].astype(BF16)) + b_ref[...]


def _ada(cc, w, b):
    d, n = w.shape
    tn = 512
    return pl.pallas_call(
        _ada_kernel,
        grid=(n // tn,),
        in_specs=[pl.BlockSpec((MOD_ROWS, d), lambda j: (0, 0)),
                  pl.BlockSpec((d, tn), lambda j: (0, j)),
                  pl.BlockSpec((1, tn), lambda j: (0, j))],
        out_specs=pl.BlockSpec((MOD_ROWS, tn), lambda j: (0, j)),
        out_shape=jax.ShapeDtypeStruct((MOD_ROWS, n), F32),
        compiler_params=_cparams(("arbitrary",)),
        name="ada",
    )(cc, w, b.reshape(1, n))


def _rope_tables(t):
    n_freq = DIFF_HEAD_DIM // 4
    inv_freq = ROPE_BASE ** (-jnp.arange(n_freq, dtype=F32) / n_freq)
    pos = jnp.arange(t, dtype=I32)
    row = (pos // GRID_W).astype(F32)[:, None] * inv_freq
    col = (pos % GRID_W).astype(F32)[:, None] * inv_freq
    zeros = jnp.zeros_like(row)
    cos64 = jnp.concatenate([jnp.cos(row), jnp.cos(row), jnp.cos(col), jnp.cos(col)], axis=1)
    sa64 = jnp.concatenate([-jnp.sin(row), zeros, -jnp.sin(col), zeros], axis=1)
    sb64 = jnp.concatenate([zeros, jnp.sin(row), zeros, jnp.sin(col)], axis=1)
    rep = lambda a: jnp.concatenate([a, a], axis=1)
    return rep(cos64), rep(sa64), rep(sb64)


def _rope128(c, cos, sa, sb):
    return c * cos + pltpu.roll(c, LANES - 16, 1) * sa + pltpu.roll(c, 16, 1) * sb


def _proj_kernel(*refs, outs, rope, has_vt):
    x_ref, mod_ref, g_ref, w_ref = refs[:4]
    pos = 4
    if rope:
        cos_ref, sa_ref, sb_ref = refs[pos:pos + 3]
        pos += 3
    if has_vt:
        wvt_ref = refs[pos]
        pos += 1
    out_refs = refs[pos:]
    m = mod_ref[0]
    h = _modulate(x_ref[0], g_ref[...], m[0:1, :], m[1:2, :]).astype(BF16)
    y = _dot(h, w_ref[...])
    for (kind, c0, width, do_rope, scale), o_ref in zip(outs, out_refs):
        for j in range(width // LANES):
            c = y[:, c0 + j * LANES:c0 + (j + 1) * LANES]
            if kind == "mul":
                c = c * y[:, c0 + width + j * LANES:c0 + width + (j + 1) * LANES]
            if do_rope and rope:
                c = _rope128(c, cos_ref[...], sa_ref[...], sb_ref[...])
            if scale != 1.0:
                c = c * scale
            if kind == "heads":
                o_ref[0, j] = c.astype(o_ref.dtype)
            else:
                o_ref[0, :, j * LANES:(j + 1) * LANES] = c.astype(o_ref.dtype)
    if has_vt:
        vt = _dot_nt(wvt_ref[...], h)
        o_ref = out_refs[len(outs)]
        for j in range(vt.shape[0] // LANES):
            o_ref[0, j] = vt[j * LANES:(j + 1) * LANES, :].astype(o_ref.dtype)


def _proj(x, mods, mod_row, gain, w, outs, tables=None, wvt=None):
    b, t, d = x.shape
    tm = min(ROW_TILE, t)
    n = w.shape[1]
    rope = tables is not None
    in_specs = [pl.BlockSpec((1, tm, d), lambda bi, i: (bi, i, 0)),
                pl.BlockSpec((1, N_MOD, d), lambda bi, i: (mod_row(bi), 0, 0)),
                pl.BlockSpec((1, d), lambda bi, i: (0, 0)),
                pl.BlockSpec((d, n), lambda bi, i: (0, 0))]
    args = [x, mods, gain.reshape(1, d), w]
    if rope:
        in_specs += [pl.BlockSpec((tm, LANES), lambda bi, i: (i, 0))] * 3
        args += list(tables)
    if wvt is not None:
        in_specs.append(pl.BlockSpec(wvt.shape, lambda bi, i: (0, 0)))
        args.append(wvt)
    out_specs, out_shapes = [], []
    for (kind, c0, width, do_rope, scale) in outs:
        if kind == "heads":
            nh = width // LANES
            out_specs.append(pl.BlockSpec((1, nh, tm, LANES), lambda bi, i: (bi, 0, i, 0)))
            out_shapes.append(jax.ShapeDtypeStruct((b, nh, t, LANES), BF16))
        else:
            out_specs.append(pl.BlockSpec((1, tm, width), lambda bi, i: (bi, i, 0)))
            out_shapes.append(jax.ShapeDtypeStruct((b, t, width), BF16))
    if wvt is not None:
        nh = wvt.shape[0] // LANES
        out_specs.append(pl.BlockSpec((1, nh, LANES, tm), lambda bi, i: (bi, 0, 0, i)))
        out_shapes.append(jax.ShapeDtypeStruct((b, nh, LANES, t), BF16))
    return pl.pallas_call(
        functools.partial(_proj_kernel, outs=tuple(outs), rope=rope, has_vt=wvt is not None),
        grid=(b, t // tm),
        in_specs=in_specs,
        out_specs=out_specs,
        out_shape=out_shapes,
        compiler_params=_cparams(("parallel", "arbitrary")),
        name="proj",
    )(*args)


def _diff_attn_kernel(*refs, nseg):
    q_ref = refs[0]
    k_refs = refs[1:1 + nseg]
    vt_refs = refs[1 + nseg:1 + 2 * nseg]
    lq1, lk1, lq2, lk2, g_ref, o_ref = refs[1 + 2 * nseg:]
    q = q_ref[0, 0]
    lane = lax.broadcasted_iota(I32, q.shape, 1)
    zero = jnp.zeros_like(q)
    qs = (jnp.where(lane < DIFF_HEAD_DIM, q, zero), jnp.where(lane >= DIFF_HEAD_DIM, q, zero))
    lam = (jnp.exp(jnp.sum(lq1[...] * lk1[...], axis=1, keepdims=True))
           - jnp.exp(jnp.sum(lq2[...] * lk2[...], axis=1, keepdims=True)) + LAM_INIT0)
    probs, norms = [], []
    for qh in qs:
        s = [_dot_nt(k[0, 0], qh) for k in k_refs]
        m = functools.reduce(jnp.maximum, [jnp.max(x, axis=0, keepdims=True) for x in s])
        p = [jnp.exp(x - m) for x in s]
        norms.append(functools.reduce(jnp.add, [jnp.sum(x, axis=0, keepdims=True) for x in p]))
        probs.append(p)
    r1 = 1.0 / norms[0]
    c2 = lam / norms[1]
    ot = None
    for si in range(nseg):
        a = (probs[0][si] * r1 - probs[1][si] * c2).astype(BF16)
        part = _dot(vt_refs[si][0, 0], a)
        ot = part if ot is None else ot + part
    ms = jnp.mean(ot * ot, axis=0, keepdims=True)
    on = ot * lax.rsqrt(ms + NORM_EPS) * (g_ref[...] * (1.0 - LAM_INIT0))
    o_ref[0, 0] = on.T.astype(o_ref.dtype)


def _diff_attn(q, ks, vts, lams, subln_g):
    b, nh, t, _ = q.shape
    tq = min(ROW_TILE, t)
    nseg = len(ks)
    in_specs = [pl.BlockSpec((1, 1, tq, LANES), lambda bi, h, i: (bi, h, i, 0))]
    for k in ks:
        in_specs.append(pl.BlockSpec((1, 1, k.shape[2], LANES), lambda bi, h, i: (bi, h, 0, 0)))
    for vt in vts:
        in_specs.append(pl.BlockSpec((1, 1, LANES, vt.shape[3]), lambda bi, h, i: (bi, h, 0, 0)))
    in_specs += [pl.BlockSpec((1, DIFF_HEAD_DIM), lambda bi, h, i: (0, 0))] * 4
    in_specs.append(pl.BlockSpec((DIFF_V_DIM, 1), lambda bi, h, i: (0, 0)))
    return pl.pallas_call(
        functools.partial(_diff_attn_kernel, nseg=nseg),
        grid=(b, nh, t // tq),
        in_specs=in_specs,
        out_specs=pl.BlockSpec((1, 1, tq, LANES), lambda bi, h, i: (bi, h, i, 0)),
        out_shape=jax.ShapeDtypeStruct((b, nh, t, LANES), BF16),
        compiler_params=_cparams(("parallel", "arbitrary", "arbitrary")),
        name="diff_attn",
    )(q, *ks, *vts, *[l.reshape(1, DIFF_HEAD_DIM) for l in lams], subln_g.reshape(DIFF_V_DIM, 1))


def _swa_kernel(sink_ref, q_ref, kl_ref, vl_ref, kc_ref, vc_ref, o_ref, *, t, tq):
    span = tq + 2 * SWA_WINDOW
    t0 = pl.program_id(1) * tq
    ws = pl.multiple_of(jnp.clip(t0 - SWA_WINDOW, 0, t - span), LANES)
    kwin = kl_ref[0, pl.ds(ws, span), :]
    vwin = vl_ref[0, pl.ds(ws, span), :]
    kc = kc_ref[0]
    vc = vc_ref[0]
    qpos = t0 + lax.broadcasted_iota(I32, (tq, span), 0)
    kpos = ws + lax.broadcasted_iota(I32, (tq, span), 1)
    band = jnp.abs(qpos - kpos) <= SWA_WINDOW
    lane = lax.broadcasted_iota(I32, (tq, LANES), 1)
    group = SWA_HEADS // SWA_KV_HEADS
    for n in range(SWA_KV_HEADS):
        sl = slice(n * LANES, (n + 1) * LANES)
        kn, vn, kcn, vcn = kwin[:, sl], vwin[:, sl], kc[:, sl], vc[:, sl]
        for j in range(group // 2):
            ci = n * (group // 2) + j
            qc = q_ref[0, :, ci * LANES:(ci + 1) * LANES]
            zero = jnp.zeros_like(qc)
            halves = []
            for half in range(2):
                hq = n * group + 2 * j + half
                keep = (lane < SWA_HEAD_DIM) if half == 0 else (lane >= SWA_HEAD_DIM)
                qz = jnp.where(keep, qc, zero)
                s_l = jnp.where(band, _dot_nt(qz, kn), NEG_INF)
                s_c = _dot_nt(qz, kcn)
                sink = sink_ref[hq]
                m = jnp.maximum(jnp.maximum(jnp.max(s_l, axis=1, keepdims=True),
                                            jnp.max(s_c, axis=1, keepdims=True)), sink)
                p_l = jnp.exp(s_l - m)
                p_c = jnp.exp(s_c - m)
                den = (jnp.sum(p_l, axis=1, keepdims=True) + jnp.sum(p_c, axis=1, keepdims=True)
                       + jnp.exp(sink - m))
                o = (_dot(p_c.astype(BF16), vcn) + _dot(p_l.astype(BF16), vn)) * (1.0 / den)
                halves.append(o)
            o_ref[0, :, ci * LANES:(ci + 1) * LANES] = (
                halves[0] + pltpu.roll(halves[1], SWA_HEAD_DIM, 1)).astype(o_ref.dtype)


def _swa_attn(sink, q, kl, vl, kc, vc):
    b, t, qw = q.shape
    tq = min(ROW_TILE, t - 2 * SWA_WINDOW)
    kw = kl.shape[2]
    c = kc.shape[1]
    return pl.pallas_call(
        functools.partial(_swa_kernel, t=t, tq=tq),
        grid=(b, t // tq),
        in_specs=[pl.BlockSpec(memory_space=pltpu.SMEM),
                  pl.BlockSpec((1, tq, qw), lambda bi, i: (bi, i, 0)),
                  pl.BlockSpec((1, t, kw), lambda bi, i: (bi, 0, 0)),
                  pl.BlockSpec((1, t, kw), lambda bi, i: (bi, 0, 0)),
                  pl.BlockSpec((1, c, kw), lambda bi, i: (bi, 0, 0)),
                  pl.BlockSpec((1, c, kw), lambda bi, i: (bi, 0, 0))],
        out_specs=pl.BlockSpec((1, tq, qw), lambda bi, i: (bi, i, 0)),
        out_shape=jax.ShapeDtypeStruct((b, t, qw), BF16),
        compiler_params=_cparams(("parallel", "arbitrary")),
        name="swa_attn",
    )(sink, q, kl, vl, kc, vc)


def _outproj_kernel(*refs, conv, tm):
    if conv:
        (x_ref, mod_ref, bg_ref, p_ref, pprev_ref, pnext_ref, cw_ref, o_ref, w_ref,
         g2_ref, rwh_ref, rwl_ref, xn_ref, haug_ref, afft_ref) = refs
        i = pl.program_id(1)
        ni = pl.num_programs(1)
        p = p_ref[0].astype(F32)
        row = lax.broadcasted_iota(I32, p.shape, 0)
        halo_prev = jnp.where(i > 0, pprev_ref[0, 15:16, :].astype(F32), 0.0)
        halo_next = jnp.where(i < ni - 1, pnext_ref[0, 0:1, :].astype(F32), 0.0)
        p_prev = jnp.where(row == 0, halo_prev, pltpu.roll(p, 1, 0))
        p_next = jnp.where(row == tm - 1, halo_next, pltpu.roll(p, tm - 1, 0))
        cw = cw_ref[...]
        cv = p_prev * cw[0:1, :] + p * cw[1:2, :] + p_next * cw[2:3, :]
        u = (bg_ref[0].astype(F32) * cv).astype(BF16)
        lhs = jnp.concatenate([u] + [o_ref[0, j] for j in range(DIFF_HEADS)], axis=1)
    else:
        x_ref, mod_ref, o_ref, w_ref, g2_ref, rwh_ref, rwl_ref, xn_ref, haug_ref, afft_ref = refs
        lhs = o_ref[0]
    m = mod_ref[0]
    xn = x_ref[0] + m[2:3, :] * _dot(lhs, w_ref[...])
    xn_ref[0] = xn
    h = _modulate(xn, g2_ref[...], m[3:4, :], m[4:5, :])
    d = h.shape[1]
    haug_ref[0, :, 0:d] = h
    h_hi = h.astype(BF16)
    h_lo = (h - h_hi.astype(F32)).astype(BF16)
    logits = _dot(h_hi, rwh_ref[...]) + _dot(h_hi, rwl_ref[...]) + _dot(h_lo, rwh_ref[...])
    lane = lax.broadcasted_iota(I32, logits.shape, 1)
    logits = jnp.where(lane < N_EXPERTS, logits, NEG_INF)
    e = jnp.exp(logits - jnp.max(logits, axis=1, keepdims=True))
    aff = e / jnp.sum(e, axis=1, keepdims=True)
    haug_ref[0, :, d:d + AUG] = aff
    afft_ref[0] = aff.T[0:N_EXPERTS, :]


def _outproj(x, mods, mod_row, w_out, n2g, rw_hi, rw_lo, o, conv_args=None):
    b, t, d = x.shape
    tm = min(ROW_TILE, t)
    conv = conv_args is not None
    xmap = lambda bi, i: (bi, i, 0)
    in_specs = [pl.BlockSpec((1, tm, d), xmap),
                pl.BlockSpec((1, N_MOD, d), lambda bi, i: (mod_row(bi), 0, 0))]
    args = [x, mods]
    if conv:
        bg, p, cw = conv_args
        hb = tm // 16
        nhb = t // 16
        in_specs += [pl.BlockSpec((1, tm, CONV_DIM), xmap),
                     pl.BlockSpec((1, tm, CONV_DIM), xmap),
                     pl.BlockSpec((1, 16, CONV_DIM), lambda bi, i: (bi, jnp.maximum(i * hb - 1, 0), 0)),
                     pl.BlockSpec((1, 16, CONV_DIM), lambda bi, i: (bi, jnp.minimum((i + 1) * hb, nhb - 1), 0)),
                     pl.BlockSpec((3, CONV_DIM), lambda bi, i: (0, 0)),
                     pl.BlockSpec((1, DIFF_HEADS, tm, LANES), lambda bi, i: (bi, 0, i, 0))]
        args += [bg, p, p, p, cw, o]
    else:
        in_specs.append(pl.BlockSpec((1, tm, o.shape[2]), xmap))
        args.append(o)
    in_specs += [pl.BlockSpec(w_out.shape, lambda bi, i: (0, 0)),
                 pl.BlockSpec((1, d), lambda bi, i: (0, 0)),
                 pl.BlockSpec((d, LANES), lambda bi, i: (0, 0)),
                 pl.BlockSpec((d, LANES), lambda bi, i: (0, 0))]
    args += [w_out, n2g.reshape(1, d), rw_hi, rw_lo]
    return pl.pallas_call(
        functools.partial(_outproj_kernel, conv=conv, tm=tm),
        grid=(b, t // tm),
        in_specs=in_specs,
        out_specs=[pl.BlockSpec((1, tm, d), xmap),
                   pl.BlockSpec((1, tm, d + AUG), xmap),
                   pl.BlockSpec((1, N_EXPERTS, tm), lambda bi, i: (bi, 0, i))],
        out_shape=[jax.ShapeDtypeStruct((b, t, d), F32),
                   jax.ShapeDtypeStruct((b, t, d + AUG), F32),
                   jax.ShapeDtypeStruct((b, N_EXPERTS, t), F32)],
        compiler_params=_cparams(("parallel", "arbitrary")),
        name="outproj",
    )(*args)


def _route_kernel(aff_ref, idx_ref, post_ref, offs_ref, pinc_ref, *, t, cap, capp, lc):
    nch = t // lc
    ne = N_EXPERTS
    aff = aff_ref[0]

    def search(i, lo):
        cand = lo | lax.shift_left(jnp.int32(1), 30 - i)
        cnt = jnp.sum(jnp.where(aff >= pltpu.bitcast(cand, F32), 1.0, 0.0), axis=1, keepdims=True)
        return jnp.where(cnt >= cap, cand, lo)

    thr_bits = lax.fori_loop(0, 31, search, jnp.zeros((ne, 1), I32))
    thr = pltpu.bitcast(thr_bits, F32)
    gt = aff > thr
    eq = aff == thr
    need = cap - jnp.sum(jnp.where(gt, 1.0, 0.0), axis=1, keepdims=True)

    r_i = lax.broadcasted_iota(I32, (lc, lc), 0)
    c_i = lax.broadcasted_iota(I32, (lc, lc), 1)
    upper = jnp.where(r_i < c_i, 1.0, 0.0).astype(BF16)

    def excl_prefix(x):
        outs, offs = [], []
        carry = jnp.zeros((ne, 1), F32)
        for c in range(nch):
            xc = x[:, c * lc:(c + 1) * lc]
            offs.append(carry)
            outs.append(_dot(xc.astype(BF16), upper) + carry)
            carry = carry + jnp.sum(xc, axis=1, keepdims=True)
        offs.append(carry)
        return jnp.concatenate(outs, axis=1), offs

    eqf = jnp.where(eq, 1.0, 0.0)
    eq_rank, _ = excl_prefix(eqf)
    sel = jnp.logical_or(gt, jnp.logical_and(eq, eq_rank < need))
    self_ = jnp.where(sel, 1.0, 0.0)
    pos, offs = excl_prefix(self_)

    lane = lax.broadcasted_iota(I32, (ne, LANES), 1)
    om = jnp.zeros((ne, LANES), F32)
    for c, o in enumerate(offs):
        om = jnp.where(lane == c, o, om)
    offs_ref[0] = om

    posm = jnp.where(sel, pos, -1.0)
    pad = jnp.full((LANES - ne, lc), -1.0, F32)
    for c in range(nch):
        blk = jnp.concatenate([posm[:, c * lc:(c + 1) * lc], pad], axis=0)
        post_ref[0, c * lc:(c + 1) * lc, :] = blk.T
        pinc_ref[c] = pos[:, c * lc:(c + 1) * lc] + self_[:, c * lc:(c + 1) * lc]

    jcol = lax.broadcasted_iota(I32, (capp, 1), 0).astype(F32)
    lane_c = lax.broadcasted_iota(I32, (capp, LANES), 1)

    def per_expert(e, acc_m):
        def per_chunk(c, acc):
            pr = pinc_ref[c, pl.ds(e, 1), :]
            hit = jnp.where(pr <= jcol, 1.0, 0.0)
            fold = hit[:, 0:LANES]
            for k in range(1, lc // LANES):
                fold = fold + hit[:, k * LANES:(k + 1) * LANES]
            return acc + fold
        acc = lax.fori_loop(0, nch, per_chunk, jnp.zeros((capp, LANES), F32))
        col = jnp.sum(acc, axis=1, keepdims=True)
        return jnp.where(lane_c == e, col, acc_m)

    idx_m = lax.fori_loop(0, ne, per_expert, jnp.zeros((capp, LANES), F32))
    base = pl.program_id(0) * t
    idx_ref[0] = idx_m.T[0:ne, :].astype(I32) + base


def _route(aff_t, cap):
    b, ne, t = aff_t.shape
    lc = min(256, t)
    capp = max(cap, LANES)
    nch = t // lc
    return pl.pallas_call(
        functools.partial(_route_kernel, t=t, cap=cap, capp=capp, lc=lc),
        grid=(b,),
        in_specs=[pl.BlockSpec((1, ne, t), lambda bi: (bi, 0, 0))],
        out_specs=[pl.BlockSpec((1, ne, capp), lambda bi: (bi, 0, 0)),
                   pl.BlockSpec((1, t, LANES), lambda bi: (bi, 0, 0)),
                   pl.BlockSpec((1, ne, LANES), lambda bi: (bi, 0, 0))],
        out_shape=[jax.ShapeDtypeStruct((b, ne, capp), I32),
                   jax.ShapeDtypeStruct((b, t, LANES), F32),
                   jax.ShapeDtypeStruct((b, ne, LANES), F32)],
        scratch_shapes=[pltpu.VMEM((nch, ne, lc), F32)],
        compiler_params=_cparams(("arbitrary",)),
        name="route",
    )(aff_t)


def _ffn_kernel(idx_ref, haug_ref, wg_ref, wu_ref, wd_ref, y_ref, hbuf, gsem, *, rows, d):
    e = pl.program_id(0)
    step = e * pl.num_programs(1) + pl.program_id(1)
    nsteps = pl.num_programs(0) * pl.num_programs(1)
    slot = lax.rem(step, 2)

    def row_copy(src_row, dst_slot, dst_row):
        return pltpu.make_async_copy(haug_ref.at[pl.ds(src_row, 1)],
                                     hbuf.at[dst_slot, pl.ds(dst_row, 1)], gsem.at[dst_slot])

    def issue(st, sl):
        base = st * rows

        def body(r, carry):
            row_copy(idx_ref[base + r], sl, r).start()
            return carry
        lax.fori_loop(0, rows, body, 0)

    @pl.when(step == 0)
    def _():
        issue(0, 0)

    @pl.when(step + 1 < nsteps)
    def _():
        issue(step + 1, 1 - slot)

    pltpu.make_async_copy(haug_ref.at[pl.ds(0, rows)], hbuf.at[slot], gsem.at[slot]).wait()

    hrow = hbuf[slot]
    hs = hrow[:, 0:d].astype(BF16)
    lane = lax.broadcasted_iota(I32, (rows, AUG), 1)
    gate = jnp.sum(jnp.where(lane == e, hrow[:, d:d + AUG], 0.0), axis=1, keepdims=True)
    a = _dot(hs, wg_ref[0])
    u = _dot(hs, wu_ref[0])
    hm = (a * jax.nn.sigmoid(a) * u).astype(BF16)
    y_ref[0] = _dot(hm, wd_ref[0]) * gate


def _ffn(idx_flat, haug, wg, wu, wd, rows):
    ne, d, f = wg.shape
    n_rows_total = idx_flat.shape[0]
    steps = n_rows_total // (ne * rows)
    grid_spec = pltpu.PrefetchScalarGridSpec(
        num_scalar_prefetch=1,
        grid=(ne, steps),
        in_specs=[pl.BlockSpec(memory_space=pl.ANY),
                  pl.BlockSpec((1, d, f), lambda e, s, idx: (e, 0, 0)),
                  pl.BlockSpec((1, d, f), lambda e, s, idx: (e, 0, 0)),
                  pl.BlockSpec((1, f, d), lambda e, s, idx: (e, 0, 0))],
        out_specs=pl.BlockSpec((1, rows, d), lambda e, s, idx: (e, s, 0)),
        scratch_shapes=[pltpu.VMEM((2, rows, d + AUG), F32),
                        pltpu.SemaphoreType.DMA((2,))],
    )
    return pl.pallas_call(
        functools.partial(_ffn_kernel, rows=rows, d=d),
        grid_spec=grid_spec,
        out_shape=jax.ShapeDtypeStruct((ne, steps * rows, d), F32),
        compiler_params=_cparams(("arbitrary", "arbitrary")),
        name="ffn",
    )(idx_flat, haug, wg, wu, wd)


def _combine_kernel(ws_ref, nr_ref, x_ref, mod_ref, post_ref, fg_ref, y_ref, o_ref, stage, acc, sem,
                    *, cap, tm, final_norm):
    bi = pl.program_id(0)
    ci = pl.program_id(1)
    nchunk = pl.num_programs(1)
    ne = N_EXPERTS
    w = COMB_W
    per = LANES // w
    total_rows = y_ref.shape[1]
    sbase = (bi * nchunk + ci) * ne
    acc[...] = jnp.zeros_like(acc)
    lane = lax.broadcasted_iota(I32, (tm, LANES), 1)

    def one_round(r, carry):
        starts = []
        for e in range(ne):
            nominal = ws_ref[sbase + e] + r * w
            grow = pl.multiple_of(jnp.minimum(bi * cap + nominal, total_rows - w), 8)
            starts.append((nominal, grow - bi * cap))
            pltpu.make_async_copy(y_ref.at[e, pl.ds(grow, w)], stage.at[pl.ds(e * w, w)], sem).start()
        for e in range(ne):
            pltpu.make_async_copy(y_ref.at[e, pl.ds(0, w)], stage.at[pl.ds(e * w, w)], sem).wait()
        pt = post_ref[0]
        blocks = []
        for g in range(ne // per):
            tgt = jnp.zeros((tm, LANES), F32)
            low = jnp.zeros((tm, LANES), F32)
            mine = jnp.zeros((tm, LANES), F32)
            for k in range(per):
                e = g * per + k
                nominal, first = starts[e]
                inb = jnp.logical_and(lane >= k * w, lane < (k + 1) * w)
                tgt = jnp.where(inb, (first - k * w + lane).astype(F32), tgt)
                low = jnp.where(inb, nominal.astype(F32), low)
                mine = jnp.where(inb, pt[:, e:e + 1], mine)
            hit = jnp.logical_and(mine == tgt, tgt >= low)
            blocks.append(jnp.where(hit, 1.0, 0.0).astype(BF16))
        onehot = jnp.concatenate(blocks, axis=1)
        y = stage[...]
        y_hi = y.astype(BF16)
        y_lo = (y - y_hi.astype(F32)).astype(BF16)
        acc[...] += _dot(onehot, y_hi) + _dot(onehot, y_lo)
        return carry

    lax.fori_loop(0, nr_ref[bi * nchunk + ci], one_round, 0)
    m = mod_ref[0]
    xn = x_ref[0] + m[5:6, :] * acc[...]
    if final_norm:
        ms = jnp.mean(xn * xn, axis=-1, keepdims=True)
        xn = xn * lax.rsqrt(ms + NORM_EPS) * fg_ref[...]
    o_ref[0] = xn


def _combine(ws_flat, nr_flat, x, mods, mod_row, post, y, cap, final_g=None):
    b, t, d = x.shape
    tm = min(ROW_TILE, t)
    ne = N_EXPERTS
    final_norm = final_g is not None
    fg = (final_g if final_norm else jnp.ones((d,), F32)).reshape(1, d)
    grid_spec = pltpu.PrefetchScalarGridSpec(
        num_scalar_prefetch=2,
        grid=(b, t // tm),
        in_specs=[pl.BlockSpec((1, tm, d), lambda bi, i, a, c: (bi, i, 0)),
                  pl.BlockSpec((1, N_MOD, d), lambda bi, i, a, c: (mod_row(bi), 0, 0)),
                  pl.BlockSpec((1, tm, LANES), lambda bi, i, a, c: (bi, i, 0)),
                  pl.BlockSpec((1, d), lambda bi, i, a, c: (0, 0)),
                  pl.BlockSpec(memory_space=pl.ANY)],
        out_specs=pl.BlockSpec((1, tm, d), lambda bi, i, a, c: (bi, i, 0)),
        scratch_shapes=[pltpu.VMEM((ne * COMB_W, d), F32),
                        pltpu.VMEM((tm, d), F32),
                        pltpu.SemaphoreType.DMA(())],
    )
    return pl.pallas_call(
        functools.partial(_combine_kernel, cap=cap, tm=tm, final_norm=final_norm),
        grid_spec=grid_spec,
        out_shape=jax.ShapeDtypeStruct((b, t, d), F32),
        compiler_params=_cparams(("arbitrary", "arbitrary")),
        name="combine",
    )(ws_flat, nr_flat, x, mods, post, fg, y)


def _moe(xn, haug, aff_t, mods, mod_row, wg, wu, wd, rows_per_step, final_g=None):
    b, t, d = xn.shape
    ne = N_EXPERTS
    cap = CAPACITY_FACTOR * t // ne
    idx, post, offs = _route(aff_t, cap)
    idx_flat = jnp.transpose(idx[:, :, :cap], (1, 0, 2)).reshape(-1)
    y = _ffn(idx_flat, haug.reshape(b * t, d + AUG), wg, wu, wd, rows_per_step)
    tm = min(ROW_TILE, t)
    nchunk = t // tm
    lc = min(256, t)
    per = tm // lc
    offs_i = offs.astype(I32)[:, :, 0:t // lc + 1:per]
    start = jnp.transpose(offs_i[:, :, :nchunk], (0, 2, 1))
    end = jnp.transpose(offs_i[:, :, 1:], (0, 2, 1))
    ws = (start // 8) * 8
    nr = jnp.max((end - ws + COMB_W - 1) // COMB_W, axis=2)
    return _combine(ws.reshape(-1), nr.reshape(-1), xn, mods, mod_row, post, y, cap, final_g)


def _split_hi_lo(w):
    hi = w.astype(BF16)
    lo = (w - hi.astype(F32)).astype(BF16)
    return hi, lo


def _router_weights(rw):
    d, ne = rw.shape
    pad = jnp.zeros((d, LANES - ne), F32)
    return _split_hi_lo(jnp.concatenate([rw, pad], axis=1))


def kernel(x, c, ctx, c_ctx, l0_ada_w, l0_ada_b, l0_norm1_g, l0_w_in, l0_conv_w, l0_lambda_q1, l0_lambda_k1, l0_lambda_q2, l0_lambda_k2, l0_subln_g, l0_w_out, l0_norm2_g, l0_router_w, l0_exp_w_gate, l0_exp_w_up, l0_exp_w_down, l1_ada_w, l1_ada_b, l1_norm1_g, l1_w_qkv, l1_sink, l1_w_out, l1_norm2_g, l1_router_w, l1_exp_w_gate, l1_exp_w_up, l1_exp_w_down, final_norm_g):
    b, t, d = x.shape
    nctx = ctx.shape[1]
    assert b + 1 <= MOD_ROWS
    lat_row = lambda bi: bi
    ctx_row = lambda bi: b

    cc = jnp.zeros((MOD_ROWS, d), F32).at[:b].set(c).at[b].set(c_ctx)
    mods0 = _ada(cc, l0_ada_w, l0_ada_b).reshape(MOD_ROWS, N_MOD, d)
    mods1 = _ada(cc, l1_ada_w, l1_ada_b).reshape(MOD_ROWS, N_MOD, d)
    tables = _rope_tables(t)
    qscale = DIFF_HEAD_DIM ** -0.5

    w_in = l0_w_in.astype(BF16)
    cd = CONV_DIM
    wvt = jnp.transpose(l0_w_in[:, 3 * cd + 1024:]).astype(BF16)
    outs0 = [("plain", 0, cd, False, 1.0),
             ("mul", cd, cd, False, 1.0),
             ("heads", 3 * cd, 512, True, qscale),
             ("heads", 3 * cd + 512, 512, True, 1.0)]
    w_main = w_in[:, :3 * cd + 1024]
    bg_l, cx_l, q_l, k_l, vt_l = _proj(x, mods0, lat_row, l0_norm1_g, w_main, outs0, tables, wvt)
    bg_c, cx_c, q_c, k_c, vt_c = _proj(ctx, mods0, ctx_row, l0_norm1_g, w_main, outs0, None, wvt)
    lams = (l0_lambda_q1, l0_lambda_k1, l0_lambda_q2, l0_lambda_k2)
    o_l = _diff_attn(q_l, [k_c, k_l], [vt_c, vt_l], lams, l0_subln_g)
    o_c = _diff_attn(q_c, [k_c], [vt_c], lams, l0_subln_g)

    w_out0 = l0_w_out.astype(BF16)
    rw0 = _router_weights(l0_router_w)
    wg0, wu0, wd0 = (w.astype(BF16) for w in (l0_exp_w_gate, l0_exp_w_up, l0_exp_w_down))
    x1, haug, aff_t = _outproj(x, mods0, lat_row, w_out0, l0_norm2_g, *rw0, o_l,
                               conv_args=(bg_l, cx_l, l0_conv_w))
    x = _moe(x1, haug, aff_t, mods0, lat_row, wg0, wu0, wd0, CAPACITY_FACTOR * t // N_EXPERTS)
    c1, haug_c, aff_tc = _outproj(ctx, mods0, ctx_row, w_out0, l0_norm2_g, *rw0, o_c,
                                  conv_args=(bg_c, cx_c, l0_conv_w))
    ctx = _moe(c1, haug_c, aff_tc, mods0, ctx_row, wg0, wu0, wd0, b * (CAPACITY_FACTOR * nctx // N_EXPERTS))

    qw = SWA_HEADS * SWA_HEAD_DIM
    kvw = SWA_KV_HEADS * SWA_HEAD_DIM
    wq = l1_w_qkv[:, :qw]
    wk = l1_w_qkv[:, qw:qw + kvw].reshape(d, SWA_KV_HEADS, SWA_HEAD_DIM)
    wv = l1_w_qkv[:, qw + kvw:].reshape(d, SWA_KV_HEADS, SWA_HEAD_DIM)
    wk2 = jnp.concatenate([wk, wk], axis=2).reshape(d, 2 * kvw)
    wv2 = jnp.concatenate([wv, jnp.zeros_like(wv)], axis=2).reshape(d, 2 * kvw)
    w_qkv = jnp.concatenate([wq, wk2, wv2], axis=1).astype(BF16)
    outs1 = [("plain", 0, qw, True, SWA_HEAD_DIM ** -0.5),
             ("plain", qw, 2 * kvw, True, 1.0),
             ("plain", qw + 2 * kvw, 2 * kvw, False, 1.0)]
    q1, k1, v1 = _proj(x, mods1, lat_row, l1_norm1_g, w_qkv, outs1, tables)
    outs1c = [("plain", 0, 2 * kvw, False, 1.0), ("plain", 2 * kvw, 2 * kvw, False, 1.0)]
    k1c, v1c = _proj(ctx, mods1, ctx_row, l1_norm1_g, w_qkv[:, qw:], outs1c)
    o1 = _swa_attn(l1_sink, q1, k1, v1, k1c, v1c)

    rw1 = _router_weights(l1_router_w)
    wg1, wu1, wd1 = (w.astype(BF16) for w in (l1_exp_w_gate, l1_exp_w_up, l1_exp_w_down))
    x1, haug, aff_t = _outproj(x, mods1, lat_row, l1_w_out.astype(BF16), l1_norm2_g, *rw1, o1)
    return _moe(x1, haug, aff_t, mods1, lat_row, wg1, wu1, wd1, CAPACITY_FACTOR * t // N_EXPERTS,
                final_g=final_norm_g)
```

```python
import functools
import math

import jax
import jax.numpy as jnp
from jax import lax
from jax.experimental import pallas as pl
from jax.experimental.pallas import tpu as pltpu

F32 = jnp.float32
BF16 = jnp.bfloat16
I32 = jnp.int32

GRID_W = 64
ROPE_BASE = 10000.0
NORM_EPS = 1e-6
NEG_INF = -1e30
N_MOD = 6

CONV_DIM = 512
DIFF_HEADS = 4
DIFF_HEAD_DIM = 64
DIFF_V_DIM = 128
LAM_INIT0 = 0.8 - 0.6 * math.exp(-0.3 * 0)

SWA_HEADS = 16
SWA_KV_HEADS = 4
SWA_HEAD_DIM = 64
SWA_WINDOW = 128

N_EXPERTS = 16
CAPACITY_FACTOR = 2

LANES = 128
SUBLANES = 8
MOD_ROWS = 16
AUG = LANES
ROW_TILE = 256
COMB_W = 64
BF16_ROWS = 16
VT_ROWS = DIFF_V_DIM
DIFF_HEADS_PER_STEP = 2
LOG2E = math.log2(math.e)
VMEM_LIMIT = 56 * 1024 * 1024


def _cparams(sem):
    return pltpu.CompilerParams(dimension_semantics=sem, vmem_limit_bytes=VMEM_LIMIT)


def _dot(a, b):
    return jnp.dot(a, b, preferred_element_type=F32)


def _dot_nt(a, b):
    return lax.dot_general(a, b, (((1,), (1,)), ((), ())), preferred_element_type=F32)


def _modulate(x, gain, shift, scale):
    ms = jnp.mean(x * x, axis=-1, keepdims=True)
    return x * lax.rsqrt(ms + NORM_EPS) * (gain * (1.0 + scale)) + shift


def _ada_kernel(c_ref, w_ref, b_ref, o_ref):
    c = c_ref[...]
    s = (c * jax.nn.sigmoid(c)).astype(BF16)
    o_ref[...] = _dot(s, w_ref[...].astype(BF16)) + b_ref[...]


def _ada(cc, w, b):
    d, n = w.shape
    tn = 512
    return pl.pallas_call(
        _ada_kernel,
        grid=(n // tn,),
        in_specs=[pl.BlockSpec((MOD_ROWS, d), lambda j: (0, 0)),
                  pl.BlockSpec((d, tn), lambda j: (0, j)),
                  pl.BlockSpec((1, tn), lambda j: (0, j))],
        out_specs=pl.BlockSpec((MOD_ROWS, tn), lambda j: (0, j)),
        out_shape=jax.ShapeDtypeStruct((MOD_ROWS, n), F32),
        compiler_params=_cparams(("arbitrary",)),
        name="ada",
    )(cc, w, b.reshape(1, n))


def _rope_tables(t):
    n_freq = DIFF_HEAD_DIM // 4
    inv_freq = ROPE_BASE ** (-jnp.arange(n_freq, dtype=F32) / n_freq)
    pos = jnp.arange(t, dtype=I32)
    row = (pos // GRID_W).astype(F32)[:, None] * inv_freq
    col = (pos % GRID_W).astype(F32)[:, None] * inv_freq
    zeros = jnp.zeros_like(row)
    cos64 = jnp.concatenate([jnp.cos(row), jnp.cos(row), jnp.cos(col), jnp.cos(col)], axis=1)
    sa64 = jnp.concatenate([-jnp.sin(row), zeros, -jnp.sin(col), zeros], axis=1)
    sb64 = jnp.concatenate([zeros, jnp.sin(row), zeros, jnp.sin(col)], axis=1)
    rep = lambda a: jnp.concatenate([a, a], axis=1)
    return rep(cos64), rep(sa64), rep(sb64)


def _rope128(c, cos, sa, sb):
    return c * cos + pltpu.roll(c, LANES - 16, 1) * sa + pltpu.roll(c, 16, 1) * sb


def _proj_kernel(*refs, outs, rope, has_vt):
    x_ref, mod_ref, g_ref, w_ref = refs[:4]
    pos = 4
    if rope:
        cos_ref, sa_ref, sb_ref = refs[pos:pos + 3]
        pos += 3
    if has_vt:
        wvt_ref = refs[pos]
        pos += 1
    out_refs = refs[pos:]
    m = mod_ref[0]
    h = _modulate(x_ref[0], g_ref[...], m[0:1, :], m[1:2, :]).astype(BF16)
    y = _dot(h, w_ref[...])
    for (kind, c0, width, do_rope, scale), o_ref in zip(outs, out_refs):
        for j in range(width // LANES):
            c = y[:, c0 + j * LANES:c0 + (j + 1) * LANES]
            if kind == "mul":
                c = c * y[:, c0 + width + j * LANES:c0 + width + (j + 1) * LANES]
            if do_rope and rope:
                c = _rope128(c, cos_ref[...], sa_ref[...], sb_ref[...])
            if scale != 1.0:
                c = c * scale
            if kind == "heads":
                o_ref[0, j] = c.astype(o_ref.dtype)
            else:
                o_ref[0, :, j * LANES:(j + 1) * LANES] = c.astype(o_ref.dtype)
    if has_vt:
        vt = _dot_nt(wvt_ref[...], h)
        o_ref = out_refs[len(outs)]
        for j in range(vt.shape[0] // LANES):
            o_ref[0, j] = vt[j * LANES:(j + 1) * LANES, :].astype(o_ref.dtype)


def _proj(x, mods, mod_row, gain, w, outs, tables=None, wvt=None):
    b, t, d = x.shape
    tm = min(ROW_TILE, t)
    n = w.shape[1]
    rope = tables is not None
    in_specs = [pl.BlockSpec((1, tm, d), lambda bi, i: (bi, i, 0)),
                pl.BlockSpec((1, N_MOD, d), lambda bi, i: (mod_row(bi), 0, 0)),
                pl.BlockSpec((1, d), lambda bi, i: (0, 0)),
                pl.BlockSpec((d, n), lambda bi, i: (0, 0))]
    args = [x, mods, gain.reshape(1, d), w]
    if rope:
        in_specs += [pl.BlockSpec((tm, LANES), lambda bi, i: (i, 0))] * 3
        args += list(tables)
    if wvt is not None:
        in_specs.append(pl.BlockSpec(wvt.shape, lambda bi, i: (0, 0)))
        args.append(wvt)
    out_specs, out_shapes = [], []
    for (kind, c0, width, do_rope, scale) in outs:
        if kind == "heads":
            nh = width // LANES
            out_specs.append(pl.BlockSpec((1, nh, tm, LANES), lambda bi, i: (bi, 0, i, 0)))
            out_shapes.append(jax.ShapeDtypeStruct((b, nh, t, LANES), BF16))
        else:
            out_specs.append(pl.BlockSpec((1, tm, width), lambda bi, i: (bi, i, 0)))
            out_shapes.append(jax.ShapeDtypeStruct((b, t, width), BF16))
    if wvt is not None:
        nh = wvt.shape[0] // LANES
        out_specs.append(pl.BlockSpec((1, nh, VT_ROWS, tm), lambda bi, i: (bi, 0, 0, i)))
        out_shapes.append(jax.ShapeDtypeStruct((b, nh, VT_ROWS, t), BF16))
    return pl.pallas_call(
        functools.partial(_proj_kernel, outs=tuple(outs), rope=rope, has_vt=wvt is not None),
        grid=(b, t // tm),
        in_specs=in_specs,
        out_specs=out_specs,
        out_shape=out_shapes,
        compiler_params=_cparams(("parallel", "arbitrary")),
        name="proj",
    )(*args)


def _diff_attn_kernel(*refs, nseg, hps):
    q_ref = refs[0]
    k_refs = refs[1:1 + nseg]
    vt_refs = refs[1 + nseg:1 + 2 * nseg]
    lq1, lk1, lq2, lk2, g_ref, o_ref = refs[1 + 2 * nseg:]
    lam = (jnp.exp(jnp.sum(lq1[...] * lk1[...], axis=1, keepdims=True))
           - jnp.exp(jnp.sum(lq2[...] * lk2[...], axis=1, keepdims=True)) + LAM_INIT0)
    for hh in range(hps):
        q = q_ref[0, hh]
        lane = lax.broadcasted_iota(I32, q.shape, 1)
        zero = jnp.zeros_like(q)
        qs = (jnp.where(lane < DIFF_HEAD_DIM, q, zero), jnp.where(lane >= DIFF_HEAD_DIM, q, zero))
        probs, norms = [], []
        for qh in qs:
            s = [_dot_nt(k[0, hh], qh) for k in k_refs]
            m = functools.reduce(jnp.maximum, [jnp.max(x, axis=0, keepdims=True) for x in s])
            p = [jnp.exp2(x - m) for x in s]
            norms.append(functools.reduce(jnp.add, [jnp.sum(x, axis=0, keepdims=True) for x in p]))
            probs.append(p)
        c2 = lam * norms[0] / norms[1]
        ot = None
        for si in range(nseg):
            a = (probs[0][si] - probs[1][si] * c2).astype(BF16)
            part = _dot(vt_refs[si][0, hh], a)
            ot = part if ot is None else ot + part
        ot = ot * (1.0 / norms[0])
        ms = jnp.mean(ot * ot, axis=0, keepdims=True)
        on = ot * lax.rsqrt(ms + NORM_EPS) * (g_ref[...] * (1.0 - LAM_INIT0))
        o_ref[0, hh] = on.T.astype(o_ref.dtype)


def _diff_attn(q, ks, vts, lams, subln_g):
    b, nh, t, _ = q.shape
    tq = min(ROW_TILE, t)
    nseg = len(ks)
    hps = DIFF_HEADS_PER_STEP
    in_specs = [pl.BlockSpec((1, hps, tq, LANES), lambda bi, h, i: (bi, h, i, 0))]
    for k in ks:
        in_specs.append(pl.BlockSpec((1, hps, k.shape[2], LANES), lambda bi, h, i: (bi, h, 0, 0)))
    for vt in vts:
        in_specs.append(pl.BlockSpec((1, hps, VT_ROWS, vt.shape[3]), lambda bi, h, i: (bi, h, 0, 0)))
    in_specs += [pl.BlockSpec((1, DIFF_HEAD_DIM), lambda bi, h, i: (0, 0))] * 4
    in_specs.append(pl.BlockSpec((DIFF_V_DIM, 1), lambda bi, h, i: (0, 0)))
    return pl.pallas_call(
        functools.partial(_diff_attn_kernel, nseg=nseg, hps=hps),
        grid=(b, nh // hps, t // tq),
        in_specs=in_specs,
        out_specs=pl.BlockSpec((1, hps, tq, LANES), lambda bi, h, i: (bi, h, i, 0)),
        out_shape=jax.ShapeDtypeStruct((b, nh, t, LANES), BF16),
        compiler_params=_cparams(("parallel", "arbitrary", "arbitrary")),
        name="diff_attn",
    )(q, *ks, *vts, *[l.reshape(1, DIFF_HEAD_DIM) for l in lams], subln_g.reshape(DIFF_V_DIM, 1))


def _swa_kernel(sink_ref, q_ref, kl_ref, vl_ref, kc_ref, vc_ref, o_ref, *, t, tq):
    span = tq + 2 * SWA_WINDOW
    t0 = pl.program_id(1) * tq
    ws = pl.multiple_of(jnp.clip(t0 - SWA_WINDOW, 0, t - span), LANES)
    kwin = kl_ref[0, pl.ds(ws, span), :]
    vwin = vl_ref[0, pl.ds(ws, span), :]
    kc = kc_ref[0]
    vc = vc_ref[0]
    qpos = t0 + lax.broadcasted_iota(I32, (tq, span), 0)
    kpos = ws + lax.broadcasted_iota(I32, (tq, span), 1)
    band = jnp.abs(qpos - kpos) <= SWA_WINDOW
    lane = lax.broadcasted_iota(I32, (tq, LANES), 1)
    group = SWA_HEADS // SWA_KV_HEADS
    for n in range(SWA_KV_HEADS):
        sl = slice(n * LANES, (n + 1) * LANES)
        kn, vn, kcn, vcn = kwin[:, sl], vwin[:, sl], kc[:, sl], vc[:, sl]
        for j in range(group // 2):
            ci = n * (group // 2) + j
            qc = q_ref[0, :, ci * LANES:(ci + 1) * LANES]
            zero = jnp.zeros_like(qc)
            halves = []
            for half in range(2):
                hq = n * group + 2 * j + half
                keep = (lane < SWA_HEAD_DIM) if half == 0 else (lane >= SWA_HEAD_DIM)
                qz = jnp.where(keep, qc, zero)
                s_l = jnp.where(band, _dot_nt(qz, kn), NEG_INF)
                s_c = _dot_nt(qz, kcn)
                sink = sink_ref[hq] * LOG2E
                m = jnp.maximum(jnp.maximum(jnp.max(s_l, axis=1, keepdims=True),
                                            jnp.max(s_c, axis=1, keepdims=True)), sink)
                p_l = jnp.exp2(s_l - m)
                p_c = jnp.exp2(s_c - m)
                den = (jnp.sum(p_l, axis=1, keepdims=True) + jnp.sum(p_c, axis=1, keepdims=True)
                       + jnp.exp2(sink - m))
                o = (_dot(p_c.astype(BF16), vcn) + _dot(p_l.astype(BF16), vn)) * (1.0 / den)
                halves.append(o)
            o_ref[0, :, ci * LANES:(ci + 1) * LANES] = (
                halves[0] + pltpu.roll(halves[1], SWA_HEAD_DIM, 1)).astype(o_ref.dtype)


def _swa_attn(sink, q, kl, vl, kc, vc):
    b, t, qw = q.shape
    tq = min(ROW_TILE, t - 2 * SWA_WINDOW)
    kw = kl.shape[2]
    c = kc.shape[1]
    return pl.pallas_call(
        functools.partial(_swa_kernel, t=t, tq=tq),
        grid=(b, t // tq),
        in_specs=[pl.BlockSpec(memory_space=pltpu.SMEM),
                  pl.BlockSpec((1, tq, qw), lambda bi, i: (bi, i, 0)),
                  pl.BlockSpec((1, t, kw), lambda bi, i: (bi, 0, 0)),
                  pl.BlockSpec((1, t, kw), lambda bi, i: (bi, 0, 0)),
                  pl.BlockSpec((1, c, kw), lambda bi, i: (bi, 0, 0)),
                  pl.BlockSpec((1, c, kw), lambda bi, i: (bi, 0, 0))],
        out_specs=pl.BlockSpec((1, tq, qw), lambda bi, i: (bi, i, 0)),
        out_shape=jax.ShapeDtypeStruct((b, t, qw), BF16),
        compiler_params=_cparams(("parallel", "arbitrary")),
        name="swa_attn",
    )(sink, q, kl, vl, kc, vc)


def _outproj_kernel(*refs, conv, tm):
    if conv:
        (x_ref, mod_ref, bg_ref, p_ref, pprev_ref, pnext_ref, cw_ref, o_ref, w_ref,
         g2_ref, rwh_ref, rwl_ref, xn_ref, haug_ref, afft_ref) = refs
        i = pl.program_id(1)
        ni = pl.num_programs(1)
        p = p_ref[0].astype(F32)
        row = lax.broadcasted_iota(I32, p.shape, 0)
        halo_prev = jnp.where(i > 0, pprev_ref[0, 15:16, :].astype(F32), 0.0)
        halo_next = jnp.where(i < ni - 1, pnext_ref[0, 0:1, :].astype(F32), 0.0)
        p_prev = jnp.where(row == 0, halo_prev, pltpu.roll(p, 1, 0))
        p_next = jnp.where(row == tm - 1, halo_next, pltpu.roll(p, tm - 1, 0))
        cw = cw_ref[...]
        cv = p_prev * cw[0:1, :] + p * cw[1:2, :] + p_next * cw[2:3, :]
        u = (bg_ref[0].astype(F32) * cv).astype(BF16)
        lhs = jnp.concatenate([u] + [o_ref[0, j] for j in range(DIFF_HEADS)], axis=1)
    else:
        x_ref, mod_ref, o_ref, w_ref, g2_ref, rwh_ref, rwl_ref, xn_ref, haug_ref, afft_ref = refs
        lhs = o_ref[0]
    m = mod_ref[0]
    xn = x_ref[0] + m[2:3, :] * _dot(lhs, w_ref[...])
    xn_ref[0] = xn
    h = _modulate(xn, g2_ref[...], m[3:4, :], m[4:5, :])
    d = h.shape[1]
    haug_ref[0, :, 0:d] = h
    h_hi = h.astype(BF16)
    h_lo = (h - h_hi.astype(F32)).astype(BF16)
    logits = _dot(h_hi, rwh_ref[...]) + _dot(h_hi, rwl_ref[...]) + _dot(h_lo, rwh_ref[...])
    lane = lax.broadcasted_iota(I32, logits.shape, 1)
    logits = jnp.where(lane < N_EXPERTS, logits, NEG_INF)
    e = jnp.exp(logits - jnp.max(logits, axis=1, keepdims=True))
    aff = e / jnp.sum(e, axis=1, keepdims=True)
    haug_ref[0, :, d:d + AUG] = aff
    afft_ref[0] = aff.T[0:N_EXPERTS, :]


def _outproj(x, mods, mod_row, w_out, n2g, rw_hi, rw_lo, o, conv_args=None):
    b, t, d = x.shape
    tm = min(ROW_TILE, t)
    conv = conv_args is not None
    xmap = lambda bi, i: (bi, i, 0)
    in_specs = [pl.BlockSpec((1, tm, d), xmap),
                pl.BlockSpec((1, N_MOD, d), lambda bi, i: (mod_row(bi), 0, 0))]
    args = [x, mods]
    if conv:
        bg, p, cw = conv_args
        hb = tm // 16
        nhb = t // 16
        in_specs += [pl.BlockSpec((1, tm, CONV_DIM), xmap),
                     pl.BlockSpec((1, tm, CONV_DIM), xmap),
                     pl.BlockSpec((1, 16, CONV_DIM), lambda bi, i: (bi, jnp.maximum(i * hb - 1, 0), 0)),
                     pl.BlockSpec((1, 16, CONV_DIM), lambda bi, i: (bi, jnp.minimum((i + 1) * hb, nhb - 1), 0)),
                     pl.BlockSpec((3, CONV_DIM), lambda bi, i: (0, 0)),
                     pl.BlockSpec((1, DIFF_HEADS, tm, LANES), lambda bi, i: (bi, 0, i, 0))]
        args += [bg, p, p, p, cw, o]
    else:
        in_specs.append(pl.BlockSpec((1, tm, o.shape[2]), xmap))
        args.append(o)
    in_specs += [pl.BlockSpec(w_out.shape, lambda bi, i: (0, 0)),
                 pl.BlockSpec((1, d), lambda bi, i: (0, 0)),
                 pl.BlockSpec((d, LANES), lambda bi, i: (0, 0)),
                 pl.BlockSpec((d, LANES), lambda bi, i: (0, 0))]
    args += [w_out, n2g.reshape(1, d), rw_hi, rw_lo]
    return pl.pallas_call(
        functools.partial(_outproj_kernel, conv=conv, tm=tm),
        grid=(b, t // tm),
        in_specs=in_specs,
        out_specs=[pl.BlockSpec((1, tm, d), xmap),
                   pl.BlockSpec((1, tm, d + AUG), xmap),
                   pl.BlockSpec((1, N_EXPERTS, tm), lambda bi, i: (bi, 0, i))],
        out_shape=[jax.ShapeDtypeStruct((b, t, d), F32),
                   jax.ShapeDtypeStruct((b, t, d + AUG), F32),
                   jax.ShapeDtypeStruct((b, N_EXPERTS, t), F32)],
        compiler_params=_cparams(("parallel", "arbitrary")),
        name="outproj",
    )(*args)


def _route_kernel(aff_ref, idx_ref, post_ref, offs_ref, pinc_ref, *, t, cap, capp, lc):
    nch = t // lc
    ne = N_EXPERTS
    aff = aff_ref[0]

    def search(i, lo):
        cand = lo | lax.shift_left(jnp.int32(1), 30 - i)
        cnt = jnp.sum(jnp.where(aff >= pltpu.bitcast(cand, F32), 1.0, 0.0), axis=1, keepdims=True)
        return jnp.where(cnt >= cap, cand, lo)

    thr_bits = lax.fori_loop(0, 31, search, jnp.zeros((ne, 1), I32))
    thr = pltpu.bitcast(thr_bits, F32)
    gt = aff > thr
    eq = aff == thr
    need = cap - jnp.sum(jnp.where(gt, 1.0, 0.0), axis=1, keepdims=True)

    r_i = lax.broadcasted_iota(I32, (lc, lc), 0)
    c_i = lax.broadcasted_iota(I32, (lc, lc), 1)
    upper = jnp.where(r_i < c_i, 1.0, 0.0).astype(BF16)

    def excl_prefix(x):
        outs, offs = [], []
        carry = jnp.zeros((ne, 1), F32)
        for c in range(nch):
            xc = x[:, c * lc:(c + 1) * lc]
            offs.append(carry)
            outs.append(_dot(xc.astype(BF16), upper) + carry)
            carry = carry + jnp.sum(xc, axis=1, keepdims=True)
        offs.append(carry)
        return jnp.concatenate(outs, axis=1), offs

    eqf = jnp.where(eq, 1.0, 0.0)
    eq_rank, _ = excl_prefix(eqf)
    sel = jnp.logical_or(gt, jnp.logical_and(eq, eq_rank < need))
    self_ = jnp.where(sel, 1.0, 0.0)
    pos, offs = excl_prefix(self_)

    lane = lax.broadcasted_iota(I32, (ne, LANES), 1)
    om = jnp.zeros((ne, LANES), F32)
    for c, o in enumerate(offs):
        om = jnp.where(lane == c, o, om)
    offs_ref[0] = om

    posm = jnp.where(sel, pos, -1.0)
    pad = jnp.full((LANES - ne, lc), -1.0, F32)
    for c in range(nch):
        blk = jnp.concatenate([posm[:, c * lc:(c + 1) * lc], pad], axis=0)
        post_ref[0, c * lc:(c + 1) * lc, :] = blk.T
    pinc = pos + self_
    nt = t // LANES
    for k in range(nt):
        pinc_ref[k] = pinc[:, k * LANES:(k + 1) * LANES]

    sub = lax.broadcasted_iota(I32, (SUBLANES, LANES), 0).astype(F32)
    lane_c = lax.broadcasted_iota(I32, (capp, LANES), 1)
    unroll = min(8, nt)

    def per_expert(e, acc_m):
        cols = []
        for jb in range(capp // LANES):
            def tiles(kk, acc, jb=jb):
                for u in range(unroll):
                    q = pinc_ref[kk * unroll + u, pl.ds(e, 1), :] - sub - float(jb * LANES)
                    acc = acc + jnp.concatenate(
                        [jnp.where(q <= float(SUBLANES * i), 1.0, 0.0) for i in range(LANES // SUBLANES)], axis=0)
                return acc
            acc = lax.fori_loop(0, nt // unroll, tiles, jnp.zeros((LANES, LANES), F32))
            cols.append(jnp.sum(acc, axis=1, keepdims=True))
        col = jnp.concatenate(cols, axis=0)
        return jnp.where(lane_c == e, col, acc_m)

    idx_m = lax.fori_loop(0, ne, per_expert, jnp.zeros((capp, LANES), F32))
    base = pl.program_id(0) * t
    idx_ref[0] = idx_m.T[0:ne, :].astype(I32) + base


def _route(aff_t, cap):
    b, ne, t = aff_t.shape
    lc = min(256, t)
    capp = max(cap, LANES)
    nch = t // lc
    return pl.pallas_call(
        functools.partial(_route_kernel, t=t, cap=cap, capp=capp, lc=lc),
        grid=(b,),
        in_specs=[pl.BlockSpec((1, ne, t), lambda bi: (bi, 0, 0))],
        out_specs=[pl.BlockSpec((1, ne, capp), lambda bi: (bi, 0, 0)),
                   pl.BlockSpec((1, t, LANES), lambda bi: (bi, 0, 0)),
                   pl.BlockSpec((1, ne, LANES), lambda bi: (bi, 0, 0))],
        out_shape=[jax.ShapeDtypeStruct((b, ne, capp), I32),
                   jax.ShapeDtypeStruct((b, t, LANES), F32),
                   jax.ShapeDtypeStruct((b, ne, LANES), F32)],
        scratch_shapes=[pltpu.VMEM((t // LANES, ne, LANES), F32)],
        compiler_params=_cparams(("arbitrary",)),
        name="route",
    )(aff_t)


def _ffn_kernel(idx_ref, haug_ref, wg_ref, wu_ref, wd_ref, y_ref, hbuf, gsem, *, rows, d):
    e = pl.program_id(0)
    step = e * pl.num_programs(1) + pl.program_id(1)
    nsteps = pl.num_programs(0) * pl.num_programs(1)
    slot = lax.rem(step, 2)

    def row_copy(src_row, dst_slot, dst_row):
        return pltpu.make_async_copy(haug_ref.at[pl.ds(src_row, 1)],
                                     hbuf.at[dst_slot, pl.ds(dst_row, 1)], gsem.at[dst_slot])

    def wait_rows(sl):
        pltpu.make_async_copy(haug_ref.at[pl.ds(0, rows)], hbuf.at[sl], gsem.at[sl]).wait()

    @pl.when(step == 0)
    def _():
        def body(r, carry):
            row_copy(idx_ref[r], 0, r).start()
            return carry
        lax.fori_loop(0, rows, body, 0)

    base = jnp.minimum(step + 1, nsteps - 1) * rows
    for r in range(rows):
        row_copy(idx_ref[base + r], 1 - slot, r).start()

    wait_rows(slot)
    hrow = hbuf[slot]
    hs = hrow[:, 0:d].astype(BF16)
    lane = lax.broadcasted_iota(I32, (rows, AUG), 1)
    gate = jnp.sum(jnp.where(lane == e, hrow[:, d:d + AUG], 0.0), axis=1, keepdims=True)
    a = _dot(hs, wg_ref[0])
    u = _dot(hs, wu_ref[0])
    hm = (a * jax.nn.sigmoid(a) * u).astype(BF16)
    y_ref[0] = (_dot(hm, wd_ref[0]) * gate).astype(y_ref.dtype)

    @pl.when(step == nsteps - 1)
    def _():
        wait_rows(1 - slot)


def _ffn(idx_flat, haug, wg, wu, wd, rows):
    ne, d, f = wg.shape
    n_rows_total = idx_flat.shape[0]
    steps = n_rows_total // (ne * rows)
    grid_spec = pltpu.PrefetchScalarGridSpec(
        num_scalar_prefetch=1,
        grid=(ne, steps),
        in_specs=[pl.BlockSpec(memory_space=pl.ANY),
                  pl.BlockSpec((1, d, f), lambda e, s, idx: (e, 0, 0)),
                  pl.BlockSpec((1, d, f), lambda e, s, idx: (e, 0, 0)),
                  pl.BlockSpec((1, f, d), lambda e, s, idx: (e, 0, 0))],
        out_specs=pl.BlockSpec((1, rows, d), lambda e, s, idx: (e, s, 0)),
        scratch_shapes=[pltpu.VMEM((2, rows, d + AUG), F32),
                        pltpu.SemaphoreType.DMA((2,))],
    )
    return pl.pallas_call(
        functools.partial(_ffn_kernel, rows=rows, d=d),
        grid_spec=grid_spec,
        out_shape=jax.ShapeDtypeStruct((ne, steps * rows, d), BF16),
        compiler_params=_cparams(("arbitrary", "arbitrary")),
        name="ffn",
    )(idx_flat, haug, wg, wu, wd)


def _combine_kernel(ws_ref, nr_ref, x_ref, mod_ref, post_ref, fg_ref, y_ref, o_ref, stage, acc, sem,
                    *, cap, tm, final_norm):
    nchunk = pl.num_programs(1)
    step = pl.program_id(0) * nchunk + pl.program_id(1)
    nsteps = pl.num_programs(0) * nchunk
    slot = lax.rem(step, 2)
    ne = N_EXPERTS
    w = COMB_W
    per = LANES // w
    total_rows = y_ref.shape[1]
    lane = lax.broadcasted_iota(I32, (tm, LANES), 1)

    def window(st, e, r):
        sample = lax.div(st, nchunk)
        nominal = ws_ref[st * ne + e] + r * w
        grow = pl.multiple_of(jnp.minimum(sample * cap + nominal, total_rows - w), BF16_ROWS)
        return nominal, grow, grow - sample * cap

    def copy(e, grow, sl, si):
        return pltpu.make_async_copy(y_ref.at[e, pl.ds(grow, w)], stage.at[sl, pl.ds(e * w, w)], sem.at[si])

    def fetch(st, r, sl, si):
        for e in range(ne):
            copy(e, window(st, e, r)[1], sl, si).start()

    def wait(sl, si):
        for e in range(ne):
            copy(e, 0, sl, si).wait()

    def gathered(st, r, sl):
        pt = post_ref[0]
        blocks = []
        for g in range(ne // per):
            tgt = jnp.zeros((tm, LANES), F32)
            low = jnp.zeros((tm, LANES), F32)
            mine = jnp.zeros((tm, LANES), F32)
            for k in range(per):
                e = g * per + k
                nominal, _, first = window(st, e, r)
                inb = jnp.logical_and(lane >= k * w, lane < (k + 1) * w)
                tgt = jnp.where(inb, (first - k * w + lane).astype(F32), tgt)
                low = jnp.where(inb, nominal.astype(F32), low)
                mine = jnp.where(inb, pt[:, e:e + 1], mine)
            hit = jnp.logical_and(mine == tgt, tgt >= low)
            blocks.append(jnp.where(hit, 1.0, 0.0).astype(BF16))
        onehot = jnp.concatenate(blocks, axis=1)
        return _dot(onehot, stage[sl])

    @pl.when(step == 0)
    def _():
        fetch(0, 0, 0, 0)

    @pl.when(step + 1 < nsteps)
    def _():
        fetch(step + 1, 0, 1 - slot, 1 - slot)

    wait(slot, slot)
    acc[...] = gathered(step, 0, slot)

    def extra_round(r, carry):
        fetch(step, r, slot, 2)
        wait(slot, 2)
        acc[...] += gathered(step, r, slot)
        return carry

    lax.fori_loop(1, nr_ref[step], extra_round, 0)
    m = mod_ref[0]
    xn = x_ref[0] + m[5:6, :] * acc[...]
    if final_norm:
        ms = jnp.mean(xn * xn, axis=-1, keepdims=True)
        xn = xn * lax.rsqrt(ms + NORM_EPS) * fg_ref[...]
    o_ref[0] = xn


def _combine(ws_flat, nr_flat, x, mods, mod_row, post, y, cap, final_g=None):
    b, t, d = x.shape
    tm = min(ROW_TILE, t)
    ne = N_EXPERTS
    final_norm = final_g is not None
    fg = (final_g if final_norm else jnp.ones((d,), F32)).reshape(1, d)
    grid_spec = pltpu.PrefetchScalarGridSpec(
        num_scalar_prefetch=2,
        grid=(b, t // tm),
        in_specs=[pl.BlockSpec((1, tm, d), lambda bi, i, a, c: (bi, i, 0)),
                  pl.BlockSpec((1, N_MOD, d), lambda bi, i, a, c: (mod_row(bi), 0, 0)),
                  pl.BlockSpec((1, tm, LANES), lambda bi, i, a, c: (bi, i, 0)),
                  pl.BlockSpec((1, d), lambda bi, i, a, c: (0, 0)),
                  pl.BlockSpec(memory_space=pl.ANY)],
        out_specs=pl.BlockSpec((1, tm, d), lambda bi, i, a, c: (bi, i, 0)),
        scratch_shapes=[pltpu.VMEM((2, ne * COMB_W, d), BF16),
                        pltpu.VMEM((tm, d), F32),
                        pltpu.SemaphoreType.DMA((3,))],
    )
    return pl.pallas_call(
        functools.partial(_combine_kernel, cap=cap, tm=tm, final_norm=final_norm),
        grid_spec=grid_spec,
        out_shape=jax.ShapeDtypeStruct((b, t, d), F32),
        compiler_params=_cparams(("arbitrary", "arbitrary")),
        name="combine",
    )(ws_flat, nr_flat, x, mods, post, fg, y)


def _moe(xn, haug, aff_t, mods, mod_row, wg, wu, wd, rows_per_step, final_g=None):
    b, t, d = xn.shape
    ne = N_EXPERTS
    cap = CAPACITY_FACTOR * t // ne
    idx, post, offs = _route(aff_t, cap)
    idx_flat = jnp.transpose(idx[:, :, :cap], (1, 0, 2)).reshape(-1)
    y = _ffn(idx_flat, haug.reshape(b * t, d + AUG), wg, wu, wd, rows_per_step)
    tm = min(ROW_TILE, t)
    nchunk = t // tm
    lc = min(256, t)
    per = tm // lc
    offs_i = offs.astype(I32)[:, :, 0:t // lc + 1:per]
    start = jnp.transpose(offs_i[:, :, :nchunk], (0, 2, 1))
    end = jnp.transpose(offs_i[:, :, 1:], (0, 2, 1))
    ws = (start // BF16_ROWS) * BF16_ROWS
    nr = jnp.maximum(jnp.max((end - ws + COMB_W - 1) // COMB_W, axis=2), 1)
    return _combine(ws.reshape(-1), nr.reshape(-1), xn, mods, mod_row, post, y, cap, final_g)


def _split_hi_lo(w):
    hi = w.astype(BF16)
    lo = (w - hi.astype(F32)).astype(BF16)
    return hi, lo


def _router_weights(rw):
    d, ne = rw.shape
    pad = jnp.zeros((d, LANES - ne), F32)
    return _split_hi_lo(jnp.concatenate([rw, pad], axis=1))


def kernel(x, c, ctx, c_ctx, l0_ada_w, l0_ada_b, l0_norm1_g, l0_w_in, l0_conv_w, l0_lambda_q1, l0_lambda_k1, l0_lambda_q2, l0_lambda_k2, l0_subln_g, l0_w_out, l0_norm2_g, l0_router_w, l0_exp_w_gate, l0_exp_w_up, l0_exp_w_down, l1_ada_w, l1_ada_b, l1_norm1_g, l1_w_qkv, l1_sink, l1_w_out, l1_norm2_g, l1_router_w, l1_exp_w_gate, l1_exp_w_up, l1_exp_w_down, final_norm_g):
    b, t, d = x.shape
    nctx = ctx.shape[1]
    assert b + 1 <= MOD_ROWS
    lat_row = lambda bi: bi
    ctx_row = lambda bi: b

    cc = jnp.zeros((MOD_ROWS, d), F32).at[:b].set(c).at[b].set(c_ctx)
    mods0 = _ada(cc, l0_ada_w, l0_ada_b).reshape(MOD_ROWS, N_MOD, d)
    mods1 = _ada(cc, l1_ada_w, l1_ada_b).reshape(MOD_ROWS, N_MOD, d)
    tables = _rope_tables(t)
    qscale = LOG2E * DIFF_HEAD_DIM ** -0.5

    w_in = l0_w_in.astype(BF16)
    cd = CONV_DIM
    wvt = jnp.transpose(l0_w_in[:, 3 * cd + 1024:]).astype(BF16)
    outs0 = [("plain", 0, cd, False, 1.0),
             ("mul", cd, cd, False, 1.0),
             ("heads", 3 * cd, 512, True, qscale),
             ("heads", 3 * cd + 512, 512, True, 1.0)]
    w_main = w_in[:, :3 * cd + 1024]
    bg_l, cx_l, q_l, k_l, vt_l = _proj(x, mods0, lat_row, l0_norm1_g, w_main, outs0, tables, wvt)
    bg_c, cx_c, q_c, k_c, vt_c = _proj(ctx, mods0, ctx_row, l0_norm1_g, w_main, outs0, None, wvt)
    lams = (l0_lambda_q1, l0_lambda_k1, l0_lambda_q2, l0_lambda_k2)
    o_l = _diff_attn(q_l, [k_c, k_l], [vt_c, vt_l], lams, l0_subln_g)
    o_c = _diff_attn(q_c, [k_c], [vt_c], lams, l0_subln_g)

    w_out0 = l0_w_out.astype(BF16)
    rw0 = _router_weights(l0_router_w)
    wg0, wu0, wd0 = (w.astype(BF16) for w in (l0_exp_w_gate, l0_exp_w_up, l0_exp_w_down))
    x1, haug, aff_t = _outproj(x, mods0, lat_row, w_out0, l0_norm2_g, *rw0, o_l,
                               conv_args=(bg_l, cx_l, l0_conv_w))
    x = _moe(x1, haug, aff_t, mods0, lat_row, wg0, wu0, wd0, CAPACITY_FACTOR * t // N_EXPERTS)
    c1, haug_c, aff_tc = _outproj(ctx, mods0, ctx_row, w_out0, l0_norm2_g, *rw0, o_c,
                                  conv_args=(bg_c, cx_c, l0_conv_w))
    ctx = _moe(c1, haug_c, aff_tc, mods0, ctx_row, wg0, wu0, wd0, b * (CAPACITY_FACTOR * nctx // N_EXPERTS))

    qw = SWA_HEADS * SWA_HEAD_DIM
    kvw = SWA_KV_HEADS * SWA_HEAD_DIM
    wq = l1_w_qkv[:, :qw]
    wk = l1_w_qkv[:, qw:qw + kvw].reshape(d, SWA_KV_HEADS, SWA_HEAD_DIM)
    wv = l1_w_qkv[:, qw + kvw:].reshape(d, SWA_KV_HEADS, SWA_HEAD_DIM)
    wk2 = jnp.concatenate([wk, wk], axis=2).reshape(d, 2 * kvw)
    wv2 = jnp.concatenate([wv, jnp.zeros_like(wv)], axis=2).reshape(d, 2 * kvw)
    w_qkv = jnp.concatenate([wq, wk2, wv2], axis=1).astype(BF16)
    outs1 = [("plain", 0, qw, True, LOG2E * SWA_HEAD_DIM ** -0.5),
             ("plain", qw, 2 * kvw, True, 1.0),
             ("plain", qw + 2 * kvw, 2 * kvw, False, 1.0)]
    q1, k1, v1 = _proj(x, mods1, lat_row, l1_norm1_g, w_qkv, outs1, tables)
    outs1c = [("plain", 0, 2 * kvw, False, 1.0), ("plain", 2 * kvw, 2 * kvw, False, 1.0)]
    k1c, v1c = _proj(ctx, mods1, ctx_row, l1_norm1_g, w_qkv[:, qw:], outs1c)
    o1 = _swa_attn(l1_sink, q1, k1, v1, k1c, v1c)

    rw1 = _router_weights(l1_router_w)
    wg1, wu1, wd1 = (w.astype(BF16) for w in (l1_exp_w_gate, l1_exp_w_up, l1_exp_w_down))
    x1, haug, aff_t = _outproj(x, mods1, lat_row, l1_w_out.astype(BF16), l1_norm2_g, *rw1, o1)
    return _moe(x1, haug, aff_t, mods1, lat_row, wg1, wu1, wd1, CAPACITY_FACTOR * t // N_EXPERTS,
                final_g=final_norm_g)
```

```python
import functools
import math

import jax
import jax.numpy as jnp
from jax import lax
from jax.experimental import pallas as pl
from jax.experimental.pallas import tpu as pltpu

F32 = jnp.float32
BF16 = jnp.bfloat16
I32 = jnp.int32

GRID_W = 64
ROPE_BASE = 10000.0
NORM_EPS = 1e-6
NEG_INF = -1e30
N_MOD = 6

CONV_DIM = 512
DIFF_HEADS = 4
DIFF_HEAD_DIM = 64
DIFF_V_DIM = 128
LAM_INIT0 = 0.8 - 0.6 * math.exp(-0.3 * 0)

SWA_HEADS = 16
SWA_KV_HEADS = 4
SWA_HEAD_DIM = 64
SWA_WINDOW = 128

N_EXPERTS = 16
CAPACITY_FACTOR = 2

LANES = 128
SUBLANES = 8
MOD_ROWS = 16
AUG = LANES
ROW_TILE = 256
COMB_W = 64
BF16_ROWS = 16
VT_ROWS = DIFF_V_DIM
DIFF_HEADS_PER_STEP = 2
LOG2E = math.log2(math.e)
VMEM_LIMIT = 56 * 1024 * 1024


def _cparams(sem):
    return pltpu.CompilerParams(dimension_semantics=sem, vmem_limit_bytes=VMEM_LIMIT)


def _dot(a, b):
    return jnp.dot(a, b, preferred_element_type=F32)


def _dot_nt(a, b):
    return lax.dot_general(a, b, (((1,), (1,)), ((), ())), preferred_element_type=F32)


def _modulate(x, gain, shift, scale):
    ms = jnp.mean(x * x, axis=-1, keepdims=True)
    return x * lax.rsqrt(ms + NORM_EPS) * (gain * (1.0 + scale)) + shift


def _ada_kernel(c_ref, w_ref, b_ref, o_ref):
    c = c_ref[...]
    s = (c * jax.nn.sigmoid(c)).astype(BF16)
    o_ref[...] = _dot(s, w_ref[...].astype(BF16)) + b_ref[...]


def _ada(cc, w, b):
    d, n = w.shape
    tn = 512
    return pl.pallas_call(
        _ada_kernel,
        grid=(n // tn,),
        in_specs=[pl.BlockSpec((MOD_ROWS, d), lambda j: (0, 0)),
                  pl.BlockSpec((d, tn), lambda j: (0, j)),
                  pl.BlockSpec((1, tn), lambda j: (0, j))],
        out_specs=pl.BlockSpec((MOD_ROWS, tn), lambda j: (0, j)),
        out_shape=jax.ShapeDtypeStruct((MOD_ROWS, n), F32),
        compiler_params=_cparams(("arbitrary",)),
        name="ada",
    )(cc, w, b.reshape(1, n))


def _rope_tables(t):
    n_freq = DIFF_HEAD_DIM // 4
    inv_freq = ROPE_BASE ** (-jnp.arange(n_freq, dtype=F32) / n_freq)
    pos = jnp.arange(t, dtype=I32)
    row = (pos // GRID_W).astype(F32)[:, None] * inv_freq
    col = (pos % GRID_W).astype(F32)[:, None] * inv_freq
    zeros = jnp.zeros_like(row)
    cos64 = jnp.concatenate([jnp.cos(row), jnp.cos(row), jnp.cos(col), jnp.cos(col)], axis=1)
    sa64 = jnp.concatenate([-jnp.sin(row), zeros, -jnp.sin(col), zeros], axis=1)
    sb64 = jnp.concatenate([zeros, jnp.sin(row), zeros, jnp.sin(col)], axis=1)
    rep = lambda a: jnp.concatenate([a, a], axis=1)
    return rep(cos64), rep(sa64), rep(sb64)


def _rope128(c, cos, sa, sb):
    return c * cos + pltpu.roll(c, LANES - 16, 1) * sa + pltpu.roll(c, 16, 1) * sb


def _proj_kernel(*refs, outs, rope, has_vt):
    x_ref, mod_ref, g_ref, w_ref = refs[:4]
    pos = 4
    if rope:
        cos_ref, sa_ref, sb_ref = refs[pos:pos + 3]
        pos += 3
    if has_vt:
        wvt_ref = refs[pos]
        pos += 1
    out_refs = refs[pos:]
    m = mod_ref[0]
    h = _modulate(x_ref[0], g_ref[...], m[0:1, :], m[1:2, :]).astype(BF16)
    y = _dot(h, w_ref[...])
    for (kind, c0, width, do_rope, scale), o_ref in zip(outs, out_refs):
        for j in range(width // LANES):
            c = y[:, c0 + j * LANES:c0 + (j + 1) * LANES]
            if kind == "mul":
                c = c * y[:, c0 + width + j * LANES:c0 + width + (j + 1) * LANES]
            if do_rope and rope:
                c = _rope128(c, cos_ref[...], sa_ref[...], sb_ref[...])
            if scale != 1.0:
                c = c * scale
            if kind == "heads":
                o_ref[0, j] = c.astype(o_ref.dtype)
            else:
                o_ref[0, :, j * LANES:(j + 1) * LANES] = c.astype(o_ref.dtype)
    if has_vt:
        vt = _dot_nt(wvt_ref[...], h)
        o_ref = out_refs[len(outs)]
        for j in range(vt.shape[0] // LANES):
            o_ref[0, j] = vt[j * LANES:(j + 1) * LANES, :].astype(o_ref.dtype)


def _proj(x, mods, mod_row, gain, w, outs, tables=None, wvt=None):
    b, t, d = x.shape
    tm = min(ROW_TILE, t)
    n = w.shape[1]
    rope = tables is not None
    in_specs = [pl.BlockSpec((1, tm, d), lambda bi, i: (bi, i, 0)),
                pl.BlockSpec((1, N_MOD, d), lambda bi, i: (mod_row(bi), 0, 0)),
                pl.BlockSpec((1, d), lambda bi, i: (0, 0)),
                pl.BlockSpec((d, n), lambda bi, i: (0, 0))]
    args = [x, mods, gain.reshape(1, d), w]
    if rope:
        in_specs += [pl.BlockSpec((tm, LANES), lambda bi, i: (i, 0))] * 3
        args += list(tables)
    if wvt is not None:
        in_specs.append(pl.BlockSpec(wvt.shape, lambda bi, i: (0, 0)))
        args.append(wvt)
    out_specs, out_shapes = [], []
    for (kind, c0, width, do_rope, scale) in outs:
        if kind == "heads":
            nh = width // LANES
            out_specs.append(pl.BlockSpec((1, nh, tm, LANES), lambda bi, i: (bi, 0, i, 0)))
            out_shapes.append(jax.ShapeDtypeStruct((b, nh, t, LANES), BF16))
        else:
            out_specs.append(pl.BlockSpec((1, tm, width), lambda bi, i: (bi, i, 0)))
            out_shapes.append(jax.ShapeDtypeStruct((b, t, width), BF16))
    if wvt is not None:
        nh = wvt.shape[0] // LANES
        out_specs.append(pl.BlockSpec((1, nh, VT_ROWS, tm), lambda bi, i: (bi, 0, 0, i)))
        out_shapes.append(jax.ShapeDtypeStruct((b, nh, VT_ROWS, t), BF16))
    return pl.pallas_call(
        functools.partial(_proj_kernel, outs=tuple(outs), rope=rope, has_vt=wvt is not None),
        grid=(b, t // tm),
        in_specs=in_specs,
        out_specs=out_specs,
        out_shape=out_shapes,
        compiler_params=_cparams(("parallel", "arbitrary")),
        name="proj",
    )(*args)


def _diff_attn_kernel(*refs, nseg):
    q_ref = refs[0]
    k_refs = refs[1:1 + nseg]
    vt_refs = refs[1 + nseg:1 + 2 * nseg]
    lq1, lk1, lq2, lk2, g_ref, o_ref, s_even, m_even, s_odd, m_odd = refs[1 + 2 * nseg:]
    i = pl.program_id(2)
    offs = [0]
    for k in k_refs:
        offs.append(offs[-1] + k.shape[2])

    @pl.when(i == 0)
    def _():
        s_odd[...] = jnp.zeros(s_odd.shape, F32)
        m_odd[...] = jnp.zeros(m_odd.shape, F32)

    def body(s_w, m_w, s_r, m_r):
        q = q_ref[0, 0]
        lane = lax.broadcasted_iota(I32, q.shape, 1)
        zero = jnp.zeros_like(q)
        qs = (jnp.where(lane < DIFF_HEAD_DIM, q, zero), jnp.where(lane >= DIFF_HEAD_DIM, q, zero))
        for half, qh in enumerate(qs):
            m = None
            for si, k in enumerate(k_refs):
                s = _dot_nt(k[0, 0], qh)
                s_w[half, offs[si]:offs[si + 1], :] = s
                ms = jnp.max(s, axis=0, keepdims=True)
                m = ms if m is None else jnp.maximum(m, ms)
            m_w[half] = m

        lam = (jnp.exp(jnp.sum(lq1[...] * lk1[...], axis=1, keepdims=True))
               - jnp.exp(jnp.sum(lq2[...] * lk2[...], axis=1, keepdims=True)) + LAM_INIT0)
        probs, norms = [], []
        for half in range(2):
            p = [jnp.exp2(s_r[half, offs[si]:offs[si + 1], :] - m_r[half]) for si in range(nseg)]
            norms.append(functools.reduce(jnp.add, [jnp.sum(x, axis=0, keepdims=True) for x in p]))
            probs.append(p)
        c2 = lam * norms[0] / norms[1]
        ot = None
        for si in range(nseg):
            a = (probs[0][si] - probs[1][si] * c2).astype(BF16)
            part = _dot(vt_refs[si][0, 0], a)
            ot = part if ot is None else ot + part
        ot = ot * (1.0 / norms[0])
        msq = jnp.mean(ot * ot, axis=0, keepdims=True)
        on = ot * lax.rsqrt(msq + NORM_EPS) * (g_ref[...] * (1.0 - LAM_INIT0))
        o_ref[0, 0] = on.T.astype(o_ref.dtype)

    @pl.when(lax.rem(i, 2) == 0)
    def _():
        body(s_even, m_even, s_odd, m_odd)

    @pl.when(lax.rem(i, 2) == 1)
    def _():
        body(s_odd, m_odd, s_even, m_even)


def _diff_attn(q, ks, vts, lams, subln_g):
    b, nh, t, _ = q.shape
    tq = min(ROW_TILE, t)
    nq = t // tq
    nseg = len(ks)
    tk = sum(k.shape[2] for k in ks)
    in_specs = [pl.BlockSpec((1, 1, tq, LANES), lambda bi, h, i: (bi, h, jnp.minimum(i, nq - 1), 0))]
    for k in ks:
        in_specs.append(pl.BlockSpec((1, 1, k.shape[2], LANES), lambda bi, h, i: (bi, h, 0, 0)))
    for vt in vts:
        in_specs.append(pl.BlockSpec((1, 1, VT_ROWS, vt.shape[3]), lambda bi, h, i: (bi, h, 0, 0)))
    in_specs += [pl.BlockSpec((1, DIFF_HEAD_DIM), lambda bi, h, i: (0, 0))] * 4
    in_specs.append(pl.BlockSpec((DIFF_V_DIM, 1), lambda bi, h, i: (0, 0)))
    return pl.pallas_call(
        functools.partial(_diff_attn_kernel, nseg=nseg),
        grid=(b, nh, nq + 1),
        in_specs=in_specs,
        out_specs=pl.BlockSpec((1, 1, tq, LANES), lambda bi, h, i: (bi, h, jnp.maximum(i - 1, 0), 0)),
        out_shape=jax.ShapeDtypeStruct((b, nh, t, LANES), BF16),
        scratch_shapes=[pltpu.VMEM((2, tk, tq), F32), pltpu.VMEM((2, 1, tq), F32)] * 2,
        compiler_params=_cparams(("parallel", "arbitrary", "arbitrary")),
        name="diff_attn",
    )(q, *ks, *vts, *[l.reshape(1, DIFF_HEAD_DIM) for l in lams], subln_g.reshape(DIFF_V_DIM, 1))


def _swa_kernel(sink_ref, q_ref, kl_ref, vl_ref, kc_ref, vc_ref, o_ref, *, t, tq):
    span = tq + 2 * SWA_WINDOW
    t0 = pl.program_id(1) * tq
    ws = pl.multiple_of(jnp.clip(t0 - SWA_WINDOW, 0, t - span), LANES)
    kwin = kl_ref[0, pl.ds(ws, span), :]
    vwin = vl_ref[0, pl.ds(ws, span), :]
    kc = kc_ref[0]
    vc = vc_ref[0]
    qpos = t0 + lax.broadcasted_iota(I32, (tq, span), 0)
    kpos = ws + lax.broadcasted_iota(I32, (tq, span), 1)
    bias = jnp.where(jnp.abs(qpos - kpos) <= SWA_WINDOW, 0.0, NEG_INF)
    lane = lax.broadcasted_iota(I32, (tq, LANES), 1)
    ones_l = lax.broadcasted_iota(I32, (span, LANES), 1) == SWA_HEAD_DIM
    ones_c = lax.broadcasted_iota(I32, (kc_ref.shape[1], LANES), 1) == SWA_HEAD_DIM
    group = SWA_HEADS // SWA_KV_HEADS
    for n in range(SWA_KV_HEADS):
        sl = slice(n * LANES, (n + 1) * LANES)
        kn, kcn = kwin[:, sl], kc[:, sl]
        vn = jnp.where(ones_l, jnp.ones_like(vwin[:, sl]), vwin[:, sl])
        vcn = jnp.where(ones_c, jnp.ones_like(vc[:, sl]), vc[:, sl])
        for j in range(group // 2):
            ci = n * (group // 2) + j
            qc = q_ref[0, :, ci * LANES:(ci + 1) * LANES]
            zero = jnp.zeros_like(qc)
            halves = []
            for half in range(2):
                hq = n * group + 2 * j + half
                keep = (lane < SWA_HEAD_DIM) if half == 0 else (lane >= SWA_HEAD_DIM)
                qz = jnp.where(keep, qc, zero)
                s_l = _dot_nt(qz, kn) + bias
                s_c = _dot_nt(qz, kcn)
                sink = sink_ref[hq] * LOG2E
                m = jnp.maximum(jnp.maximum(jnp.max(s_l, axis=1, keepdims=True),
                                            jnp.max(s_c, axis=1, keepdims=True)), sink)
                p_l = jnp.exp2(s_l - m).astype(BF16)
                p_c = jnp.exp2(s_c - m).astype(BF16)
                o = _dot(p_c, vcn) + _dot(p_l, vn)
                den = o[:, SWA_HEAD_DIM:SWA_HEAD_DIM + 1] + jnp.exp2(sink - m)
                halves.append(jnp.where(lane < SWA_HEAD_DIM, o * (1.0 / den), 0.0))
            o_ref[0, :, ci * LANES:(ci + 1) * LANES] = (
                halves[0] + pltpu.roll(halves[1], SWA_HEAD_DIM, 1)).astype(o_ref.dtype)


def _swa_attn(sink, q, kl, vl, kc, vc):
    b, t, qw = q.shape
    tq = min(ROW_TILE, t - 2 * SWA_WINDOW)
    kw = kl.shape[2]
    c = kc.shape[1]
    return pl.pallas_call(
        functools.partial(_swa_kernel, t=t, tq=tq),
        grid=(b, t // tq),
        in_specs=[pl.BlockSpec(memory_space=pltpu.SMEM),
                  pl.BlockSpec((1, tq, qw), lambda bi, i: (bi, i, 0)),
                  pl.BlockSpec((1, t, kw), lambda bi, i: (bi, 0, 0)),
                  pl.BlockSpec((1, t, kw), lambda bi, i: (bi, 0, 0)),
                  pl.BlockSpec((1, c, kw), lambda bi, i: (bi, 0, 0)),
                  pl.BlockSpec((1, c, kw), lambda bi, i: (bi, 0, 0))],
        out_specs=pl.BlockSpec((1, tq, qw), lambda bi, i: (bi, i, 0)),
        out_shape=jax.ShapeDtypeStruct((b, t, qw), BF16),
        compiler_params=_cparams(("parallel", "arbitrary")),
        name="swa_attn",
    )(sink, q, kl, vl, kc, vc)


def _outproj_kernel(*refs, conv, tm):
    if conv:
        (x_ref, mod_ref, bg_ref, p_ref, pprev_ref, pnext_ref, cw_ref, o_ref, w_ref,
         g2_ref, rwh_ref, rwl_ref, xn_ref, haug_ref, afft_ref) = refs
        i = pl.program_id(1)
        ni = pl.num_programs(1)
        p = p_ref[0].astype(F32)
        row = lax.broadcasted_iota(I32, p.shape, 0)
        halo_prev = jnp.where(i > 0, pprev_ref[0, 15:16, :].astype(F32), 0.0)
        halo_next = jnp.where(i < ni - 1, pnext_ref[0, 0:1, :].astype(F32), 0.0)
        p_prev = jnp.where(row == 0, halo_prev, pltpu.roll(p, 1, 0))
        p_next = jnp.where(row == tm - 1, halo_next, pltpu.roll(p, tm - 1, 0))
        cw = cw_ref[...]
        cv = p_prev * cw[0:1, :] + p * cw[1:2, :] + p_next * cw[2:3, :]
        u = (bg_ref[0].astype(F32) * cv).astype(BF16)
        lhs = jnp.concatenate([u] + [o_ref[0, j] for j in range(DIFF_HEADS)], axis=1)
    else:
        x_ref, mod_ref, o_ref, w_ref, g2_ref, rwh_ref, rwl_ref, xn_ref, haug_ref, afft_ref = refs
        lhs = o_ref[0]
    m = mod_ref[0]
    xn = x_ref[0] + m[2:3, :] * _dot(lhs, w_ref[...])
    xn_ref[0] = xn
    h = _modulate(xn, g2_ref[...], m[3:4, :], m[4:5, :])
    d = h.shape[1]
    haug_ref[0, :, 0:d] = h
    h_hi = h.astype(BF16)
    h_lo = (h - h_hi.astype(F32)).astype(BF16)
    logits = _dot(h_hi, rwh_ref[...]) + _dot(h_hi, rwl_ref[...]) + _dot(h_lo, rwh_ref[...])
    lane = lax.broadcasted_iota(I32, logits.shape, 1)
    logits = jnp.where(lane < N_EXPERTS, logits, NEG_INF)
    e = jnp.exp(logits - jnp.max(logits, axis=1, keepdims=True))
    aff = e / jnp.sum(e, axis=1, keepdims=True)
    haug_ref[0, :, d:d + AUG] = aff
    afft_ref[0] = aff.T[0:N_EXPERTS, :]


def _outproj(x, mods, mod_row, w_out, n2g, rw_hi, rw_lo, o, conv_args=None):
    b, t, d = x.shape
    tm = min(ROW_TILE, t)
    conv = conv_args is not None
    xmap = lambda bi, i: (bi, i, 0)
    in_specs = [pl.BlockSpec((1, tm, d), xmap),
                pl.BlockSpec((1, N_MOD, d), lambda bi, i: (mod_row(bi), 0, 0))]
    args = [x, mods]
    if conv:
        bg, p, cw = conv_args
        hb = tm // 16
        nhb = t // 16
        in_specs += [pl.BlockSpec((1, tm, CONV_DIM), xmap),
                     pl.BlockSpec((1, tm, CONV_DIM), xmap),
                     pl.BlockSpec((1, 16, CONV_DIM), lambda bi, i: (bi, jnp.maximum(i * hb - 1, 0), 0)),
                     pl.BlockSpec((1, 16, CONV_DIM), lambda bi, i: (bi, jnp.minimum((i + 1) * hb, nhb - 1), 0)),
                     pl.BlockSpec((3, CONV_DIM), lambda bi, i: (0, 0)),
                     pl.BlockSpec((1, DIFF_HEADS, tm, LANES), lambda bi, i: (bi, 0, i, 0))]
        args += [bg, p, p, p, cw, o]
    else:
        in_specs.append(pl.BlockSpec((1, tm, o.shape[2]), xmap))
        args.append(o)
    in_specs += [pl.BlockSpec(w_out.shape, lambda bi, i: (0, 0)),
                 pl.BlockSpec((1, d), lambda bi, i: (0, 0)),
                 pl.BlockSpec((d, LANES), lambda bi, i: (0, 0)),
                 pl.BlockSpec((d, LANES), lambda bi, i: (0, 0))]
    args += [w_out, n2g.reshape(1, d), rw_hi, rw_lo]
    return pl.pallas_call(
        functools.partial(_outproj_kernel, conv=conv, tm=tm),
        grid=(b, t // tm),
        in_specs=in_specs,
        out_specs=[pl.BlockSpec((1, tm, d), xmap),
                   pl.BlockSpec((1, tm, d + AUG), xmap),
                   pl.BlockSpec((1, N_EXPERTS, tm), lambda bi, i: (bi, 0, i))],
        out_shape=[jax.ShapeDtypeStruct((b, t, d), F32),
                   jax.ShapeDtypeStruct((b, t, d + AUG), F32),
                   jax.ShapeDtypeStruct((b, N_EXPERTS, t), F32)],
        compiler_params=_cparams(("parallel", "arbitrary")),
        name="outproj",
    )(*args)


def _route_kernel(aff_ref, idx_ref, post_ref, offs_ref, pinc_ref, *, t, cap, capp, lc):
    nch = t // lc
    ne = N_EXPERTS
    aff = aff_ref[0]

    def search(i, lo):
        cand = lo | lax.shift_left(jnp.int32(1), 30 - i)
        cnt = jnp.sum(jnp.where(aff >= pltpu.bitcast(cand, F32), 1.0, 0.0), axis=1, keepdims=True)
        return jnp.where(cnt >= cap, cand, lo)

    thr_bits = lax.fori_loop(0, 31, search, jnp.zeros((ne, 1), I32))
    thr = pltpu.bitcast(thr_bits, F32)
    gt = aff > thr
    eq = aff == thr
    need = cap - jnp.sum(jnp.where(gt, 1.0, 0.0), axis=1, keepdims=True)

    r_i = lax.broadcasted_iota(I32, (lc, lc), 0)
    c_i = lax.broadcasted_iota(I32, (lc, lc), 1)
    upper = jnp.where(r_i < c_i, 1.0, 0.0).astype(BF16)

    def excl_prefix(x):
        outs, offs = [], []
        carry = jnp.zeros((ne, 1), F32)
        for c in range(nch):
            xc = x[:, c * lc:(c + 1) * lc]
            offs.append(carry)
            outs.append(_dot(xc.astype(BF16), upper) + carry)
            carry = carry + jnp.sum(xc, axis=1, keepdims=True)
        offs.append(carry)
        return jnp.concatenate(outs, axis=1), offs

    eqf = jnp.where(eq, 1.0, 0.0)
    eq_rank, _ = excl_prefix(eqf)
    sel = jnp.logical_or(gt, jnp.logical_and(eq, eq_rank < need))
    self_ = jnp.where(sel, 1.0, 0.0)
    pos, offs = excl_prefix(self_)

    lane = lax.broadcasted_iota(I32, (ne, LANES), 1)
    om = jnp.zeros((ne, LANES), F32)
    for c, o in enumerate(offs):
        om = jnp.where(lane == c, o, om)
    offs_ref[0] = om

    posm = jnp.where(sel, pos, -1.0)
    pad = jnp.full((LANES - ne, lc), -1.0, F32)
    for c in range(nch):
        blk = jnp.concatenate([posm[:, c * lc:(c + 1) * lc], pad], axis=0)
        post_ref[0, c * lc:(c + 1) * lc, :] = blk.T
    pinc = pos + self_
    nt = t // LANES
    for k in range(nt):
        pinc_ref[k] = pinc[:, k * LANES:(k + 1) * LANES]

    sub = lax.broadcasted_iota(I32, (SUBLANES, LANES), 0).astype(F32)
    lane_c = lax.broadcasted_iota(I32, (capp, LANES), 1)
    unroll = min(8, nt)

    def per_expert(e, acc_m):
        cols = []
        for jb in range(capp // LANES):
            def tiles(kk, acc, jb=jb):
                for u in range(unroll):
                    q = pinc_ref[kk * unroll + u, pl.ds(e, 1), :] - sub - float(jb * LANES)
                    acc = acc + jnp.concatenate(
                        [jnp.where(q <= float(SUBLANES * i), 1.0, 0.0) for i in range(LANES // SUBLANES)], axis=0)
                return acc
            acc = lax.fori_loop(0, nt // unroll, tiles, jnp.zeros((LANES, LANES), F32))
            cols.append(jnp.sum(acc, axis=1, keepdims=True))
        col = jnp.concatenate(cols, axis=0)
        return jnp.where(lane_c == e, col, acc_m)

    idx_m = lax.fori_loop(0, ne, per_expert, jnp.zeros((capp, LANES), F32))
    base = pl.program_id(0) * t
    idx_ref[0] = idx_m.T[0:ne, :].astype(I32) + base


def _route(aff_t, cap):
    b, ne, t = aff_t.shape
    lc = min(256, t)
    capp = max(cap, LANES)
    nch = t // lc
    return pl.pallas_call(
        functools.partial(_route_kernel, t=t, cap=cap, capp=capp, lc=lc),
        grid=(b,),
        in_specs=[pl.BlockSpec((1, ne, t), lambda bi: (bi, 0, 0))],
        out_specs=[pl.BlockSpec((1, ne, capp), lambda bi: (bi, 0, 0)),
                   pl.BlockSpec((1, t, LANES), lambda bi: (bi, 0, 0)),
                   pl.BlockSpec((1, ne, LANES), lambda bi: (bi, 0, 0))],
        out_shape=[jax.ShapeDtypeStruct((b, ne, capp), I32),
                   jax.ShapeDtypeStruct((b, t, LANES), F32),
                   jax.ShapeDtypeStruct((b, ne, LANES), F32)],
        scratch_shapes=[pltpu.VMEM((t // LANES, ne, LANES), F32)],
        compiler_params=_cparams(("arbitrary",)),
        name="route",
    )(aff_t)


def _ffn_kernel(idx_ref, haug_ref, wg_ref, wu_ref, wd_ref, y_ref, hbuf, w_in, w_dn, gsem, *, rows, d):
    e = pl.program_id(0)
    step = e * pl.num_programs(1) + pl.program_id(1)
    nsteps = pl.num_programs(0) * pl.num_programs(1)

    @pl.when(pl.program_id(1) == 0)
    def _():
        w_in[0] = wg_ref[0].astype(BF16)
        w_in[1] = wu_ref[0].astype(BF16)
        w_dn[...] = wd_ref[0].astype(BF16)

    def row_copy(src_row, dst_slot, dst_row):
        return pltpu.make_async_copy(haug_ref.at[pl.ds(src_row, 1)],
                                     hbuf.at[dst_slot, pl.ds(dst_row, 1)], gsem.at[dst_slot])

    def wait_rows(sl):
        pltpu.make_async_copy(haug_ref.at[pl.ds(0, rows)], hbuf.at[sl], gsem.at[sl]).wait()

    @pl.when(step == 0)
    def _():
        def body(r, carry):
            row_copy(idx_ref[r], 0, r).start()
            return carry
        lax.fori_loop(0, rows, body, 0)

    def run(slot):
        base = jnp.minimum(step + 1, nsteps - 1) * rows
        for r in range(rows):
            row_copy(idx_ref[base + r], 1 - slot, r).start()

        wait_rows(slot)
        hrow = hbuf[slot]
        hs = hrow[:, 0:d].astype(BF16)
        lane = lax.broadcasted_iota(I32, (rows, AUG), 1)
        gate = jnp.sum(jnp.where(lane == e, hrow[:, d:d + AUG], 0.0), axis=1, keepdims=True)
        a = _dot(hs, w_in[0])
        u = _dot(hs, w_in[1])
        hm = (a * jax.nn.sigmoid(a) * u).astype(BF16)
        y_ref[0] = (_dot(hm, w_dn[...]) * gate).astype(y_ref.dtype)

        @pl.when(step == nsteps - 1)
        def _():
            wait_rows(1 - slot)

    for parity in range(2):
        pl.when(lax.rem(step, 2) == parity)(functools.partial(run, parity))


def _ffn(idx_flat, haug, wg, wu, wd, rows):
    ne, d, f = wg.shape
    n_rows_total = idx_flat.shape[0]
    steps = n_rows_total // (ne * rows)
    grid_spec = pltpu.PrefetchScalarGridSpec(
        num_scalar_prefetch=1,
        grid=(ne, steps),
        in_specs=[pl.BlockSpec(memory_space=pl.ANY),
                  pl.BlockSpec((1, d, f), lambda e, s, idx: (e, 0, 0)),
                  pl.BlockSpec((1, d, f), lambda e, s, idx: (e, 0, 0)),
                  pl.BlockSpec((1, f, d), lambda e, s, idx: (e, 0, 0))],
        out_specs=pl.BlockSpec((1, rows, d), lambda e, s, idx: (e, s, 0)),
        scratch_shapes=[pltpu.VMEM((2, rows, d + AUG), F32),
                        pltpu.VMEM((2, d, f), BF16),
                        pltpu.VMEM((f, d), BF16),
                        pltpu.SemaphoreType.DMA((2,))],
    )
    return pl.pallas_call(
        functools.partial(_ffn_kernel, rows=rows, d=d),
        grid_spec=grid_spec,
        out_shape=jax.ShapeDtypeStruct((ne, steps * rows, d), BF16),
        compiler_params=_cparams(("arbitrary", "arbitrary")),
        name="ffn",
    )(idx_flat, haug, wg, wu, wd)


def _combine_kernel(ws_ref, nr_ref, x_ref, mod_ref, post_ref, fg_ref, y_ref, o_ref, stage, acc, sem,
                    *, cap, tm, final_norm):
    nchunk = pl.num_programs(1)
    step = pl.program_id(0) * nchunk + pl.program_id(1)
    nsteps = pl.num_programs(0) * nchunk
    slot = lax.rem(step, 2)
    ne = N_EXPERTS
    w = COMB_W
    per = LANES // w
    total_rows = y_ref.shape[1]
    lane = lax.broadcasted_iota(I32, (tm, LANES), 1)

    def window(st, e, r):
        sample = lax.div(st, nchunk)
        nominal = ws_ref[st * ne + e] + r * w
        grow = pl.multiple_of(jnp.minimum(sample * cap + nominal, total_rows - w), BF16_ROWS)
        return nominal, grow, grow - sample * cap

    def copy(e, grow, sl, si):
        return pltpu.make_async_copy(y_ref.at[e, pl.ds(grow, w)], stage.at[sl, pl.ds(e * w, w)], sem.at[si])

    def fetch(st, r, sl, si):
        for e in range(ne):
            copy(e, window(st, e, r)[1], sl, si).start()

    def wait(sl, si):
        for e in range(ne):
            copy(e, 0, sl, si).wait()

    def gathered(st, r, sl):
        pt = post_ref[0]
        blocks = []
        for g in range(ne // per):
            tgt = jnp.zeros((tm, LANES), F32)
            low = jnp.zeros((tm, LANES), F32)
            mine = jnp.zeros((tm, LANES), F32)
            for k in range(per):
                e = g * per + k
                nominal, _, first = window(st, e, r)
                inb = jnp.logical_and(lane >= k * w, lane < (k + 1) * w)
                tgt = jnp.where(inb, (first - k * w + lane).astype(F32), tgt)
                low = jnp.where(inb, nominal.astype(F32), low)
                mine = jnp.where(inb, pt[:, e:e + 1], mine)
            hit = jnp.logical_and(mine == tgt, tgt >= low)
            blocks.append(jnp.where(hit, 1.0, 0.0).astype(BF16))
        onehot = jnp.concatenate(blocks, axis=1)
        return _dot(onehot, stage[sl])

    @pl.when(step == 0)
    def _():
        fetch(0, 0, 0, 0)

    @pl.when(step + 1 < nsteps)
    def _():
        fetch(step + 1, 0, 1 - slot, 1 - slot)

    wait(slot, slot)
    acc[...] = gathered(step, 0, slot)

    def extra_round(r, carry):
        fetch(step, r, slot, 2)
        wait(slot, 2)
        acc[...] += gathered(step, r, slot)
        return carry

    lax.fori_loop(1, nr_ref[step], extra_round, 0)
    m = mod_ref[0]
    xn = x_ref[0] + m[5:6, :] * acc[...]
    if final_norm:
        ms = jnp.mean(xn * xn, axis=-1, keepdims=True)
        xn = xn * lax.rsqrt(ms + NORM_EPS) * fg_ref[...]
    o_ref[0] = xn


def _combine(ws_flat, nr_flat, x, mods, mod_row, post, y, cap, final_g=None):
    b, t, d = x.shape
    tm = min(ROW_TILE, t)
    ne = N_EXPERTS
    final_norm = final_g is not None
    fg = (final_g if final_norm else jnp.ones((d,), F32)).reshape(1, d)
    grid_spec = pltpu.PrefetchScalarGridSpec(
        num_scalar_prefetch=2,
        grid=(b, t // tm),
        in_specs=[pl.BlockSpec((1, tm, d), lambda bi, i, a, c: (bi, i, 0)),
                  pl.BlockSpec((1, N_MOD, d), lambda bi, i, a, c: (mod_row(bi), 0, 0)),
                  pl.BlockSpec((1, tm, LANES), lambda bi, i, a, c: (bi, i, 0)),
                  pl.BlockSpec((1, d), lambda bi, i, a, c: (0, 0)),
                  pl.BlockSpec(memory_space=pl.ANY)],
        out_specs=pl.BlockSpec((1, tm, d), lambda bi, i, a, c: (bi, i, 0)),
        scratch_shapes=[pltpu.VMEM((2, ne * COMB_W, d), BF16),
                        pltpu.VMEM((tm, d), F32),
                        pltpu.SemaphoreType.DMA((3,))],
    )
    return pl.pallas_call(
        functools.partial(_combine_kernel, cap=cap, tm=tm, final_norm=final_norm),
        grid_spec=grid_spec,
        out_shape=jax.ShapeDtypeStruct((b, t, d), F32),
        compiler_params=_cparams(("arbitrary", "arbitrary")),
        name="combine",
    )(ws_flat, nr_flat, x, mods, post, fg, y)


def _moe(xn, haug, aff_t, mods, mod_row, wg, wu, wd, rows_per_step, final_g=None):
    b, t, d = xn.shape
    ne = N_EXPERTS
    cap = CAPACITY_FACTOR * t // ne
    idx, post, offs = _route(aff_t, cap)
    idx_flat = jnp.transpose(idx[:, :, :cap], (1, 0, 2)).reshape(-1)
    y = _ffn(idx_flat, haug.reshape(b * t, d + AUG), wg, wu, wd, rows_per_step)
    tm = min(ROW_TILE, t)
    nchunk = t // tm
    lc = min(256, t)
    per = tm // lc
    offs_i = offs.astype(I32)[:, :, 0:t // lc + 1:per]
    start = jnp.transpose(offs_i[:, :, :nchunk], (0, 2, 1))
    end = jnp.transpose(offs_i[:, :, 1:], (0, 2, 1))
    ws = (start // BF16_ROWS) * BF16_ROWS
    nr = jnp.maximum(jnp.max((end - ws + COMB_W - 1) // COMB_W, axis=2), 1)
    return _combine(ws.reshape(-1), nr.reshape(-1), xn, mods, mod_row, post, y, cap, final_g)


def _split_hi_lo(w):
    hi = w.astype(BF16)
    lo = (w - hi.astype(F32)).astype(BF16)
    return hi, lo


def _router_weights(rw):
    d, ne = rw.shape
    pad = jnp.zeros((d, LANES - ne), F32)
    return _split_hi_lo(jnp.concatenate([rw, pad], axis=1))


def kernel(x, c, ctx, c_ctx, l0_ada_w, l0_ada_b, l0_norm1_g, l0_w_in, l0_conv_w, l0_lambda_q1, l0_lambda_k1, l0_lambda_q2, l0_lambda_k2, l0_subln_g, l0_w_out, l0_norm2_g, l0_router_w, l0_exp_w_gate, l0_exp_w_up, l0_exp_w_down, l1_ada_w, l1_ada_b, l1_norm1_g, l1_w_qkv, l1_sink, l1_w_out, l1_norm2_g, l1_router_w, l1_exp_w_gate, l1_exp_w_up, l1_exp_w_down, final_norm_g):
    b, t, d = x.shape
    nctx = ctx.shape[1]
    assert b + 1 <= MOD_ROWS
    lat_row = lambda bi: bi
    ctx_row = lambda bi: b

    cc = jnp.zeros((MOD_ROWS, d), F32).at[:b].set(c).at[b].set(c_ctx)
    mods0 = _ada(cc, l0_ada_w, l0_ada_b).reshape(MOD_ROWS, N_MOD, d)
    mods1 = _ada(cc, l1_ada_w, l1_ada_b).reshape(MOD_ROWS, N_MOD, d)
    tables = _rope_tables(t)
    qscale = LOG2E * DIFF_HEAD_DIM ** -0.5

    w_in = l0_w_in.astype(BF16)
    cd = CONV_DIM
    wvt = jnp.transpose(l0_w_in[:, 3 * cd + 1024:]).astype(BF16)
    outs0 = [("plain", 0, cd, False, 1.0),
             ("mul", cd, cd, False, 1.0),
             ("heads", 3 * cd, 512, True, qscale),
             ("heads", 3 * cd + 512, 512, True, 1.0)]
    w_main = w_in[:, :3 * cd + 1024]
    bg_l, cx_l, q_l, k_l, vt_l = _proj(x, mods0, lat_row, l0_norm1_g, w_main, outs0, tables, wvt)
    bg_c, cx_c, q_c, k_c, vt_c = _proj(ctx, mods0, ctx_row, l0_norm1_g, w_main, outs0, None, wvt)
    lams = (l0_lambda_q1, l0_lambda_k1, l0_lambda_q2, l0_lambda_k2)
    o_l = _diff_attn(q_l, [k_c, k_l], [vt_c, vt_l], lams, l0_subln_g)
    o_c = _diff_attn(q_c, [k_c], [vt_c], lams, l0_subln_g)

    w_out0 = l0_w_out.astype(BF16)
    rw0 = _router_weights(l0_router_w)
    wg0, wu0, wd0 = l0_exp_w_gate, l0_exp_w_up, l0_exp_w_down
    x1, haug, aff_t = _outproj(x, mods0, lat_row, w_out0, l0_norm2_g, *rw0, o_l,
                               conv_args=(bg_l, cx_l, l0_conv_w))
    x = _moe(x1, haug, aff_t, mods0, lat_row, wg0, wu0, wd0, CAPACITY_FACTOR * t // N_EXPERTS)
    c1, haug_c, aff_tc = _outproj(ctx, mods0, ctx_row, w_out0, l0_norm2_g, *rw0, o_c,
                                  conv_args=(bg_c, cx_c, l0_conv_w))
    ctx = _moe(c1, haug_c, aff_tc, mods0, ctx_row, wg0, wu0, wd0, b * (CAPACITY_FACTOR * nctx // N_EXPERTS))

    qw = SWA_HEADS * SWA_HEAD_DIM
    kvw = SWA_KV_HEADS * SWA_HEAD_DIM
    wq = l1_w_qkv[:, :qw]
    wk = l1_w_qkv[:, qw:qw + kvw].reshape(d, SWA_KV_HEADS, SWA_HEAD_DIM)
    wv = l1_w_qkv[:, qw + kvw:].reshape(d, SWA_KV_HEADS, SWA_HEAD_DIM)
    wk2 = jnp.concatenate([wk, wk], axis=2).reshape(d, 2 * kvw)
    wv2 = jnp.concatenate([wv, jnp.zeros_like(wv)], axis=2).reshape(d, 2 * kvw)
    w_qkv = jnp.concatenate([wq, wk2, wv2], axis=1).astype(BF16)
    outs1 = [("plain", 0, qw, True, LOG2E * SWA_HEAD_DIM ** -0.5),
             ("plain", qw, 2 * kvw, True, 1.0),
             ("plain", qw + 2 * kvw, 2 * kvw, False, 1.0)]
    q1, k1, v1 = _proj(x, mods1, lat_row, l1_norm1_g, w_qkv, outs1, tables)
    outs1c = [("plain", 0, 2 * kvw, False, 1.0), ("plain", 2 * kvw, 2 * kvw, False, 1.0)]
    k1c, v1c = _proj(ctx, mods1, ctx_row, l1_norm1_g, w_qkv[:, qw:], outs1c)
    o1 = _swa_attn(l1_sink, q1, k1, v1, k1c, v1c)

    rw1 = _router_weights(l1_router_w)
    wg1, wu1, wd1 = l1_exp_w_gate, l1_exp_w_up, l1_exp_w_down
    x1, haug, aff_t = _outproj(x, mods1, lat_row, l1_w_out.astype(BF16), l1_norm2_g, *rw1, o1)
    return _moe(x1, haug, aff_t, mods1, lat_row, wg1, wu1, wd1, CAPACITY_FACTOR * t // N_EXPERTS,
                final_g=final_norm_g)
```

```python
import functools
import math

import jax
import jax.numpy as jnp
from jax import lax
from jax.experimental import pallas as pl
from jax.experimental.pallas import tpu as pltpu

F32 = jnp.float32
BF16 = jnp.bfloat16
I32 = jnp.int32

GRID_W = 64
ROPE_BASE = 10000.0
NORM_EPS = 1e-6
NEG_INF = -1e30
N_MOD = 6

CONV_DIM = 512
DIFF_HEADS = 4
DIFF_HEAD_DIM = 64
DIFF_V_DIM = 128
LAM_INIT0 = 0.8 - 0.6 * math.exp(-0.3 * 0)

SWA_HEADS = 16
SWA_KV_HEADS = 4
SWA_HEAD_DIM = 64
SWA_WINDOW = 128

N_EXPERTS = 16
CAPACITY_FACTOR = 2

LANES = 128
SUBLANES = 8
MOD_ROWS = 16
AUG = LANES
ROW_TILE = 256
COMB_W = 64
BF16_ROWS = 16
VT_ROWS = DIFF_V_DIM
SWA_KV_PER_STEP = 4
LOG2E = math.log2(math.e)
VMEM_LIMIT = 56 * 1024 * 1024


def _cparams(sem):
    return pltpu.CompilerParams(dimension_semantics=sem, vmem_limit_bytes=VMEM_LIMIT)


def _dot(a, b):
    return jnp.dot(a, b, preferred_element_type=F32)


def _dot_nt(a, b):
    return lax.dot_general(a, b, (((1,), (1,)), ((), ())), preferred_element_type=F32)


def _modulate(x, gain, shift, scale):
    ms = jnp.mean(x * x, axis=-1, keepdims=True)
    return x * lax.rsqrt(ms + NORM_EPS) * (gain * (1.0 + scale)) + shift


def _ada_kernel(c_ref, w_ref, b_ref, o_ref):
    c = c_ref[...]
    s = (c * jax.nn.sigmoid(c)).astype(BF16)
    o_ref[...] = _dot(s, w_ref[...].astype(BF16)) + b_ref[...]


def _ada(cc, w, b):
    d, n = w.shape
    tn = 512
    return pl.pallas_call(
        _ada_kernel,
        grid=(n // tn,),
        in_specs=[pl.BlockSpec((MOD_ROWS, d), lambda j: (0, 0)),
                  pl.BlockSpec((d, tn), lambda j: (0, j)),
                  pl.BlockSpec((1, tn), lambda j: (0, j))],
        out_specs=pl.BlockSpec((MOD_ROWS, tn), lambda j: (0, j)),
        out_shape=jax.ShapeDtypeStruct((MOD_ROWS, n), F32),
        compiler_params=_cparams(("arbitrary",)),
        name="ada",
    )(cc, w, b.reshape(1, n))


def _rope_tables(t):
    n_freq = DIFF_HEAD_DIM // 4
    inv_freq = ROPE_BASE ** (-jnp.arange(n_freq, dtype=F32) / n_freq)
    pos = jnp.arange(t, dtype=I32)
    row = (pos // GRID_W).astype(F32)[:, None] * inv_freq
    col = (pos % GRID_W).astype(F32)[:, None] * inv_freq
    zeros = jnp.zeros_like(row)
    cos64 = jnp.concatenate([jnp.cos(row), jnp.cos(row), jnp.cos(col), jnp.cos(col)], axis=1)
    sa64 = jnp.concatenate([-jnp.sin(row), zeros, -jnp.sin(col), zeros], axis=1)
    sb64 = jnp.concatenate([zeros, jnp.sin(row), zeros, jnp.sin(col)], axis=1)
    rep = lambda a: jnp.concatenate([a, a], axis=1)
    return rep(cos64), rep(sa64), rep(sb64)


def _rope128(c, cos, sa, sb):
    return c * cos + pltpu.roll(c, LANES - 16, 1) * sa + pltpu.roll(c, 16, 1) * sb


def _proj_kernel(*refs, outs, rope, has_vt):
    x_ref, mod_ref, g_ref, w_ref = refs[:4]
    pos = 4
    if rope:
        cos_ref, sa_ref, sb_ref = refs[pos:pos + 3]
        pos += 3
    if has_vt:
        wvt_ref = refs[pos]
        pos += 1
    out_refs = refs[pos:]
    m = mod_ref[0]
    h = _modulate(x_ref[0], g_ref[...], m[0:1, :], m[1:2, :]).astype(BF16)
    y = _dot(h, w_ref[...])
    for (kind, c0, width, do_rope, scale), o_ref in zip(outs, out_refs):
        for j in range(width // LANES):
            c = y[:, c0 + j * LANES:c0 + (j + 1) * LANES]
            if kind == "mul":
                c = c * y[:, c0 + width + j * LANES:c0 + width + (j + 1) * LANES]
            if do_rope and rope:
                c = _rope128(c, cos_ref[...], sa_ref[...], sb_ref[...])
            if scale != 1.0:
                c = c * scale
            if kind == "heads":
                o_ref[0, j] = c.astype(o_ref.dtype)
            else:
                o_ref[0, :, j * LANES:(j + 1) * LANES] = c.astype(o_ref.dtype)
    if has_vt:
        vt = _dot_nt(wvt_ref[...], h)
        o_ref = out_refs[len(outs)]
        for j in range(vt.shape[0] // LANES):
            o_ref[0, j] = vt[j * LANES:(j + 1) * LANES, :].astype(o_ref.dtype)


def _proj(x, mods, mod_row, gain, w, outs, tables=None, wvt=None):
    b, t, d = x.shape
    tm = min(ROW_TILE, t)
    n = w.shape[1]
    rope = tables is not None
    in_specs = [pl.BlockSpec((1, tm, d), lambda bi, i: (bi, i, 0)),
                pl.BlockSpec((1, N_MOD, d), lambda bi, i: (mod_row(bi), 0, 0)),
                pl.BlockSpec((1, d), lambda bi, i: (0, 0)),
                pl.BlockSpec((d, n), lambda bi, i: (0, 0))]
    args = [x, mods, gain.reshape(1, d), w]
    if rope:
        in_specs += [pl.BlockSpec((tm, LANES), lambda bi, i: (i, 0))] * 3
        args += list(tables)
    if wvt is not None:
        in_specs.append(pl.BlockSpec(wvt.shape, lambda bi, i: (0, 0)))
        args.append(wvt)
    out_specs, out_shapes = [], []
    for (kind, c0, width, do_rope, scale) in outs:
        if kind == "heads":
            nh = width // LANES
            out_specs.append(pl.BlockSpec((1, nh, tm, LANES), lambda bi, i: (bi, 0, i, 0)))
            out_shapes.append(jax.ShapeDtypeStruct((b, nh, t, LANES), BF16))
        else:
            out_specs.append(pl.BlockSpec((1, tm, width), lambda bi, i: (bi, i, 0)))
            out_shapes.append(jax.ShapeDtypeStruct((b, t, width), BF16))
    if wvt is not None:
        nh = wvt.shape[0] // LANES
        out_specs.append(pl.BlockSpec((1, nh, VT_ROWS, tm), lambda bi, i: (bi, 0, 0, i)))
        out_shapes.append(jax.ShapeDtypeStruct((b, nh, VT_ROWS, t), BF16))
    return pl.pallas_call(
        functools.partial(_proj_kernel, outs=tuple(outs), rope=rope, has_vt=wvt is not None),
        grid=(b, t // tm),
        in_specs=in_specs,
        out_specs=out_specs,
        out_shape=out_shapes,
        compiler_params=_cparams(("parallel", "arbitrary")),
        name="proj",
    )(*args)


def _diff_attn_kernel(*refs, nseg):
    q_ref = refs[0]
    k_refs = refs[1:1 + nseg]
    vt_refs = refs[1 + nseg:1 + 2 * nseg]
    lq1, lk1, lq2, lk2, g_ref, o_ref, s_even, m_even, s_odd, m_odd = refs[1 + 2 * nseg:]
    i = pl.program_id(2)
    offs = [0]
    for k in k_refs:
        offs.append(offs[-1] + k.shape[2])

    @pl.when(i == 0)
    def _():
        s_odd[...] = jnp.zeros(s_odd.shape, F32)
        m_odd[...] = jnp.zeros(m_odd.shape, F32)

    def body(s_w, m_w, s_r, m_r):
        q = q_ref[0, 0]
        lane = lax.broadcasted_iota(I32, q.shape, 1)
        zero = jnp.zeros_like(q)
        qs = (jnp.where(lane < DIFF_HEAD_DIM, q, zero), jnp.where(lane >= DIFF_HEAD_DIM, q, zero))
        for half, qh in enumerate(qs):
            m = None
            for si, k in enumerate(k_refs):
                s = _dot_nt(k[0, 0], qh)
                s_w[half, offs[si]:offs[si + 1], :] = s
                ms = jnp.max(s, axis=0, keepdims=True)
                m = ms if m is None else jnp.maximum(m, ms)
            m_w[half] = m

        lam = (jnp.exp(jnp.sum(lq1[...] * lk1[...], axis=1, keepdims=True))
               - jnp.exp(jnp.sum(lq2[...] * lk2[...], axis=1, keepdims=True)) + LAM_INIT0)
        probs, norms = [], []
        for half in range(2):
            p = [jnp.exp2(s_r[half, offs[si]:offs[si + 1], :] - m_r[half]) for si in range(nseg)]
            norms.append(functools.reduce(jnp.add, [jnp.sum(x, axis=0, keepdims=True) for x in p]))
            probs.append(p)
        c2 = lam * norms[0] / norms[1]
        ot = None
        for si in range(nseg):
            a = (probs[0][si] - probs[1][si] * c2).astype(BF16)
            part = _dot(vt_refs[si][0, 0], a)
            ot = part if ot is None else ot + part
        ot = ot * (1.0 / norms[0])
        msq = jnp.mean(ot * ot, axis=0, keepdims=True)
        on = ot * lax.rsqrt(msq + NORM_EPS) * (g_ref[...] * (1.0 - LAM_INIT0))
        o_ref[0, 0] = on.T.astype(o_ref.dtype)

    @pl.when(lax.rem(i, 2) == 0)
    def _():
        body(s_even, m_even, s_odd, m_odd)

    @pl.when(lax.rem(i, 2) == 1)
    def _():
        body(s_odd, m_odd, s_even, m_even)


def _diff_attn(q, ks, vts, lams, subln_g):
    b, nh, t, _ = q.shape
    tq = min(ROW_TILE, t)
    nq = t // tq
    nseg = len(ks)
    tk = sum(k.shape[2] for k in ks)
    in_specs = [pl.BlockSpec((1, 1, tq, LANES), lambda bi, h, i: (bi, h, jnp.minimum(i, nq - 1), 0))]
    for k in ks:
        in_specs.append(pl.BlockSpec((1, 1, k.shape[2], LANES), lambda bi, h, i: (bi, h, 0, 0)))
    for vt in vts:
        in_specs.append(pl.BlockSpec((1, 1, VT_ROWS, vt.shape[3]), lambda bi, h, i: (bi, h, 0, 0)))
    in_specs += [pl.BlockSpec((1, DIFF_HEAD_DIM), lambda bi, h, i: (0, 0))] * 4
    in_specs.append(pl.BlockSpec((DIFF_V_DIM, 1), lambda bi, h, i: (0, 0)))
    return pl.pallas_call(
        functools.partial(_diff_attn_kernel, nseg=nseg),
        grid=(b, nh, nq + 1),
        in_specs=in_specs,
        out_specs=pl.BlockSpec((1, 1, tq, LANES), lambda bi, h, i: (bi, h, jnp.maximum(i - 1, 0), 0)),
        out_shape=jax.ShapeDtypeStruct((b, nh, t, LANES), BF16),
        scratch_shapes=[pltpu.VMEM((2, tk, tq), F32), pltpu.VMEM((2, 1, tq), F32)] * 2,
        compiler_params=_cparams(("parallel", "arbitrary", "arbitrary")),
        name="diff_attn",
    )(q, *ks, *vts, *[l.reshape(1, DIFF_HEAD_DIM) for l in lams], subln_g.reshape(DIFF_V_DIM, 1))


def _swa_kernel(*refs, t, tq, nq, nwin, kvps):
    sink_ref, q_ref = refs[0], refs[1]
    kw_refs = refs[2:2 + nwin]
    kc_ref = refs[2 + nwin]
    vw_refs = refs[3 + nwin:3 + 2 * nwin]
    vc_ref, o_ref, s_even, m_even, s_odd, m_odd = refs[3 + 2 * nwin:]
    n = pl.program_id(1)
    i = pl.program_id(2)
    c = kc_ref.shape[2]
    span = nwin * LANES
    group = SWA_HEADS // SWA_KV_HEADS

    @pl.when(i == 0)
    def _():
        s_odd[...] = jnp.zeros(s_odd.shape, F32)
        m_odd[...] = jnp.zeros(m_odd.shape, F32)

    def body(s_w, m_w, s_r, m_r):
        ti = jnp.minimum(i, nq - 1)
        kpos = (ti * tq - SWA_WINDOW) + lax.broadcasted_iota(I32, (span, tq), 0)
        qpos = ti * tq + lax.broadcasted_iota(I32, (span, tq), 1)
        ok = jnp.logical_and(jnp.logical_and(kpos >= 0, kpos < t), jnp.abs(qpos - kpos) <= SWA_WINDOW)
        bias = jnp.where(ok, 0.0, NEG_INF)
        lane = lax.broadcasted_iota(I32, (tq, LANES), 1)
        for kk in range(kvps):
            kwin = jnp.concatenate([r[0, kk] for r in kw_refs], axis=0)
            kc = kc_ref[0, kk]
            for j in range(group // 2):
                qc = q_ref[0, kk * (group // 2) + j]
                zero = jnp.zeros_like(qc)
                for half in range(2):
                    hh = kk * group + 2 * j + half
                    keep = (lane < SWA_HEAD_DIM) if half == 0 else (lane >= SWA_HEAD_DIM)
                    qz = jnp.where(keep, qc, zero)
                    s_c = _dot_nt(kc, qz)
                    s_l = _dot_nt(kwin, qz) + bias
                    sink = sink_ref[n * kvps * group + hh] * LOG2E
                    s_w[hh, 0:c, :] = s_c
                    s_w[hh, c:c + span, :] = s_l
                    m_w[hh] = jnp.maximum(jnp.maximum(jnp.max(s_c, axis=0, keepdims=True),
                                                      jnp.max(s_l, axis=0, keepdims=True)), sink)

        for kk in range(kvps):
            vwin = jnp.concatenate([r[0, kk] for r in vw_refs], axis=1)
            vc = vc_ref[0, kk]
            vwin = jnp.where(lax.broadcasted_iota(I32, vwin.shape, 0) == SWA_HEAD_DIM, jnp.ones_like(vwin), vwin)
            vc = jnp.where(lax.broadcasted_iota(I32, vc.shape, 0) == SWA_HEAD_DIM, jnp.ones_like(vc), vc)
            for j in range(group // 2):
                res = []
                for half in range(2):
                    hh = kk * group + 2 * j + half
                    m = m_r[hh]
                    p_c = jnp.exp2(s_r[hh, 0:c, :] - m).astype(BF16)
                    p_l = jnp.exp2(s_r[hh, c:c + span, :] - m).astype(BF16)
                    ot = _dot(vc, p_c) + _dot(vwin, p_l)
                    sink = sink_ref[n * kvps * group + hh] * LOG2E
                    den = ot[SWA_HEAD_DIM:SWA_HEAD_DIM + 1, :] + jnp.exp2(sink - m)
                    res.append(ot[0:SWA_HEAD_DIM, :] * (1.0 / den))
                o_ref[0, kk * (group // 2) + j] = jnp.concatenate(res, axis=0).T.astype(o_ref.dtype)

    @pl.when(lax.rem(i, 2) == 0)
    def _():
        body(s_even, m_even, s_odd, m_odd)

    @pl.when(lax.rem(i, 2) == 1)
    def _():
        body(s_odd, m_odd, s_even, m_even)


def _swa_attn(sink, q, kl, vtl, kc, vtc):
    b, nchunk, t, _ = q.shape
    nkv = kl.shape[1]
    cpk = nchunk // nkv
    c = kc.shape[2]
    tq = ROW_TILE
    nq = t // tq
    nblk = t // LANES
    bpt = tq // LANES
    nwin = bpt + 2 * (SWA_WINDOW // LANES)
    kvps = SWA_KV_PER_STEP

    def kblk(i, j, lag):
        ti = jnp.clip(i - lag, 0, nq - 1)
        return jnp.clip(ti * bpt - SWA_WINDOW // LANES + j, 0, nblk - 1)

    in_specs = [pl.BlockSpec(memory_space=pltpu.SMEM),
                pl.BlockSpec((1, kvps * cpk, tq, LANES), lambda bi, n, i: (bi, n, jnp.minimum(i, nq - 1), 0))]
    for j in range(nwin):
        in_specs.append(pl.BlockSpec((1, kvps, LANES, LANES), lambda bi, n, i, j=j: (bi, n, kblk(i, j, 0), 0)))
    in_specs.append(pl.BlockSpec((1, kvps, c, LANES), lambda bi, n, i: (bi, n, 0, 0)))
    for j in range(nwin):
        in_specs.append(pl.BlockSpec((1, kvps, LANES, LANES), lambda bi, n, i, j=j: (bi, n, 0, kblk(i, j, 1))))
    in_specs.append(pl.BlockSpec((1, kvps, LANES, c), lambda bi, n, i: (bi, n, 0, 0)))
    heads = kvps * SWA_HEADS // nkv
    scratch = [pltpu.VMEM((heads, c + nwin * LANES, tq), F32), pltpu.VMEM((heads, 1, tq), F32)] * 2
    return pl.pallas_call(
        functools.partial(_swa_kernel, t=t, tq=tq, nq=nq, nwin=nwin, kvps=kvps),
        grid=(b, nkv // kvps, nq + 1),
        in_specs=in_specs,
        out_specs=pl.BlockSpec((1, kvps * cpk, tq, LANES), lambda bi, n, i: (bi, n, jnp.maximum(i - 1, 0), 0)),
        out_shape=jax.ShapeDtypeStruct((b, nchunk, t, LANES), BF16),
        scratch_shapes=scratch,
        compiler_params=_cparams(("parallel", "arbitrary", "arbitrary")),
        name="swa_attn",
    )(sink, q, *([kl] * nwin), kc, *([vtl] * nwin), vtc)


def _outproj_kernel(*refs, conv, tm):
    if conv:
        (x_ref, mod_ref, bg_ref, p_ref, pprev_ref, pnext_ref, cw_ref, o_ref, w_ref,
         g2_ref, rwh_ref, rwl_ref, xn_ref, haug_ref, afft_ref) = refs
        i = pl.program_id(1)
        ni = pl.num_programs(1)
        p = p_ref[0].astype(F32)
        row = lax.broadcasted_iota(I32, p.shape, 0)
        halo_prev = jnp.where(i > 0, pprev_ref[0, 15:16, :].astype(F32), 0.0)
        halo_next = jnp.where(i < ni - 1, pnext_ref[0, 0:1, :].astype(F32), 0.0)
        p_prev = jnp.where(row == 0, halo_prev, pltpu.roll(p, 1, 0))
        p_next = jnp.where(row == tm - 1, halo_next, pltpu.roll(p, tm - 1, 0))
        cw = cw_ref[...]
        cv = p_prev * cw[0:1, :] + p * cw[1:2, :] + p_next * cw[2:3, :]
        u = (bg_ref[0].astype(F32) * cv).astype(BF16)
        lhs = jnp.concatenate([u] + [o_ref[0, j] for j in range(DIFF_HEADS)], axis=1)
    else:
        x_ref, mod_ref, o_ref, w_ref, g2_ref, rwh_ref, rwl_ref, xn_ref, haug_ref, afft_ref = refs
        lhs = jnp.concatenate([o_ref[0, j] for j in range(o_ref.shape[1])], axis=1)
    m = mod_ref[0]
    xn = x_ref[0] + m[2:3, :] * _dot(lhs, w_ref[...])
    xn_ref[0] = xn
    h = _modulate(xn, g2_ref[...], m[3:4, :], m[4:5, :])
    d = h.shape[1]
    haug_ref[0, :, 0:d] = h
    h_hi = h.astype(BF16)
    h_lo = (h - h_hi.astype(F32)).astype(BF16)
    logits = _dot(h_hi, rwh_ref[...]) + _dot(h_hi, rwl_ref[...]) + _dot(h_lo, rwh_ref[...])
    lane = lax.broadcasted_iota(I32, logits.shape, 1)
    logits = jnp.where(lane < N_EXPERTS, logits, NEG_INF)
    e = jnp.exp(logits - jnp.max(logits, axis=1, keepdims=True))
    aff = e / jnp.sum(e, axis=1, keepdims=True)
    haug_ref[0, :, d:d + AUG] = aff
    afft_ref[0] = aff.T[0:N_EXPERTS, :]


def _outproj(x, mods, mod_row, w_out, n2g, rw_hi, rw_lo, o, conv_args=None):
    b, t, d = x.shape
    tm = min(ROW_TILE, t)
    conv = conv_args is not None
    xmap = lambda bi, i: (bi, i, 0)
    in_specs = [pl.BlockSpec((1, tm, d), xmap),
                pl.BlockSpec((1, N_MOD, d), lambda bi, i: (mod_row(bi), 0, 0))]
    args = [x, mods]
    if conv:
        bg, p, cw = conv_args
        hb = tm // 16
        nhb = t // 16
        in_specs += [pl.BlockSpec((1, tm, CONV_DIM), xmap),
                     pl.BlockSpec((1, tm, CONV_DIM), xmap),
                     pl.BlockSpec((1, 16, CONV_DIM), lambda bi, i: (bi, jnp.maximum(i * hb - 1, 0), 0)),
                     pl.BlockSpec((1, 16, CONV_DIM), lambda bi, i: (bi, jnp.minimum((i + 1) * hb, nhb - 1), 0)),
                     pl.BlockSpec((3, CONV_DIM), lambda bi, i: (0, 0)),
                     pl.BlockSpec((1, DIFF_HEADS, tm, LANES), lambda bi, i: (bi, 0, i, 0))]
        args += [bg, p, p, p, cw, o]
    else:
        in_specs.append(pl.BlockSpec((1, o.shape[1], tm, LANES), lambda bi, i: (bi, 0, i, 0)))
        args.append(o)
    in_specs += [pl.BlockSpec(w_out.shape, lambda bi, i: (0, 0)),
                 pl.BlockSpec((1, d), lambda bi, i: (0, 0)),
                 pl.BlockSpec((d, LANES), lambda bi, i: (0, 0)),
                 pl.BlockSpec((d, LANES), lambda bi, i: (0, 0))]
    args += [w_out, n2g.reshape(1, d), rw_hi, rw_lo]
    return pl.pallas_call(
        functools.partial(_outproj_kernel, conv=conv, tm=tm),
        grid=(b, t // tm),
        in_specs=in_specs,
        out_specs=[pl.BlockSpec((1, tm, d), xmap),
                   pl.BlockSpec((1, tm, d + AUG), xmap),
                   pl.BlockSpec((1, N_EXPERTS, tm), lambda bi, i: (bi, 0, i))],
        out_shape=[jax.ShapeDtypeStruct((b, t, d), F32),
                   jax.ShapeDtypeStruct((b, t, d + AUG), F32),
                   jax.ShapeDtypeStruct((b, N_EXPERTS, t), F32)],
        compiler_params=_cparams(("parallel", "arbitrary")),
        name="outproj",
    )(*args)


def _route_kernel(aff_ref, idx_ref, post_ref, offs_ref, pinc_ref, *, t, cap, capp, lc):
    nch = t // lc
    ne = N_EXPERTS
    aff = aff_ref[0]

    def search(i, lo):
        cand = lo | lax.shift_left(jnp.int32(1), 30 - i)
        cnt = jnp.sum(jnp.where(aff >= pltpu.bitcast(cand, F32), 1.0, 0.0), axis=1, keepdims=True)
        return jnp.where(cnt >= cap, cand, lo)

    thr_bits = lax.fori_loop(0, 31, search, jnp.zeros((ne, 1), I32))
    thr = pltpu.bitcast(thr_bits, F32)
    gt = aff > thr
    eq = aff == thr
    need = cap - jnp.sum(jnp.where(gt, 1.0, 0.0), axis=1, keepdims=True)

    r_i = lax.broadcasted_iota(I32, (lc, lc), 0)
    c_i = lax.broadcasted_iota(I32, (lc, lc), 1)
    upper = jnp.where(r_i < c_i, 1.0, 0.0).astype(BF16)

    def excl_prefix(x):
        outs, offs = [], []
        carry = jnp.zeros((ne, 1), F32)
        for c in range(nch):
            xc = x[:, c * lc:(c + 1) * lc]
            offs.append(carry)
            outs.append(_dot(xc.astype(BF16), upper) + carry)
            carry = carry + jnp.sum(xc, axis=1, keepdims=True)
        offs.append(carry)
        return jnp.concatenate(outs, axis=1), offs

    eqf = jnp.where(eq, 1.0, 0.0)
    eq_rank, _ = excl_prefix(eqf)
    sel = jnp.logical_or(gt, jnp.logical_and(eq, eq_rank < need))
    self_ = jnp.where(sel, 1.0, 0.0)
    pos, offs = excl_prefix(self_)

    lane = lax.broadcasted_iota(I32, (ne, LANES), 1)
    om = jnp.zeros((ne, LANES), F32)
    for c, o in enumerate(offs):
        om = jnp.where(lane == c, o, om)
    offs_ref[0] = om

    posm = jnp.where(sel, pos, -1.0)
    pad = jnp.full((LANES - ne, lc), -1.0, F32)
    for c in range(nch):
        blk = jnp.concatenate([posm[:, c * lc:(c + 1) * lc], pad], axis=0)
        post_ref[0, c * lc:(c + 1) * lc, :] = blk.T
    pinc = pos + self_
    nt = t // LANES
    for k in range(nt):
        pinc_ref[k] = pinc[:, k * LANES:(k + 1) * LANES]

    sub = lax.broadcasted_iota(I32, (SUBLANES, LANES), 0).astype(F32)
    lane_c = lax.broadcasted_iota(I32, (capp, LANES), 1)
    unroll = min(8, nt)

    def per_expert(e, acc_m):
        cols = []
        for jb in range(capp // LANES):
            def tiles(kk, acc, jb=jb):
                for u in range(unroll):
                    q = pinc_ref[kk * unroll + u, pl.ds(e, 1), :] - sub - float(jb * LANES)
                    acc = acc + jnp.concatenate(
                        [jnp.where(q <= float(SUBLANES * i), 1.0, 0.0) for i in range(LANES // SUBLANES)], axis=0)
                return acc
            acc = lax.fori_loop(0, nt // unroll, tiles, jnp.zeros((LANES, LANES), F32))
            cols.append(jnp.sum(acc, axis=1, keepdims=True))
        col = jnp.concatenate(cols, axis=0)
        return jnp.where(lane_c == e, col, acc_m)

    idx_m = lax.fori_loop(0, ne, per_expert, jnp.zeros((capp, LANES), F32))
    base = pl.program_id(0) * t
    idx_ref[0] = idx_m.T[0:ne, :].astype(I32) + base


def _route(aff_t, cap):
    b, ne, t = aff_t.shape
    lc = min(256, t)
    capp = max(cap, LANES)
    nch = t // lc
    return pl.pallas_call(
        functools.partial(_route_kernel, t=t, cap=cap, capp=capp, lc=lc),
        grid=(b,),
        in_specs=[pl.BlockSpec((1, ne, t), lambda bi: (bi, 0, 0))],
        out_specs=[pl.BlockSpec((1, ne, capp), lambda bi: (bi, 0, 0)),
                   pl.BlockSpec((1, t, LANES), lambda bi: (bi, 0, 0)),
                   pl.BlockSpec((1, ne, LANES), lambda bi: (bi, 0, 0))],
        out_shape=[jax.ShapeDtypeStruct((b, ne, capp), I32),
                   jax.ShapeDtypeStruct((b, t, LANES), F32),
                   jax.ShapeDtypeStruct((b, ne, LANES), F32)],
        scratch_shapes=[pltpu.VMEM((t // LANES, ne, LANES), F32)],
        compiler_params=_cparams(("arbitrary",)),
        name="route",
    )(aff_t)


def _ffn_kernel(idx_ref, haug_ref, wg_ref, wu_ref, wd_ref, y_ref, hbuf, w_in, w_dn, gsem, *, rows, d):
    e = pl.program_id(0)
    step = e * pl.num_programs(1) + pl.program_id(1)
    nsteps = pl.num_programs(0) * pl.num_programs(1)

    @pl.when(pl.program_id(1) == 0)
    def _():
        w_in[0] = wg_ref[0].astype(BF16)
        w_in[1] = wu_ref[0].astype(BF16)
        w_dn[...] = wd_ref[0].astype(BF16)

    def row_copy(src_row, dst_slot, dst_row):
        return pltpu.make_async_copy(haug_ref.at[pl.ds(src_row, 1)],
                                     hbuf.at[dst_slot, pl.ds(dst_row, 1)], gsem.at[dst_slot])

    def wait_rows(sl):
        pltpu.make_async_copy(haug_ref.at[pl.ds(0, rows)], hbuf.at[sl], gsem.at[sl]).wait()

    @pl.when(step == 0)
    def _():
        def body(r, carry):
            row_copy(idx_ref[r], 0, r).start()
            return carry
        lax.fori_loop(0, rows, body, 0)

    def run(slot):
        base = jnp.minimum(step + 1, nsteps - 1) * rows
        for r in range(rows):
            row_copy(idx_ref[base + r], 1 - slot, r).start()

        wait_rows(slot)
        hrow = hbuf[slot]
        hs = hrow[:, 0:d].astype(BF16)
        lane = lax.broadcasted_iota(I32, (rows, AUG), 1)
        gate = jnp.sum(jnp.where(lane == e, hrow[:, d:d + AUG], 0.0), axis=1, keepdims=True)
        a = _dot(hs, w_in[0])
        u = _dot(hs, w_in[1])
        hm = (a * jax.nn.sigmoid(a) * u).astype(BF16)
        y_ref[0] = (_dot(hm, w_dn[...]) * gate).astype(y_ref.dtype)

        @pl.when(step == nsteps - 1)
        def _():
            wait_rows(1 - slot)

    for parity in range(2):
        pl.when(lax.rem(step, 2) == parity)(functools.partial(run, parity))


def _ffn(idx_flat, haug, wg, wu, wd, rows):
    ne, d, f = wg.shape
    n_rows_total = idx_flat.shape[0]
    steps = n_rows_total // (ne * rows)
    grid_spec = pltpu.PrefetchScalarGridSpec(
        num_scalar_prefetch=1,
        grid=(ne, steps),
        in_specs=[pl.BlockSpec(memory_space=pl.ANY),
                  pl.BlockSpec((1, d, f), lambda e, s, idx: (e, 0, 0)),
                  pl.BlockSpec((1, d, f), lambda e, s, idx: (e, 0, 0)),
                  pl.BlockSpec((1, f, d), lambda e, s, idx: (e, 0, 0))],
        out_specs=pl.BlockSpec((1, rows, d), lambda e, s, idx: (e, s, 0)),
        scratch_shapes=[pltpu.VMEM((2, rows, d + AUG), F32),
                        pltpu.VMEM((2, d, f), BF16),
                        pltpu.VMEM((f, d), BF16),
                        pltpu.SemaphoreType.DMA((2,))],
    )
    return pl.pallas_call(
        functools.partial(_ffn_kernel, rows=rows, d=d),
        grid_spec=grid_spec,
        out_shape=jax.ShapeDtypeStruct((ne, steps * rows, d), BF16),
        compiler_params=_cparams(("arbitrary", "arbitrary")),
        name="ffn",
    )(idx_flat, haug, wg, wu, wd)


def _combine_kernel(ws_ref, nr_ref, x_ref, mod_ref, post_ref, fg_ref, y_ref, o_ref, stage, acc, sem,
                    *, cap, tm, final_norm):
    nchunk = pl.num_programs(1)
    step = pl.program_id(0) * nchunk + pl.program_id(1)
    nsteps = pl.num_programs(0) * nchunk
    slot = lax.rem(step, 2)
    ne = N_EXPERTS
    w = COMB_W
    per = LANES // w
    total_rows = y_ref.shape[1]
    lane_row = lax.broadcasted_iota(I32, (1, LANES), 1)

    def window(st, e, r):
        sample = lax.div(st, nchunk)
        nominal = ws_ref[st * ne + e] + r * w
        grow = pl.multiple_of(jnp.minimum(sample * cap + nominal, total_rows - w), BF16_ROWS)
        return nominal, grow, grow - sample * cap

    def copy(e, grow, sl, si):
        return pltpu.make_async_copy(y_ref.at[e, pl.ds(grow, w)], stage.at[sl, pl.ds(e * w, w)], sem.at[si])

    def fetch(st, r, sl, si):
        for e in range(ne):
            copy(e, window(st, e, r)[1], sl, si).start()

    def wait(sl, si):
        for e in range(ne):
            copy(e, 0, sl, si).wait()

    def gathered(st, r, sl):
        pt = post_ref[0]
        blocks = []
        for g in range(ne // per):
            tgt = jnp.zeros((1, LANES), F32)
            mine = jnp.zeros((tm, LANES), F32)
            for k in range(per):
                e = g * per + k
                nominal, _, first = window(st, e, r)
                inb = jnp.logical_and(lane_row >= k * w, lane_row < (k + 1) * w)
                slot = first - k * w + lane_row
                tgt = jnp.where(inb, jnp.where(slot >= nominal, slot, -2).astype(F32), tgt)
                mine = jnp.where(inb, pt[:, e:e + 1], mine)
            blocks.append(jnp.where(mine == tgt, 1.0, 0.0).astype(BF16))
        onehot = jnp.concatenate(blocks, axis=1)
        return _dot(onehot, stage[sl])

    @pl.when(step == 0)
    def _():
        fetch(0, 0, 0, 0)

    @pl.when(step + 1 < nsteps)
    def _():
        fetch(step + 1, 0, 1 - slot, 1 - slot)

    wait(slot, slot)
    acc[...] = gathered(step, 0, slot)

    def extra_round(r, carry):
        fetch(step, r, slot, 2)
        wait(slot, 2)
        acc[...] += gathered(step, r, slot)
        return carry

    lax.fori_loop(1, nr_ref[step], extra_round, 0)
    m = mod_ref[0]
    xn = x_ref[0] + m[5:6, :] * acc[...]
    if final_norm:
        ms = jnp.mean(xn * xn, axis=-1, keepdims=True)
        xn = xn * lax.rsqrt(ms + NORM_EPS) * fg_ref[...]
    o_ref[0] = xn


def _combine(ws_flat, nr_flat, x, mods, mod_row, post, y, cap, final_g=None):
    b, t, d = x.shape
    tm = min(ROW_TILE, t)
    ne = N_EXPERTS
    final_norm = final_g is not None
    fg = (final_g if final_norm else jnp.ones((d,), F32)).reshape(1, d)
    grid_spec = pltpu.PrefetchScalarGridSpec(
        num_scalar_prefetch=2,
        grid=(b, t // tm),
        in_specs=[pl.BlockSpec((1, tm, d), lambda bi, i, a, c: (bi, i, 0)),
                  pl.BlockSpec((1, N_MOD, d), lambda bi, i, a, c: (mod_row(bi), 0, 0)),
                  pl.BlockSpec((1, tm, LANES), lambda bi, i, a, c: (bi, i, 0)),
                  pl.BlockSpec((1, d), lambda bi, i, a, c: (0, 0)),
                  pl.BlockSpec(memory_space=pl.ANY)],
        out_specs=pl.BlockSpec((1, tm, d), lambda bi, i, a, c: (bi, i, 0)),
        scratch_shapes=[pltpu.VMEM((2, ne * COMB_W, d), BF16),
                        pltpu.VMEM((tm, d), F32),
                        pltpu.SemaphoreType.DMA((3,))],
    )
    return pl.pallas_call(
        functools.partial(_combine_kernel, cap=cap, tm=tm, final_norm=final_norm),
        grid_spec=grid_spec,
        out_shape=jax.ShapeDtypeStruct((b, t, d), F32),
        compiler_params=_cparams(("arbitrary", "arbitrary")),
        name="combine",
    )(ws_flat, nr_flat, x, mods, post, fg, y)


def _moe(xn, haug, aff_t, mods, mod_row, wg, wu, wd, rows_per_step, final_g=None):
    b, t, d = xn.shape
    ne = N_EXPERTS
    cap = CAPACITY_FACTOR * t // ne
    idx, post, offs = _route(aff_t, cap)
    idx_flat = jnp.transpose(idx[:, :, :cap], (1, 0, 2)).reshape(-1)
    y = _ffn(idx_flat, haug.reshape(b * t, d + AUG), wg, wu, wd, rows_per_step)
    tm = min(ROW_TILE, t)
    nchunk = t // tm
    lc = min(256, t)
    per = tm // lc
    offs_i = offs.astype(I32)[:, :, 0:t // lc + 1:per]
    start = jnp.transpose(offs_i[:, :, :nchunk], (0, 2, 1))
    end = jnp.transpose(offs_i[:, :, 1:], (0, 2, 1))
    ws = (start // BF16_ROWS) * BF16_ROWS
    nr = jnp.maximum(jnp.max((end - ws + COMB_W - 1) // COMB_W, axis=2), 1)
    return _combine(ws.reshape(-1), nr.reshape(-1), xn, mods, mod_row, post, y, cap, final_g)


def _split_hi_lo(w):
    hi = w.astype(BF16)
    lo = (w - hi.astype(F32)).astype(BF16)
    return hi, lo


def _router_weights(rw):
    d, ne = rw.shape
    pad = jnp.zeros((d, LANES - ne), F32)
    return _split_hi_lo(jnp.concatenate([rw, pad], axis=1))


def kernel(x, c, ctx, c_ctx, l0_ada_w, l0_ada_b, l0_norm1_g, l0_w_in, l0_conv_w, l0_lambda_q1, l0_lambda_k1, l0_lambda_q2, l0_lambda_k2, l0_subln_g, l0_w_out, l0_norm2_g, l0_router_w, l0_exp_w_gate, l0_exp_w_up, l0_exp_w_down, l1_ada_w, l1_ada_b, l1_norm1_g, l1_w_qkv, l1_sink, l1_w_out, l1_norm2_g, l1_router_w, l1_exp_w_gate, l1_exp_w_up, l1_exp_w_down, final_norm_g):
    b, t, d = x.shape
    nctx = ctx.shape[1]
    assert b + 1 <= MOD_ROWS
    lat_row = lambda bi: bi
    ctx_row = lambda bi: b

    cc = jnp.zeros((MOD_ROWS, d), F32).at[:b].set(c).at[b].set(c_ctx)
    mods0 = _ada(cc, l0_ada_w, l0_ada_b).reshape(MOD_ROWS, N_MOD, d)
    mods1 = _ada(cc, l1_ada_w, l1_ada_b).reshape(MOD_ROWS, N_MOD, d)
    tables = _rope_tables(t)
    qscale = LOG2E * DIFF_HEAD_DIM ** -0.5

    w_in = l0_w_in.astype(BF16)
    cd = CONV_DIM
    wvt = jnp.transpose(l0_w_in[:, 3 * cd + 1024:]).astype(BF16)
    outs0 = [("plain", 0, cd, False, 1.0),
             ("mul", cd, cd, False, 1.0),
             ("heads", 3 * cd, 512, True, qscale),
             ("heads", 3 * cd + 512, 512, True, 1.0)]
    w_main = w_in[:, :3 * cd + 1024]
    bg_l, cx_l, q_l, k_l, vt_l = _proj(x, mods0, lat_row, l0_norm1_g, w_main, outs0, tables, wvt)
    bg_c, cx_c, q_c, k_c, vt_c = _proj(ctx, mods0, ctx_row, l0_norm1_g, w_main, outs0, None, wvt)
    lams = (l0_lambda_q1, l0_lambda_k1, l0_lambda_q2, l0_lambda_k2)
    o_l = _diff_attn(q_l, [k_c, k_l], [vt_c, vt_l], lams, l0_subln_g)
    o_c = _diff_attn(q_c, [k_c], [vt_c], lams, l0_subln_g)

    w_out0 = l0_w_out.astype(BF16)
    rw0 = _router_weights(l0_router_w)
    wg0, wu0, wd0 = l0_exp_w_gate, l0_exp_w_up, l0_exp_w_down
    x1, haug, aff_t = _outproj(x, mods0, lat_row, w_out0, l0_norm2_g, *rw0, o_l,
                               conv_args=(bg_l, cx_l, l0_conv_w))
    x = _moe(x1, haug, aff_t, mods0, lat_row, wg0, wu0, wd0, CAPACITY_FACTOR * t // N_EXPERTS)
    c1, haug_c, aff_tc = _outproj(ctx, mods0, ctx_row, w_out0, l0_norm2_g, *rw0, o_c,
                                  conv_args=(bg_c, cx_c, l0_conv_w))
    ctx = _moe(c1, haug_c, aff_tc, mods0, ctx_row, wg0, wu0, wd0, b * (CAPACITY_FACTOR * nctx // N_EXPERTS))

    qw = SWA_HEADS * SWA_HEAD_DIM
    kvw = SWA_KV_HEADS * SWA_HEAD_DIM
    wq = l1_w_qkv[:, :qw]
    wk = l1_w_qkv[:, qw:qw + kvw].reshape(d, SWA_KV_HEADS, SWA_HEAD_DIM)
    wv = l1_w_qkv[:, qw + kvw:].reshape(d, SWA_KV_HEADS, SWA_HEAD_DIM)
    wk2 = jnp.concatenate([wk, wk], axis=2).reshape(d, 2 * kvw)
    wv2 = jnp.concatenate([wv, jnp.zeros_like(wv)], axis=2).reshape(d, 2 * kvw)
    w_qk = jnp.concatenate([wq, wk2], axis=1).astype(BF16)
    wvt1 = jnp.transpose(wv2).astype(BF16)
    outs1 = [("heads", 0, qw, True, LOG2E * SWA_HEAD_DIM ** -0.5),
             ("heads", qw, 2 * kvw, True, 1.0)]
    q1, k1, vt1 = _proj(x, mods1, lat_row, l1_norm1_g, w_qk, outs1, tables, wvt1)
    k1c, vt1c = _proj(ctx, mods1, ctx_row, l1_norm1_g, w_qk[:, qw:], [("heads", 0, 2 * kvw, False, 1.0)], None, wvt1)
    o1 = _swa_attn(l1_sink, q1, k1, vt1, k1c, vt1c)

    rw1 = _router_weights(l1_router_w)
    wg1, wu1, wd1 = l1_exp_w_gate, l1_exp_w_up, l1_exp_w_down
    x1, haug, aff_t = _outproj(x, mods1, lat_row, l1_w_out.astype(BF16), l1_norm2_g, *rw1, o1)
    return _moe(x1, haug, aff_t, mods1, lat_row, wg1, wu1, wd1, CAPACITY_FACTOR * t // N_EXPERTS,
                final_g=final_norm_g)
```

```python
import functools
import math

import jax
import jax.numpy as jnp
from jax import lax
from jax.experimental import pallas as pl
from jax.experimental.pallas import tpu as pltpu

F32 = jnp.float32
BF16 = jnp.bfloat16
I32 = jnp.int32

GRID_W = 64
ROPE_BASE = 10000.0
NORM_EPS = 1e-6
NEG_INF = -1e30
N_MOD = 6

CONV_DIM = 512
DIFF_HEADS = 4
DIFF_HEAD_DIM = 64
DIFF_V_DIM = 128
LAM_INIT0 = 0.8 - 0.6 * math.exp(-0.3 * 0)

SWA_HEADS = 16
SWA_KV_HEADS = 4
SWA_HEAD_DIM = 64
SWA_WINDOW = 128

N_EXPERTS = 16
CAPACITY_FACTOR = 2

LANES = 128
SUBLANES = 8
MOD_ROWS = 16
AUG = LANES
ROW_TILE = 256
PROJ_TILE = 512
COMB_W = 64
BF16_ROWS = 16
VT_ROWS = DIFF_V_DIM
SWA_KV_PER_STEP = 4
LOG2E = math.log2(math.e)
VMEM_LIMIT = 56 * 1024 * 1024


def _cparams(sem):
    return pltpu.CompilerParams(dimension_semantics=sem, vmem_limit_bytes=VMEM_LIMIT)


def _dot(a, b):
    return jnp.dot(a, b, preferred_element_type=F32)


def _dot_nt(a, b):
    return lax.dot_general(a, b, (((1,), (1,)), ((), ())), preferred_element_type=F32)


def _modulate(x, gain, shift, scale):
    ms = jnp.mean(x * x, axis=-1, keepdims=True)
    return x * lax.rsqrt(ms + NORM_EPS) * (gain * (1.0 + scale)) + shift


def _ada_kernel(c_ref, w_ref, b_ref, o_ref):
    c = c_ref[...]
    s = (c * jax.nn.sigmoid(c)).astype(BF16)
    o_ref[...] = _dot(s, w_ref[...].astype(BF16)) + b_ref[...]


def _ada(cc, w, b):
    d, n = w.shape
    tn = 512
    return pl.pallas_call(
        _ada_kernel,
        grid=(n // tn,),
        in_specs=[pl.BlockSpec((MOD_ROWS, d), lambda j: (0, 0)),
                  pl.BlockSpec((d, tn), lambda j: (0, j)),
                  pl.BlockSpec((1, tn), lambda j: (0, j))],
        out_specs=pl.BlockSpec((MOD_ROWS, tn), lambda j: (0, j)),
        out_shape=jax.ShapeDtypeStruct((MOD_ROWS, n), F32),
        compiler_params=_cparams(("arbitrary",)),
        name="ada",
    )(cc, w, b.reshape(1, n))


def _rope_tables(t):
    n_freq = DIFF_HEAD_DIM // 4
    inv_freq = ROPE_BASE ** (-jnp.arange(n_freq, dtype=F32) / n_freq)
    pos = jnp.arange(t, dtype=I32)
    row = (pos // GRID_W).astype(F32)[:, None] * inv_freq
    col = (pos % GRID_W).astype(F32)[:, None] * inv_freq
    zeros = jnp.zeros_like(row)
    cos64 = jnp.concatenate([jnp.cos(row), jnp.cos(row), jnp.cos(col), jnp.cos(col)], axis=1)
    sa64 = jnp.concatenate([-jnp.sin(row), zeros, -jnp.sin(col), zeros], axis=1)
    sb64 = jnp.concatenate([zeros, jnp.sin(row), zeros, jnp.sin(col)], axis=1)
    rep = lambda a: jnp.concatenate([a, a], axis=1)
    return rep(cos64), rep(sa64), rep(sb64)


def _rope128(c, cos, sa, sb):
    return c * cos + pltpu.roll(c, LANES - 16, 1) * sa + pltpu.roll(c, 16, 1) * sb


def _proj_kernel(*refs, outs, rope, has_vt):
    x_ref, mod_ref, g_ref, w_ref = refs[:4]
    pos = 4
    if rope:
        cos_ref, sa_ref, sb_ref = refs[pos:pos + 3]
        pos += 3
    if has_vt:
        wvt_ref = refs[pos]
        pos += 1
    out_refs = refs[pos:]
    m = mod_ref[0]
    h = _modulate(x_ref[0], g_ref[...], m[0:1, :], m[1:2, :]).astype(BF16)
    y = _dot(h, w_ref[...])
    for (kind, c0, width, do_rope, scale), o_ref in zip(outs, out_refs):
        for j in range(width // LANES):
            c = y[:, c0 + j * LANES:c0 + (j + 1) * LANES]
            if kind == "mul":
                c = c * y[:, c0 + width + j * LANES:c0 + width + (j + 1) * LANES]
            if do_rope and rope:
                c = _rope128(c, cos_ref[...], sa_ref[...], sb_ref[...])
            if scale != 1.0:
                c = c * scale
            if kind == "heads":
                o_ref[0, j] = c.astype(o_ref.dtype)
            else:
                o_ref[0, :, j * LANES:(j + 1) * LANES] = c.astype(o_ref.dtype)
    if has_vt:
        vt = _dot_nt(wvt_ref[...], h)
        o_ref = out_refs[len(outs)]
        for j in range(vt.shape[0] // LANES):
            o_ref[0, j] = vt[j * LANES:(j + 1) * LANES, :].astype(o_ref.dtype)


def _proj(x, mods, mod_row, gain, w, outs, tables=None, wvt=None):
    b, t, d = x.shape
    tm = min(PROJ_TILE, t)
    n = w.shape[1]
    rope = tables is not None
    in_specs = [pl.BlockSpec((1, tm, d), lambda bi, i: (bi, i, 0)),
                pl.BlockSpec((1, N_MOD, d), lambda bi, i: (mod_row(bi), 0, 0)),
                pl.BlockSpec((1, d), lambda bi, i: (0, 0)),
                pl.BlockSpec((d, n), lambda bi, i: (0, 0))]
    args = [x, mods, gain.reshape(1, d), w]
    if rope:
        in_specs += [pl.BlockSpec((tm, LANES), lambda bi, i: (i, 0))] * 3
        args += list(tables)
    if wvt is not None:
        in_specs.append(pl.BlockSpec(wvt.shape, lambda bi, i: (0, 0)))
        args.append(wvt)
    out_specs, out_shapes = [], []
    for (kind, c0, width, do_rope, scale) in outs:
        if kind == "heads":
            nh = width // LANES
            out_specs.append(pl.BlockSpec((1, nh, tm, LANES), lambda bi, i: (bi, 0, i, 0)))
            out_shapes.append(jax.ShapeDtypeStruct((b, nh, t, LANES), BF16))
        else:
            out_specs.append(pl.BlockSpec((1, tm, width), lambda bi, i: (bi, i, 0)))
            out_shapes.append(jax.ShapeDtypeStruct((b, t, width), BF16))
    if wvt is not None:
        nh = wvt.shape[0] // LANES
        out_specs.append(pl.BlockSpec((1, nh, VT_ROWS, tm), lambda bi, i: (bi, 0, 0, i)))
        out_shapes.append(jax.ShapeDtypeStruct((b, nh, VT_ROWS, t), BF16))
    return pl.pallas_call(
        functools.partial(_proj_kernel, outs=tuple(outs), rope=rope, has_vt=wvt is not None),
        grid=(b, t // tm),
        in_specs=in_specs,
        out_specs=out_specs,
        out_shape=out_shapes,
        compiler_params=_cparams(("parallel", "arbitrary")),
        name="proj",
    )(*args)


def _diff_attn_kernel(*refs, nseg):
    q_ref = refs[0]
    k_refs = refs[1:1 + nseg]
    vt_refs = refs[1 + nseg:1 + 2 * nseg]
    lq1, lk1, lq2, lk2, g_ref, o_ref, s_even, m_even, s_odd, m_odd = refs[1 + 2 * nseg:]
    i = pl.program_id(2)
    offs = [0]
    for k in k_refs:
        offs.append(offs[-1] + k.shape[2])

    @pl.when(i == 0)
    def _():
        s_odd[...] = jnp.zeros(s_odd.shape, F32)
        m_odd[...] = jnp.zeros(m_odd.shape, F32)

    def body(s_w, m_w, s_r, m_r):
        q = q_ref[0, 0]
        lane = lax.broadcasted_iota(I32, q.shape, 1)
        zero = jnp.zeros_like(q)
        qs = (jnp.where(lane < DIFF_HEAD_DIM, q, zero), jnp.where(lane >= DIFF_HEAD_DIM, q, zero))
        for half, qh in enumerate(qs):
            m = None
            for si, k in enumerate(k_refs):
                s = _dot_nt(k[0, 0], qh)
                s_w[half, offs[si]:offs[si + 1], :] = s
                ms = jnp.max(s, axis=0, keepdims=True)
                m = ms if m is None else jnp.maximum(m, ms)
            m_w[half] = m

        lam = (jnp.exp(jnp.sum(lq1[...] * lk1[...], axis=1, keepdims=True))
               - jnp.exp(jnp.sum(lq2[...] * lk2[...], axis=1, keepdims=True)) + LAM_INIT0)
        probs, norms = [], []
        for half in range(2):
            p = [jnp.exp2(s_r[half, offs[si]:offs[si + 1], :] - m_r[half]) for si in range(nseg)]
            norms.append(functools.reduce(jnp.add, [jnp.sum(x, axis=0, keepdims=True) for x in p]))
            probs.append(p)
        c2 = lam * norms[0] / norms[1]
        ot = None
        for si in range(nseg):
            a = (probs[0][si] - probs[1][si] * c2).astype(BF16)
            part = _dot(vt_refs[si][0, 0], a)
            ot = part if ot is None else ot + part
        ot = ot * (1.0 / norms[0])
        msq = jnp.mean(ot * ot, axis=0, keepdims=True)
        on = ot * lax.rsqrt(msq + NORM_EPS) * (g_ref[...] * (1.0 - LAM_INIT0))
        o_ref[0, 0] = on.T.astype(o_ref.dtype)

    @pl.when(lax.rem(i, 2) == 0)
    def _():
        body(s_even, m_even, s_odd, m_odd)

    @pl.when(lax.rem(i, 2) == 1)
    def _():
        body(s_odd, m_odd, s_even, m_even)


def _diff_attn(q, ks, vts, lams, subln_g):
    b, nh, t, _ = q.shape
    tq = min(ROW_TILE, t)
    nq = t // tq
    nseg = len(ks)
    tk = sum(k.shape[2] for k in ks)
    in_specs = [pl.BlockSpec((1, 1, tq, LANES), lambda bi, h, i: (bi, h, jnp.minimum(i, nq - 1), 0))]
    for k in ks:
        in_specs.append(pl.BlockSpec((1, 1, k.shape[2], LANES), lambda bi, h, i: (bi, h, 0, 0)))
    for vt in vts:
        in_specs.append(pl.BlockSpec((1, 1, VT_ROWS, vt.shape[3]), lambda bi, h, i: (bi, h, 0, 0)))
    in_specs += [pl.BlockSpec((1, DIFF_HEAD_DIM), lambda bi, h, i: (0, 0))] * 4
    in_specs.append(pl.BlockSpec((DIFF_V_DIM, 1), lambda bi, h, i: (0, 0)))
    return pl.pallas_call(
        functools.partial(_diff_attn_kernel, nseg=nseg),
        grid=(b, nh, nq + 1),
        in_specs=in_specs,
        out_specs=pl.BlockSpec((1, 1, tq, LANES), lambda bi, h, i: (bi, h, jnp.maximum(i - 1, 0), 0)),
        out_shape=jax.ShapeDtypeStruct((b, nh, t, LANES), BF16),
        scratch_shapes=[pltpu.VMEM((2, tk, tq), F32), pltpu.VMEM((2, 1, tq), F32)] * 2,
        compiler_params=_cparams(("parallel", "arbitrary", "arbitrary")),
        name="diff_attn",
    )(q, *ks, *vts, *[l.reshape(1, DIFF_HEAD_DIM) for l in lams], subln_g.reshape(DIFF_V_DIM, 1))


def _swa_kernel(*refs, t, tq, nq, nwin, kvps):
    sink_ref, q_ref = refs[0], refs[1]
    kw_refs = refs[2:2 + nwin]
    kc_ref = refs[2 + nwin]
    vw_refs = refs[3 + nwin:3 + 2 * nwin]
    vc_ref, o_ref, s_even, m_even, s_odd, m_odd = refs[3 + 2 * nwin:]
    n = pl.program_id(1)
    i = pl.program_id(2)
    c = kc_ref.shape[2]
    span = nwin * LANES
    group = SWA_HEADS // SWA_KV_HEADS

    @pl.when(i == 0)
    def _():
        s_odd[...] = jnp.zeros(s_odd.shape, F32)
        m_odd[...] = jnp.zeros(m_odd.shape, F32)

    def body(s_w, m_w, s_r, m_r):
        ti = jnp.minimum(i, nq - 1)
        kpos = (ti * tq - SWA_WINDOW) + lax.broadcasted_iota(I32, (span, tq), 0)
        qpos = ti * tq + lax.broadcasted_iota(I32, (span, tq), 1)
        ok = jnp.logical_and(jnp.logical_and(kpos >= 0, kpos < t), jnp.abs(qpos - kpos) <= SWA_WINDOW)
        bias = jnp.where(ok, 0.0, NEG_INF)
        lane = lax.broadcasted_iota(I32, (tq, LANES), 1)
        for kk in range(kvps):
            kwin = jnp.concatenate([r[0, kk] for r in kw_refs], axis=0)
            kc = kc_ref[0, kk]
            for j in range(group // 2):
                qc = q_ref[0, kk * (group // 2) + j]
                zero = jnp.zeros_like(qc)
                for half in range(2):
                    hh = kk * group + 2 * j + half
                    keep = (lane < SWA_HEAD_DIM) if half == 0 else (lane >= SWA_HEAD_DIM)
                    qz = jnp.where(keep, qc, zero)
                    s_c = _dot_nt(kc, qz)
                    s_l = _dot_nt(kwin, qz) + bias
                    sink = sink_ref[n * kvps * group + hh] * LOG2E
                    s_w[hh, 0:c, :] = s_c
                    s_w[hh, c:c + span, :] = s_l
                    m_w[hh] = jnp.maximum(jnp.maximum(jnp.max(s_c, axis=0, keepdims=True),
                                                      jnp.max(s_l, axis=0, keepdims=True)), sink)

        for kk in range(kvps):
            vwin = jnp.concatenate([r[0, kk] for r in vw_refs], axis=1)
            vc = vc_ref[0, kk]
            vwin = jnp.where(lax.broadcasted_iota(I32, vwin.shape, 0) == SWA_HEAD_DIM, jnp.ones_like(vwin), vwin)
            vc = jnp.where(lax.broadcasted_iota(I32, vc.shape, 0) == SWA_HEAD_DIM, jnp.ones_like(vc), vc)
            for j in range(group // 2):
                res = []
                for half in range(2):
                    hh = kk * group + 2 * j + half
                    m = m_r[hh]
                    p_c = jnp.exp2(s_r[hh, 0:c, :] - m).astype(BF16)
                    p_l = jnp.exp2(s_r[hh, c:c + span, :] - m).astype(BF16)
                    ot = _dot(vc, p_c) + _dot(vwin, p_l)
                    sink = sink_ref[n * kvps * group + hh] * LOG2E
                    den = ot[SWA_HEAD_DIM:SWA_HEAD_DIM + 1, :] + jnp.exp2(sink - m)
                    res.append(ot[0:SWA_HEAD_DIM, :] * (1.0 / den))
                o_ref[0, kk * (group // 2) + j] = jnp.concatenate(res, axis=0).T.astype(o_ref.dtype)

    @pl.when(lax.rem(i, 2) == 0)
    def _():
        body(s_even, m_even, s_odd, m_odd)

    @pl.when(lax.rem(i, 2) == 1)
    def _():
        body(s_odd, m_odd, s_even, m_even)


def _swa_attn(sink, q, kl, vtl, kc, vtc):
    b, nchunk, t, _ = q.shape
    nkv = kl.shape[1]
    cpk = nchunk // nkv
    c = kc.shape[2]
    tq = ROW_TILE
    nq = t // tq
    nblk = t // LANES
    bpt = tq // LANES
    nwin = bpt + 2 * (SWA_WINDOW // LANES)
    kvps = SWA_KV_PER_STEP

    def kblk(i, j, lag):
        ti = jnp.clip(i - lag, 0, nq - 1)
        return jnp.clip(ti * bpt - SWA_WINDOW // LANES + j, 0, nblk - 1)

    in_specs = [pl.BlockSpec(memory_space=pltpu.SMEM),
                pl.BlockSpec((1, kvps * cpk, tq, LANES), lambda bi, n, i: (bi, n, jnp.minimum(i, nq - 1), 0))]
    for j in range(nwin):
        in_specs.append(pl.BlockSpec((1, kvps, LANES, LANES), lambda bi, n, i, j=j: (bi, n, kblk(i, j, 0), 0)))
    in_specs.append(pl.BlockSpec((1, kvps, c, LANES), lambda bi, n, i: (bi, n, 0, 0)))
    for j in range(nwin):
        in_specs.append(pl.BlockSpec((1, kvps, LANES, LANES), lambda bi, n, i, j=j: (bi, n, 0, kblk(i, j, 1))))
    in_specs.append(pl.BlockSpec((1, kvps, LANES, c), lambda bi, n, i: (bi, n, 0, 0)))
    heads = kvps * SWA_HEADS // nkv
    scratch = [pltpu.VMEM((heads, c + nwin * LANES, tq), F32), pltpu.VMEM((heads, 1, tq), F32)] * 2
    return pl.pallas_call(
        functools.partial(_swa_kernel, t=t, tq=tq, nq=nq, nwin=nwin, kvps=kvps),
        grid=(b, nkv // kvps, nq + 1),
        in_specs=in_specs,
        out_specs=pl.BlockSpec((1, kvps * cpk, tq, LANES), lambda bi, n, i: (bi, n, jnp.maximum(i - 1, 0), 0)),
        out_shape=jax.ShapeDtypeStruct((b, nchunk, t, LANES), BF16),
        scratch_shapes=scratch,
        compiler_params=_cparams(("parallel", "arbitrary", "arbitrary")),
        name="swa_attn",
    )(sink, q, *([kl] * nwin), kc, *([vtl] * nwin), vtc)


def _outproj_kernel(*refs, conv, tm, nq):
    if conv:
        (x_ref, mod_ref, bg_ref, p_ref, pprev_ref, pnext_ref, cw_ref, o_ref, w_ref,
         g2_ref, rwh_ref, rwl_ref, xn_ref, haug_ref, afft_ref, x_even, x_odd) = refs
    else:
        (x_ref, mod_ref, o_ref, w_ref, g2_ref, rwh_ref, rwl_ref, xn_ref, haug_ref, afft_ref, x_even, x_odd) = refs
    i = pl.program_id(1)

    @pl.when(i == 0)
    def _():
        x_odd[...] = jnp.zeros(x_odd.shape, F32)

    def body(x_w, x_r):
        if conv:
            ti = jnp.minimum(i, nq - 1)
            p = p_ref[0].astype(F32)
            row = lax.broadcasted_iota(I32, p.shape, 0)
            halo_prev = jnp.where(ti > 0, pprev_ref[0, BF16_ROWS - 1:BF16_ROWS, :].astype(F32), 0.0)
            halo_next = jnp.where(ti < nq - 1, pnext_ref[0, 0:1, :].astype(F32), 0.0)
            p_prev = jnp.where(row == 0, halo_prev, pltpu.roll(p, 1, 0))
            p_next = jnp.where(row == tm - 1, halo_next, pltpu.roll(p, tm - 1, 0))
            cw = cw_ref[...]
            cv = p_prev * cw[0:1, :] + p * cw[1:2, :] + p_next * cw[2:3, :]
            u = (bg_ref[0].astype(F32) * cv).astype(BF16)
            lhs = jnp.concatenate([u] + [o_ref[0, j] for j in range(DIFF_HEADS)], axis=1)
        else:
            lhs = jnp.concatenate([o_ref[0, j] for j in range(o_ref.shape[1])], axis=1)
        m = mod_ref[0]
        xn = x_ref[0] + m[2:3, :] * _dot(lhs, w_ref[...])
        xn_ref[0] = xn
        x_w[...] = xn

        h = _modulate(x_r[...], g2_ref[...], m[3:4, :], m[4:5, :])
        d = h.shape[1]
        haug_ref[0, :, 0:d] = h
        h_hi = h.astype(BF16)
        h_lo = (h - h_hi.astype(F32)).astype(BF16)
        logits = _dot(h_hi, rwh_ref[...]) + _dot(h_hi, rwl_ref[...]) + _dot(h_lo, rwh_ref[...])
        lane = lax.broadcasted_iota(I32, logits.shape, 1)
        logits = jnp.where(lane < N_EXPERTS, logits, NEG_INF)
        e = jnp.exp(logits - jnp.max(logits, axis=1, keepdims=True))
        aff = e / jnp.sum(e, axis=1, keepdims=True)
        haug_ref[0, :, d:d + AUG] = aff
        afft_ref[0] = aff.T[0:N_EXPERTS, :]

    @pl.when(lax.rem(i, 2) == 0)
    def _():
        body(x_even, x_odd)

    @pl.when(lax.rem(i, 2) == 1)
    def _():
        body(x_odd, x_even)


def _outproj(x, mods, mod_row, w_out, n2g, rw_hi, rw_lo, o, conv_args=None):
    b, t, d = x.shape
    tm = min(ROW_TILE, t)
    nq = t // tm
    conv = conv_args is not None
    cur = lambda i: jnp.minimum(i, nq - 1)
    prv = lambda i: jnp.maximum(i - 1, 0)
    xmap = lambda bi, i: (bi, cur(i), 0)
    in_specs = [pl.BlockSpec((1, tm, d), xmap),
                pl.BlockSpec((1, N_MOD, d), lambda bi, i: (mod_row(bi), 0, 0))]
    args = [x, mods]
    if conv:
        bg, p, cw = conv_args
        hb = tm // BF16_ROWS
        nhb = t // BF16_ROWS
        in_specs += [pl.BlockSpec((1, tm, CONV_DIM), xmap),
                     pl.BlockSpec((1, tm, CONV_DIM), xmap),
                     pl.BlockSpec((1, BF16_ROWS, CONV_DIM), lambda bi, i: (bi, jnp.maximum(cur(i) * hb - 1, 0), 0)),
                     pl.BlockSpec((1, BF16_ROWS, CONV_DIM),
                                  lambda bi, i: (bi, jnp.minimum((cur(i) + 1) * hb, nhb - 1), 0)),
                     pl.BlockSpec((3, CONV_DIM), lambda bi, i: (0, 0)),
                     pl.BlockSpec((1, DIFF_HEADS, tm, LANES), lambda bi, i: (bi, 0, cur(i), 0))]
        args += [bg, p, p, p, cw, o]
    else:
        in_specs.append(pl.BlockSpec((1, o.shape[1], tm, LANES), lambda bi, i: (bi, 0, cur(i), 0)))
        args.append(o)
    in_specs += [pl.BlockSpec(w_out.shape, lambda bi, i: (0, 0)),
                 pl.BlockSpec((1, d), lambda bi, i: (0, 0)),
                 pl.BlockSpec((d, LANES), lambda bi, i: (0, 0)),
                 pl.BlockSpec((d, LANES), lambda bi, i: (0, 0))]
    args += [w_out, n2g.reshape(1, d), rw_hi, rw_lo]
    return pl.pallas_call(
        functools.partial(_outproj_kernel, conv=conv, tm=tm, nq=nq),
        grid=(b, nq + 1),
        in_specs=in_specs,
        out_specs=[pl.BlockSpec((1, tm, d), xmap),
                   pl.BlockSpec((1, tm, d + AUG), lambda bi, i: (bi, prv(i), 0)),
                   pl.BlockSpec((1, N_EXPERTS, tm), lambda bi, i: (bi, 0, prv(i)))],
        out_shape=[jax.ShapeDtypeStruct((b, t, d), F32),
                   jax.ShapeDtypeStruct((b, t, d + AUG), F32),
                   jax.ShapeDtypeStruct((b, N_EXPERTS, t), F32)],
        scratch_shapes=[pltpu.VMEM((tm, d), F32)] * 2,
        compiler_params=_cparams(("parallel", "arbitrary")),
        name="outproj",
    )(*args)


def _route_kernel(aff_ref, idx_ref, post_ref, offs_ref, pinc_ref, *, t, cap, capp, lc):
    nch = t // lc
    ne = N_EXPERTS
    aff = aff_ref[0]

    def search(i, lo):
        cand = lo | lax.shift_left(jnp.int32(1), 30 - i)
        cnt = jnp.sum(jnp.where(aff >= pltpu.bitcast(cand, F32), 1.0, 0.0), axis=1, keepdims=True)
        return jnp.where(cnt >= cap, cand, lo)

    thr_bits = lax.fori_loop(0, 31, search, jnp.zeros((ne, 1), I32))
    thr = pltpu.bitcast(thr_bits, F32)
    gt = aff > thr
    eq = aff == thr
    need = cap - jnp.sum(jnp.where(gt, 1.0, 0.0), axis=1, keepdims=True)

    r_i = lax.broadcasted_iota(I32, (lc, lc), 0)
    c_i = lax.broadcasted_iota(I32, (lc, lc), 1)
    upper = jnp.where(r_i < c_i, 1.0, 0.0).astype(BF16)

    def excl_prefix(x):
        outs, offs = [], []
        carry = jnp.zeros((ne, 1), F32)
        for c in range(nch):
            xc = x[:, c * lc:(c + 1) * lc]
            offs.append(carry)
            outs.append(_dot(xc.astype(BF16), upper) + carry)
            carry = carry + jnp.sum(xc, axis=1, keepdims=True)
        offs.append(carry)
        return jnp.concatenate(outs, axis=1), offs

    eqf = jnp.where(eq, 1.0, 0.0)
    eq_rank, _ = excl_prefix(eqf)
    sel = jnp.logical_or(gt, jnp.logical_and(eq, eq_rank < need))
    self_ = jnp.where(sel, 1.0, 0.0)
    pos, offs = excl_prefix(self_)

    lane = lax.broadcasted_iota(I32, (ne, LANES), 1)
    om = jnp.zeros((ne, LANES), F32)
    for c, o in enumerate(offs):
        om = jnp.where(lane == c, o, om)
    offs_ref[0] = om

    posm = jnp.where(sel, pos, -1.0)
    pad = jnp.full((LANES - ne, lc), -1.0, F32)
    for c in range(nch):
        blk = jnp.concatenate([posm[:, c * lc:(c + 1) * lc], pad], axis=0)
        post_ref[0, c * lc:(c + 1) * lc, :] = blk.T
    pinc = pos + self_
    nt = t // LANES
    for k in range(nt):
        pinc_ref[k] = pinc[:, k * LANES:(k + 1) * LANES]

    sub = lax.broadcasted_iota(I32, (SUBLANES, LANES), 0).astype(F32)
    lane_c = lax.broadcasted_iota(I32, (capp, LANES), 1)
    unroll = min(8, nt)

    def per_expert(e, acc_m):
        cols = []
        for jb in range(capp // LANES):
            def tiles(kk, acc, jb=jb):
                for u in range(unroll):
                    q = pinc_ref[kk * unroll + u, pl.ds(e, 1), :] - sub - float(jb * LANES)
                    acc = acc + jnp.concatenate(
                        [jnp.where(q <= float(SUBLANES * i), 1.0, 0.0) for i in range(LANES // SUBLANES)], axis=0)
                return acc
            acc = lax.fori_loop(0, nt // unroll, tiles, jnp.zeros((LANES, LANES), F32))
            cols.append(jnp.sum(acc, axis=1, keepdims=True))
        col = jnp.concatenate(cols, axis=0)
        return jnp.where(lane_c == e, col, acc_m)

    idx_m = lax.fori_loop(0, ne, per_expert, jnp.zeros((capp, LANES), F32))
    base = pl.program_id(0) * t
    idx_ref[0] = idx_m.T[0:ne, :].astype(I32) + base


def _route(aff_t, cap):
    b, ne, t = aff_t.shape
    lc = min(256, t)
    capp = max(cap, LANES)
    nch = t // lc
    return pl.pallas_call(
        functools.partial(_route_kernel, t=t, cap=cap, capp=capp, lc=lc),
        grid=(b,),
        in_specs=[pl.BlockSpec((1, ne, t), lambda bi: (bi, 0, 0))],
        out_specs=[pl.BlockSpec((1, ne, capp), lambda bi: (bi, 0, 0)),
                   pl.BlockSpec((1, t, LANES), lambda bi: (bi, 0, 0)),
                   pl.BlockSpec((1, ne, LANES), lambda bi: (bi, 0, 0))],
        out_shape=[jax.ShapeDtypeStruct((b, ne, capp), I32),
                   jax.ShapeDtypeStruct((b, t, LANES), F32),
                   jax.ShapeDtypeStruct((b, ne, LANES), F32)],
        scratch_shapes=[pltpu.VMEM((t // LANES, ne, LANES), F32)],
        compiler_params=_cparams(("arbitrary",)),
        name="route",
    )(aff_t)


def _ffn_kernel(idx_ref, haug_ref, wg_ref, wu_ref, wd_ref, y_ref, hbuf, w_in, w_dn, gsem, *, rows, d):
    e = pl.program_id(0)
    step = e * pl.num_programs(1) + pl.program_id(1)
    nsteps = pl.num_programs(0) * pl.num_programs(1)

    @pl.when(pl.program_id(1) == 0)
    def _():
        w_in[0] = wg_ref[0].astype(BF16)
        w_in[1] = wu_ref[0].astype(BF16)
        w_dn[...] = wd_ref[0].astype(BF16)

    def row_copy(src_row, dst_slot, dst_row):
        return pltpu.make_async_copy(haug_ref.at[pl.ds(src_row, 1)],
                                     hbuf.at[dst_slot, pl.ds(dst_row, 1)], gsem.at[dst_slot])

    def wait_rows(sl):
        pltpu.make_async_copy(haug_ref.at[pl.ds(0, rows)], hbuf.at[sl], gsem.at[sl]).wait()

    @pl.when(step == 0)
    def _():
        def body(r, carry):
            row_copy(idx_ref[r], 0, r).start()
            return carry
        lax.fori_loop(0, rows, body, 0)

    def run(slot):
        base = jnp.minimum(step + 1, nsteps - 1) * rows
        for r in range(rows):
            row_copy(idx_ref[base + r], 1 - slot, r).start()

        wait_rows(slot)
        hrow = hbuf[slot]
        hs = hrow[:, 0:d].astype(BF16)
        lane = lax.broadcasted_iota(I32, (rows, AUG), 1)
        gate = jnp.sum(jnp.where(lane == e, hrow[:, d:d + AUG], 0.0), axis=1, keepdims=True)
        a = _dot(hs, w_in[0])
        u = _dot(hs, w_in[1])
        hm = (a * jax.nn.sigmoid(a) * u).astype(BF16)
        y_ref[0] = (_dot(hm, w_dn[...]) * gate).astype(y_ref.dtype)

        @pl.when(step == nsteps - 1)
        def _():
            wait_rows(1 - slot)

    for parity in range(2):
        pl.when(lax.rem(step, 2) == parity)(functools.partial(run, parity))


def _ffn(idx_flat, haug, wg, wu, wd, rows):
    ne, d, f = wg.shape
    n_rows_total = idx_flat.shape[0]
    steps = n_rows_total // (ne * rows)
    grid_spec = pltpu.PrefetchScalarGridSpec(
        num_scalar_prefetch=1,
        grid=(ne, steps),
        in_specs=[pl.BlockSpec(memory_space=pl.ANY),
                  pl.BlockSpec((1, d, f), lambda e, s, idx: (e, 0, 0)),
                  pl.BlockSpec((1, d, f), lambda e, s, idx: (e, 0, 0)),
                  pl.BlockSpec((1, f, d), lambda e, s, idx: (e, 0, 0))],
        out_specs=pl.BlockSpec((1, rows, d), lambda e, s, idx: (e, s, 0)),
        scratch_shapes=[pltpu.VMEM((2, rows, d + AUG), F32),
                        pltpu.VMEM((2, d, f), BF16),
                        pltpu.VMEM((f, d), BF16),
                        pltpu.SemaphoreType.DMA((2,))],
    )
    return pl.pallas_call(
        functools.partial(_ffn_kernel, rows=rows, d=d),
        grid_spec=grid_spec,
        out_shape=jax.ShapeDtypeStruct((ne, steps * rows, d), BF16),
        compiler_params=_cparams(("arbitrary", "arbitrary")),
        name="ffn",
    )(idx_flat, haug, wg, wu, wd)


def _combine_kernel(ws_ref, nr_ref, x_ref, mod_ref, post_ref, fg_ref, y_ref, o_ref, stage, acc, sem,
                    *, cap, tm, final_norm):
    nchunk = pl.num_programs(1)
    step = pl.program_id(0) * nchunk + pl.program_id(1)
    nsteps = pl.num_programs(0) * nchunk
    slot = lax.rem(step, 2)
    ne = N_EXPERTS
    w = COMB_W
    per = LANES // w
    total_rows = y_ref.shape[1]
    lane_row = lax.broadcasted_iota(I32, (1, LANES), 1)

    def window(st, e, r):
        sample = lax.div(st, nchunk)
        nominal = ws_ref[st * ne + e] + r * w
        grow = pl.multiple_of(jnp.minimum(sample * cap + nominal, total_rows - w), BF16_ROWS)
        return nominal, grow, grow - sample * cap

    def copy(e, grow, sl, si):
        return pltpu.make_async_copy(y_ref.at[e, pl.ds(grow, w)], stage.at[sl, pl.ds(e * w, w)], sem.at[si])

    def fetch(st, r, sl, si):
        for e in range(ne):
            copy(e, window(st, e, r)[1], sl, si).start()

    def wait(sl, si):
        for e in range(ne):
            copy(e, 0, sl, si).wait()

    def gathered(st, r, sl):
        pt = post_ref[0]
        blocks = []
        for g in range(ne // per):
            tgt = jnp.zeros((1, LANES), F32)
            mine = jnp.zeros((tm, LANES), F32)
            for k in range(per):
                e = g * per + k
                nominal, _, first = window(st, e, r)
                inb = jnp.logical_and(lane_row >= k * w, lane_row < (k + 1) * w)
                slot = first - k * w + lane_row
                tgt = jnp.where(inb, jnp.where(slot >= nominal, slot, -2).astype(F32), tgt)
                mine = jnp.where(inb, pt[:, e:e + 1], mine)
            blocks.append(jnp.where(mine == tgt, 1.0, 0.0).astype(BF16))
        onehot = jnp.concatenate(blocks, axis=1)
        return _dot(onehot, stage[sl])

    @pl.when(step == 0)
    def _():
        fetch(0, 0, 0, 0)

    @pl.when(step + 1 < nsteps)
    def _():
        fetch(step + 1, 0, 1 - slot, 1 - slot)

    wait(slot, slot)
    acc[...] = gathered(step, 0, slot)

    def extra_round(r, carry):
        fetch(step, r, slot, 2)
        wait(slot, 2)
        acc[...] += gathered(step, r, slot)
        return carry

    lax.fori_loop(1, nr_ref[step], extra_round, 0)
    m = mod_ref[0]
    xn = x_ref[0] + m[5:6, :] * acc[...]
    if final_norm:
        ms = jnp.mean(xn * xn, axis=-1, keepdims=True)
        xn = xn * lax.rsqrt(ms + NORM_EPS) * fg_ref[...]
    o_ref[0] = xn


def _combine(ws_flat, nr_flat, x, mods, mod_row, post, y, cap, final_g=None):
    b, t, d = x.shape
    tm = min(ROW_TILE, t)
    ne = N_EXPERTS
    final_norm = final_g is not None
    fg = (final_g if final_norm else jnp.ones((d,), F32)).reshape(1, d)
    grid_spec = pltpu.PrefetchScalarGridSpec(
        num_scalar_prefetch=2,
        grid=(b, t // tm),
        in_specs=[pl.BlockSpec((1, tm, d), lambda bi, i, a, c: (bi, i, 0)),
                  pl.BlockSpec((1, N_MOD, d), lambda bi, i, a, c: (mod_row(bi), 0, 0)),
                  pl.BlockSpec((1, tm, LANES), lambda bi, i, a, c: (bi, i, 0)),
                  pl.BlockSpec((1, d), lambda bi, i, a, c: (0, 0)),
                  pl.BlockSpec(memory_space=pl.ANY)],
        out_specs=pl.BlockSpec((1, tm, d), lambda bi, i, a, c: (bi, i, 0)),
        scratch_shapes=[pltpu.VMEM((2, ne * COMB_W, d), BF16),
                        pltpu.VMEM((tm, d), F32),
                        pltpu.SemaphoreType.DMA((3,))],
    )
    return pl.pallas_call(
        functools.partial(_combine_kernel, cap=cap, tm=tm, final_norm=final_norm),
        grid_spec=grid_spec,
        out_shape=jax.ShapeDtypeStruct((b, t, d), F32),
        compiler_params=_cparams(("arbitrary", "arbitrary")),
        name="combine",
    )(ws_flat, nr_flat, x, mods, post, fg, y)


def _moe(xn, haug, aff_t, mods, mod_row, wg, wu, wd, rows_per_step, final_g=None):
    b, t, d = xn.shape
    ne = N_EXPERTS
    cap = CAPACITY_FACTOR * t // ne
    idx, post, offs = _route(aff_t, cap)
    idx_flat = jnp.transpose(idx[:, :, :cap], (1, 0, 2)).reshape(-1)
    y = _ffn(idx_flat, haug.reshape(b * t, d + AUG), wg, wu, wd, rows_per_step)
    tm = min(ROW_TILE, t)
    nchunk = t // tm
    lc = min(256, t)
    per = tm // lc
    offs_i = offs.astype(I32)[:, :, 0:t // lc + 1:per]
    start = jnp.transpose(offs_i[:, :, :nchunk], (0, 2, 1))
    end = jnp.transpose(offs_i[:, :, 1:], (0, 2, 1))
    ws = (start // BF16_ROWS) * BF16_ROWS
    nr = jnp.maximum(jnp.max((end - ws + COMB_W - 1) // COMB_W, axis=2), 1)
    return _combine(ws.reshape(-1), nr.reshape(-1), xn, mods, mod_row, post, y, cap, final_g)


def _split_hi_lo(w):
    hi = w.astype(BF16)
    lo = (w - hi.astype(F32)).astype(BF16)
    return hi, lo


def _router_weights(rw):
    d, ne = rw.shape
    pad = jnp.zeros((d, LANES - ne), F32)
    return _split_hi_lo(jnp.concatenate([rw, pad], axis=1))


def kernel(x, c, ctx, c_ctx, l0_ada_w, l0_ada_b, l0_norm1_g, l0_w_in, l0_conv_w, l0_lambda_q1, l0_lambda_k1, l0_lambda_q2, l0_lambda_k2, l0_subln_g, l0_w_out, l0_norm2_g, l0_router_w, l0_exp_w_gate, l0_exp_w_up, l0_exp_w_down, l1_ada_w, l1_ada_b, l1_norm1_g, l1_w_qkv, l1_sink, l1_w_out, l1_norm2_g, l1_router_w, l1_exp_w_gate, l1_exp_w_up, l1_exp_w_down, final_norm_g):
    b, t, d = x.shape
    nctx = ctx.shape[1]
    assert b + 1 <= MOD_ROWS
    lat_row = lambda bi: bi
    ctx_row = lambda bi: b

    cc = jnp.zeros((MOD_ROWS, d), F32).at[:b].set(c).at[b].set(c_ctx)
    mods0 = _ada(cc, l0_ada_w, l0_ada_b).reshape(MOD_ROWS, N_MOD, d)
    mods1 = _ada(cc, l1_ada_w, l1_ada_b).reshape(MOD_ROWS, N_MOD, d)
    tables = _rope_tables(t)
    qscale = LOG2E * DIFF_HEAD_DIM ** -0.5

    w_in = l0_w_in.astype(BF16)
    cd = CONV_DIM
    wvt = jnp.transpose(l0_w_in[:, 3 * cd + 1024:]).astype(BF16)
    outs0 = [("plain", 0, cd, False, 1.0),
             ("mul", cd, cd, False, 1.0),
             ("heads", 3 * cd, 512, True, qscale),
             ("heads", 3 * cd + 512, 512, True, 1.0)]
    w_main = w_in[:, :3 * cd + 1024]
    bg_l, cx_l, q_l, k_l, vt_l = _proj(x, mods0, lat_row, l0_norm1_g, w_main, outs0, tables, wvt)
    bg_c, cx_c, q_c, k_c, vt_c = _proj(ctx, mods0, ctx_row, l0_norm1_g, w_main, outs0, None, wvt)
    lams = (l0_lambda_q1, l0_lambda_k1, l0_lambda_q2, l0_lambda_k2)
    o_l = _diff_attn(q_l, [k_c, k_l], [vt_c, vt_l], lams, l0_subln_g)
    o_c = _diff_attn(q_c, [k_c], [vt_c], lams, l0_subln_g)

    w_out0 = l0_w_out.astype(BF16)
    rw0 = _router_weights(l0_router_w)
    wg0, wu0, wd0 = l0_exp_w_gate, l0_exp_w_up, l0_exp_w_down
    x1, haug, aff_t = _outproj(x, mods0, lat_row, w_out0, l0_norm2_g, *rw0, o_l,
                               conv_args=(bg_l, cx_l, l0_conv_w))
    x = _moe(x1, haug, aff_t, mods0, lat_row, wg0, wu0, wd0, CAPACITY_FACTOR * t // N_EXPERTS)
    c1, haug_c, aff_tc = _outproj(ctx, mods0, ctx_row, w_out0, l0_norm2_g, *rw0, o_c,
                                  conv_args=(bg_c, cx_c, l0_conv_w))
    ctx = _moe(c1, haug_c, aff_tc, mods0, ctx_row, wg0, wu0, wd0, b * (CAPACITY_FACTOR * nctx // N_EXPERTS))

    qw = SWA_HEADS * SWA_HEAD_DIM
    kvw = SWA_KV_HEADS * SWA_HEAD_DIM
    wq = l1_w_qkv[:, :qw]
    wk = l1_w_qkv[:, qw:qw + kvw].reshape(d, SWA_KV_HEADS, SWA_HEAD_DIM)
    wv = l1_w_qkv[:, qw + kvw:].reshape(d, SWA_KV_HEADS, SWA_HEAD_DIM)
    wk2 = jnp.concatenate([wk, wk], axis=2).reshape(d, 2 * kvw)
    wv2 = jnp.concatenate([wv, jnp.zeros_like(wv)], axis=2).reshape(d, 2 * kvw)
    w_qk = jnp.concatenate([wq, wk2], axis=1).astype(BF16)
    wvt1 = jnp.transpose(wv2).astype(BF16)
    outs1 = [("heads", 0, qw, True, LOG2E * SWA_HEAD_DIM ** -0.5),
             ("heads", qw, 2 * kvw, True, 1.0)]
    q1, k1, vt1 = _proj(x, mods1, lat_row, l1_norm1_g, w_qk, outs1, tables, wvt1)
    k1c, vt1c = _proj(ctx, mods1, ctx_row, l1_norm1_g, w_qk[:, qw:], [("heads", 0, 2 * kvw, False, 1.0)], None, wvt1)
    o1 = _swa_attn(l1_sink, q1, k1, vt1, k1c, vt1c)

    rw1 = _router_weights(l1_router_w)
    wg1, wu1, wd1 = l1_exp_w_gate, l1_exp_w_up, l1_exp_w_down
    x1, haug, aff_t = _outproj(x, mods1, lat_row, l1_w_out.astype(BF16), l1_norm2_g, *rw1, o1)
    return _moe(x1, haug, aff_t, mods1, lat_row, wg1, wu1, wd1, CAPACITY_FACTOR * t // N_EXPERTS,
                final_g=final_norm_g)
```

```python
import functools
import math

import jax
import jax.numpy as jnp
from jax import lax
from jax.experimental import pallas as pl
from jax.experimental.pallas import tpu as pltpu

F32 = jnp.float32
BF16 = jnp.bfloat16
I32 = jnp.int32

GRID_W = 64
ROPE_BASE = 10000.0
NORM_EPS = 1e-6
NEG_INF = -1e30
N_MOD = 6

CONV_DIM = 512
DIFF_HEADS = 4
DIFF_HEAD_DIM = 64
DIFF_V_DIM = 128
LAM_INIT0 = 0.8 - 0.6 * math.exp(-0.3 * 0)

SWA_HEADS = 16
SWA_KV_HEADS = 4
SWA_HEAD_DIM = 64
SWA_WINDOW = 128

N_EXPERTS = 16
CAPACITY_FACTOR = 2

LANES = 128
SUBLANES = 8
MOD_ROWS = 16
AUG = LANES
ROW_TILE = 256
PROJ_TILE = 512
COMB_W = 64
BF16_ROWS = 16
VT_ROWS = DIFF_V_DIM
SWA_KV_PER_STEP = 4
LOG2E = math.log2(math.e)
VMEM_LIMIT = 56 * 1024 * 1024


def _cparams(sem):
    return pltpu.CompilerParams(dimension_semantics=sem, vmem_limit_bytes=VMEM_LIMIT)


def _dot(a, b):
    return jnp.dot(a, b, preferred_element_type=F32)


def _dot_nt(a, b):
    return lax.dot_general(a, b, (((1,), (1,)), ((), ())), preferred_element_type=F32)


def _modulate(x, gain, shift, scale):
    ms = jnp.mean(x * x, axis=-1, keepdims=True)
    return x * lax.rsqrt(ms + NORM_EPS) * (gain * (1.0 + scale)) + shift


def _ada_kernel(c_ref, w_ref, b_ref, o_ref):
    c = c_ref[...]
    s = (c * jax.nn.sigmoid(c)).astype(BF16)
    o_ref[...] = _dot(s, w_ref[...].astype(BF16)) + b_ref[...]


def _ada(cc, w, b):
    d, n = w.shape
    tn = 512
    return pl.pallas_call(
        _ada_kernel,
        grid=(n // tn,),
        in_specs=[pl.BlockSpec((MOD_ROWS, d), lambda j: (0, 0)),
                  pl.BlockSpec((d, tn), lambda j: (0, j)),
                  pl.BlockSpec((1, tn), lambda j: (0, j))],
        out_specs=pl.BlockSpec((MOD_ROWS, tn), lambda j: (0, j)),
        out_shape=jax.ShapeDtypeStruct((MOD_ROWS, n), F32),
        compiler_params=_cparams(("arbitrary",)),
        name="ada",
    )(cc, w, b.reshape(1, n))


def _rope_tables(t):
    n_freq = DIFF_HEAD_DIM // 4
    inv_freq = ROPE_BASE ** (-jnp.arange(n_freq, dtype=F32) / n_freq)
    pos = jnp.arange(t, dtype=I32)
    row = (pos // GRID_W).astype(F32)[:, None] * inv_freq
    col = (pos % GRID_W).astype(F32)[:, None] * inv_freq
    zeros = jnp.zeros_like(row)
    cos64 = jnp.concatenate([jnp.cos(row), jnp.cos(row), jnp.cos(col), jnp.cos(col)], axis=1)
    sa64 = jnp.concatenate([-jnp.sin(row), zeros, -jnp.sin(col), zeros], axis=1)
    sb64 = jnp.concatenate([zeros, jnp.sin(row), zeros, jnp.sin(col)], axis=1)
    rep = lambda a: jnp.concatenate([a, a], axis=1)
    return rep(cos64), rep(sa64), rep(sb64)


def _rope128(c, cos, sa, sb):
    return c * cos + pltpu.roll(c, LANES - 16, 1) * sa + pltpu.roll(c, 16, 1) * sb


def _proj_kernel(*refs, outs, rope, has_vt):
    x_ref, mod_ref, g_ref, w_ref = refs[:4]
    pos = 4
    if rope:
        cos_ref, sa_ref, sb_ref = refs[pos:pos + 3]
        pos += 3
    if has_vt:
        wvt_ref = refs[pos]
        pos += 1
    out_refs = refs[pos:]
    m = mod_ref[0]
    h = _modulate(x_ref[0], g_ref[...], m[0:1, :], m[1:2, :]).astype(BF16)
    y = _dot(h, w_ref[...])
    for (kind, c0, width, do_rope, scale), o_ref in zip(outs, out_refs):
        for j in range(width // LANES):
            c = y[:, c0 + j * LANES:c0 + (j + 1) * LANES]
            if kind == "mul":
                c = c * y[:, c0 + width + j * LANES:c0 + width + (j + 1) * LANES]
            if do_rope and rope:
                c = _rope128(c, cos_ref[...], sa_ref[...], sb_ref[...])
            if scale != 1.0:
                c = c * scale
            if kind == "heads":
                o_ref[0, j] = c.astype(o_ref.dtype)
            else:
                o_ref[0, :, j * LANES:(j + 1) * LANES] = c.astype(o_ref.dtype)
    if has_vt:
        vt = _dot_nt(wvt_ref[...], h)
        o_ref = out_refs[len(outs)]
        for j in range(vt.shape[0] // LANES):
            o_ref[0, j] = vt[j * LANES:(j + 1) * LANES, :].astype(o_ref.dtype)


def _proj(x, mods, mod_row, gain, w, outs, tables=None, wvt=None):
    b, t, d = x.shape
    tm = min(PROJ_TILE, t)
    n = w.shape[1]
    rope = tables is not None
    in_specs = [pl.BlockSpec((1, tm, d), lambda bi, i: (bi, i, 0)),
                pl.BlockSpec((1, N_MOD, d), lambda bi, i: (mod_row(bi), 0, 0)),
                pl.BlockSpec((1, d), lambda bi, i: (0, 0)),
                pl.BlockSpec((d, n), lambda bi, i: (0, 0))]
    args = [x, mods, gain.reshape(1, d), w]
    if rope:
        in_specs += [pl.BlockSpec((tm, LANES), lambda bi, i: (i, 0))] * 3
        args += list(tables)
    if wvt is not None:
        in_specs.append(pl.BlockSpec(wvt.shape, lambda bi, i: (0, 0)))
        args.append(wvt)
    out_specs, out_shapes = [], []
    for (kind, c0, width, do_rope, scale) in outs:
        if kind == "heads":
            nh = width // LANES
            out_specs.append(pl.BlockSpec((1, nh, tm, LANES), lambda bi, i: (bi, 0, i, 0)))
            out_shapes.append(jax.ShapeDtypeStruct((b, nh, t, LANES), BF16))
        else:
            out_specs.append(pl.BlockSpec((1, tm, width), lambda bi, i: (bi, i, 0)))
            out_shapes.append(jax.ShapeDtypeStruct((b, t, width), BF16))
    if wvt is not None:
        nh = wvt.shape[0] // LANES
        out_specs.append(pl.BlockSpec((1, nh, VT_ROWS, tm), lambda bi, i: (bi, 0, 0, i)))
        out_shapes.append(jax.ShapeDtypeStruct((b, nh, VT_ROWS, t), BF16))
    return pl.pallas_call(
        functools.partial(_proj_kernel, outs=tuple(outs), rope=rope, has_vt=wvt is not None),
        grid=(b, t // tm),
        in_specs=in_specs,
        out_specs=out_specs,
        out_shape=out_shapes,
        compiler_params=_cparams(("parallel", "arbitrary")),
        name="proj",
    )(*args)


def _diff_attn_kernel(*refs, nseg):
    q_ref = refs[0]
    k_refs = refs[1:1 + nseg]
    vt_refs = refs[1 + nseg:1 + 2 * nseg]
    lq1, lk1, lq2, lk2, g_ref, o_ref, s_even, m_even, s_odd, m_odd = refs[1 + 2 * nseg:]
    i = pl.program_id(1)
    offs = [0]
    for k in k_refs:
        offs.append(offs[-1] + k.shape[2])

    @pl.when(i == 0)
    def _():
        s_odd[...] = jnp.zeros(s_odd.shape, F32)
        m_odd[...] = jnp.zeros(m_odd.shape, F32)

    def body(s_w, m_w, s_r, m_r):
        q = q_ref[0, 0]
        lane = lax.broadcasted_iota(I32, q.shape, 1)
        zero = jnp.zeros_like(q)
        qs = (jnp.where(lane < DIFF_HEAD_DIM, q, zero), jnp.where(lane >= DIFF_HEAD_DIM, q, zero))
        for half, qh in enumerate(qs):
            m = None
            for si, k in enumerate(k_refs):
                s = _dot_nt(k[0, 0], qh)
                s_w[half, offs[si]:offs[si + 1], :] = s
                ms = jnp.max(s, axis=0, keepdims=True)
                m = ms if m is None else jnp.maximum(m, ms)
            m_w[half] = m

        lam = (jnp.exp(jnp.sum(lq1[...] * lk1[...], axis=1, keepdims=True))
               - jnp.exp(jnp.sum(lq2[...] * lk2[...], axis=1, keepdims=True)) + LAM_INIT0)
        probs, norms = [], []
        for half in range(2):
            p = [jnp.exp2(s_r[half, offs[si]:offs[si + 1], :] - m_r[half]) for si in range(nseg)]
            norms.append(functools.reduce(jnp.add, [jnp.sum(x, axis=0, keepdims=True) for x in p]))
            probs.append(p)
        c2 = lam * norms[0] / norms[1]
        ot = None
        for si in range(nseg):
            a = (probs[0][si] - probs[1][si] * c2).astype(BF16)
            part = _dot(vt_refs[si][0, 0], a)
            ot = part if ot is None else ot + part
        ot = ot * (1.0 / norms[0])
        msq = jnp.mean(ot * ot, axis=0, keepdims=True)
        on = ot * lax.rsqrt(msq + NORM_EPS) * (g_ref[...] * (1.0 - LAM_INIT0))
        o_ref[0, 0] = on.T.astype(o_ref.dtype)

    @pl.when(lax.rem(i, 2) == 0)
    def _():
        body(s_even, m_even, s_odd, m_odd)

    @pl.when(lax.rem(i, 2) == 1)
    def _():
        body(s_odd, m_odd, s_even, m_even)


def _diff_attn(q, ks, vts, lams, subln_g):
    b, nh, t, _ = q.shape
    tq = min(ROW_TILE, t)
    nq = t // tq
    nseg = len(ks)
    tk = sum(k.shape[2] for k in ks)
    ntiles = nh * nq

    def tile(s, lag):
        ts = jnp.clip(s - lag, 0, ntiles - 1)
        return ts // nq, ts % nq

    in_specs = [pl.BlockSpec((1, 1, tq, LANES), lambda bi, s: (bi, *tile(s, 0), 0))]
    for k in ks:
        in_specs.append(pl.BlockSpec((1, 1, k.shape[2], LANES), lambda bi, s: (bi, tile(s, 0)[0], 0, 0)))
    for vt in vts:
        in_specs.append(pl.BlockSpec((1, 1, VT_ROWS, vt.shape[3]), lambda bi, s: (bi, tile(s, 1)[0], 0, 0)))
    in_specs += [pl.BlockSpec((1, DIFF_HEAD_DIM), lambda bi, s: (0, 0))] * 4
    in_specs.append(pl.BlockSpec((DIFF_V_DIM, 1), lambda bi, s: (0, 0)))
    return pl.pallas_call(
        functools.partial(_diff_attn_kernel, nseg=nseg),
        grid=(b, ntiles + 1),
        in_specs=in_specs,
        out_specs=pl.BlockSpec((1, 1, tq, LANES), lambda bi, s: (bi, *tile(s, 1), 0)),
        out_shape=jax.ShapeDtypeStruct((b, nh, t, LANES), BF16),
        scratch_shapes=[pltpu.VMEM((2, tk, tq), F32), pltpu.VMEM((2, 1, tq), F32)] * 2,
        compiler_params=_cparams(("parallel", "arbitrary")),
        name="diff_attn",
    )(q, *ks, *vts, *[l.reshape(1, DIFF_HEAD_DIM) for l in lams], subln_g.reshape(DIFF_V_DIM, 1))


def _swa_kernel(*refs, t, tq, nq, nwin, kvps):
    sink_ref, q_ref = refs[0], refs[1]
    kw_refs = refs[2:2 + nwin]
    kc_ref = refs[2 + nwin]
    vw_refs = refs[3 + nwin:3 + 2 * nwin]
    vc_ref, o_ref, s_even, m_even, s_odd, m_odd = refs[3 + 2 * nwin:]
    n = pl.program_id(1)
    i = pl.program_id(2)
    c = kc_ref.shape[2]
    span = nwin * LANES
    group = SWA_HEADS // SWA_KV_HEADS

    @pl.when(i == 0)
    def _():
        s_odd[...] = jnp.zeros(s_odd.shape, F32)
        m_odd[...] = jnp.zeros(m_odd.shape, F32)

    def body(s_w, m_w, s_r, m_r):
        ti = jnp.minimum(i, nq - 1)
        kpos = (ti * tq - SWA_WINDOW) + lax.broadcasted_iota(I32, (span, tq), 0)
        qpos = ti * tq + lax.broadcasted_iota(I32, (span, tq), 1)
        ok = jnp.logical_and(jnp.logical_and(kpos >= 0, kpos < t), jnp.abs(qpos - kpos) <= SWA_WINDOW)
        bias = jnp.where(ok, 0.0, NEG_INF)
        lane = lax.broadcasted_iota(I32, (tq, LANES), 1)
        for kk in range(kvps):
            kwin = jnp.concatenate([r[0, kk] for r in kw_refs], axis=0)
            kc = kc_ref[0, kk]
            for j in range(group // 2):
                qc = q_ref[0, kk * (group // 2) + j]
                zero = jnp.zeros_like(qc)
                for half in range(2):
                    hh = kk * group + 2 * j + half
                    keep = (lane < SWA_HEAD_DIM) if half == 0 else (lane >= SWA_HEAD_DIM)
                    qz = jnp.where(keep, qc, zero)
                    s_c = _dot_nt(kc, qz)
                    s_l = _dot_nt(kwin, qz) + bias
                    sink = sink_ref[n * kvps * group + hh] * LOG2E
                    s_w[hh, 0:c, :] = s_c
                    s_w[hh, c:c + span, :] = s_l
                    m_w[hh] = jnp.maximum(jnp.maximum(jnp.max(s_c, axis=0, keepdims=True),
                                                      jnp.max(s_l, axis=0, keepdims=True)), sink)

        for kk in range(kvps):
            vwin = jnp.concatenate([r[0, kk] for r in vw_refs], axis=1)
            vc = vc_ref[0, kk]
            vwin = jnp.where(lax.broadcasted_iota(I32, vwin.shape, 0) == SWA_HEAD_DIM, jnp.ones_like(vwin), vwin)
            vc = jnp.where(lax.broadcasted_iota(I32, vc.shape, 0) == SWA_HEAD_DIM, jnp.ones_like(vc), vc)
            for j in range(group // 2):
                res = []
                for half in range(2):
                    hh = kk * group + 2 * j + half
                    m = m_r[hh]
                    p_c = jnp.exp2(s_r[hh, 0:c, :] - m).astype(BF16)
                    p_l = jnp.exp2(s_r[hh, c:c + span, :] - m).astype(BF16)
                    ot = _dot(vc, p_c) + _dot(vwin, p_l)
                    sink = sink_ref[n * kvps * group + hh] * LOG2E
                    den = ot[SWA_HEAD_DIM:SWA_HEAD_DIM + 1, :] + jnp.exp2(sink - m)
                    res.append(ot[0:SWA_HEAD_DIM, :] * (1.0 / den))
                o_ref[0, kk * (group // 2) + j] = jnp.concatenate(res, axis=0).T.astype(o_ref.dtype)

    @pl.when(lax.rem(i, 2) == 0)
    def _():
        body(s_even, m_even, s_odd, m_odd)

    @pl.when(lax.rem(i, 2) == 1)
    def _():
        body(s_odd, m_odd, s_even, m_even)


def _swa_attn(sink, q, kl, vtl, kc, vtc):
    b, nchunk, t, _ = q.shape
    nkv = kl.shape[1]
    cpk = nchunk // nkv
    c = kc.shape[2]
    tq = ROW_TILE
    nq = t // tq
    nblk = t // LANES
    bpt = tq // LANES
    nwin = bpt + 2 * (SWA_WINDOW // LANES)
    kvps = SWA_KV_PER_STEP

    def kblk(i, j, lag):
        ti = jnp.clip(i - lag, 0, nq - 1)
        return jnp.clip(ti * bpt - SWA_WINDOW // LANES + j, 0, nblk - 1)

    in_specs = [pl.BlockSpec(memory_space=pltpu.SMEM),
                pl.BlockSpec((1, kvps * cpk, tq, LANES), lambda bi, n, i: (bi, n, jnp.minimum(i, nq - 1), 0))]
    for j in range(nwin):
        in_specs.append(pl.BlockSpec((1, kvps, LANES, LANES), lambda bi, n, i, j=j: (bi, n, kblk(i, j, 0), 0)))
    in_specs.append(pl.BlockSpec((1, kvps, c, LANES), lambda bi, n, i: (bi, n, 0, 0)))
    for j in range(nwin):
        in_specs.append(pl.BlockSpec((1, kvps, LANES, LANES), lambda bi, n, i, j=j: (bi, n, 0, kblk(i, j, 1))))
    in_specs.append(pl.BlockSpec((1, kvps, LANES, c), lambda bi, n, i: (bi, n, 0, 0)))
    heads = kvps * SWA_HEADS // nkv
    scratch = [pltpu.VMEM((heads, c + nwin * LANES, tq), F32), pltpu.VMEM((heads, 1, tq), F32)] * 2
    return pl.pallas_call(
        functools.partial(_swa_kernel, t=t, tq=tq, nq=nq, nwin=nwin, kvps=kvps),
        grid=(b, nkv // kvps, nq + 1),
        in_specs=in_specs,
        out_specs=pl.BlockSpec((1, kvps * cpk, tq, LANES), lambda bi, n, i: (bi, n, jnp.maximum(i - 1, 0), 0)),
        out_shape=jax.ShapeDtypeStruct((b, nchunk, t, LANES), BF16),
        scratch_shapes=scratch,
        compiler_params=_cparams(("parallel", "arbitrary", "arbitrary")),
        name="swa_attn",
    )(sink, q, *([kl] * nwin), kc, *([vtl] * nwin), vtc)


def _outproj_kernel(*refs, conv, tm):
    if conv:
        (x_ref, mod_ref, bg_ref, p_ref, pprev_ref, pnext_ref, cw_ref, o_ref, w_ref,
         g2_ref, rwh_ref, rwl_ref, xn_ref, haug_ref, afft_ref) = refs
        i = pl.program_id(1)
        ni = pl.num_programs(1)
        p = p_ref[0].astype(F32)
        row = lax.broadcasted_iota(I32, p.shape, 0)
        halo_prev = jnp.where(i > 0, pprev_ref[0, BF16_ROWS - 1:BF16_ROWS, :].astype(F32), 0.0)
        halo_next = jnp.where(i < ni - 1, pnext_ref[0, 0:1, :].astype(F32), 0.0)
        p_prev = jnp.where(row == 0, halo_prev, pltpu.roll(p, 1, 0))
        p_next = jnp.where(row == tm - 1, halo_next, pltpu.roll(p, tm - 1, 0))
        cw = cw_ref[...]
        cv = p_prev * cw[0:1, :] + p * cw[1:2, :] + p_next * cw[2:3, :]
        u = (bg_ref[0].astype(F32) * cv).astype(BF16)
        lhs = jnp.concatenate([u] + [o_ref[0, j] for j in range(DIFF_HEADS)], axis=1)
    else:
        x_ref, mod_ref, o_ref, w_ref, g2_ref, rwh_ref, rwl_ref, xn_ref, haug_ref, afft_ref = refs
        lhs = jnp.concatenate([o_ref[0, j] for j in range(o_ref.shape[1])], axis=1)
    m = mod_ref[0]
    xn = x_ref[0] + m[2:3, :] * _dot(lhs, w_ref[...])
    xn_ref[0] = xn
    h = _modulate(xn, g2_ref[...], m[3:4, :], m[4:5, :])
    d = h.shape[1]
    haug_ref[0, :, 0:d] = h
    h_hi = h.astype(BF16)
    h_lo = (h - h_hi.astype(F32)).astype(BF16)
    logits = _dot(h_hi, rwh_ref[...]) + _dot(h_hi, rwl_ref[...]) + _dot(h_lo, rwh_ref[...])
    lane = lax.broadcasted_iota(I32, logits.shape, 1)
    logits = jnp.where(lane < N_EXPERTS, logits, NEG_INF)
    e = jnp.exp(logits - jnp.max(logits, axis=1, keepdims=True))
    aff = e / jnp.sum(e, axis=1, keepdims=True)
    haug_ref[0, :, d:d + AUG] = aff
    afft_ref[0] = aff.T[0:N_EXPERTS, :]


def _outproj(x, mods, mod_row, w_out, n2g, rw_hi, rw_lo, o, conv_args=None):
    b, t, d = x.shape
    tm = min(ROW_TILE, t)
    conv = conv_args is not None
    xmap = lambda bi, i: (bi, i, 0)
    in_specs = [pl.BlockSpec((1, tm, d), xmap),
                pl.BlockSpec((1, N_MOD, d), lambda bi, i: (mod_row(bi), 0, 0))]
    args = [x, mods]
    if conv:
        bg, p, cw = conv_args
        hb = tm // BF16_ROWS
        nhb = t // BF16_ROWS
        in_specs += [pl.BlockSpec((1, tm, CONV_DIM), xmap),
                     pl.BlockSpec((1, tm, CONV_DIM), xmap),
                     pl.BlockSpec((1, BF16_ROWS, CONV_DIM), lambda bi, i: (bi, jnp.maximum(i * hb - 1, 0), 0)),
                     pl.BlockSpec((1, BF16_ROWS, CONV_DIM), lambda bi, i: (bi, jnp.minimum((i + 1) * hb, nhb - 1), 0)),
                     pl.BlockSpec((3, CONV_DIM), lambda bi, i: (0, 0)),
                     pl.BlockSpec((1, DIFF_HEADS, tm, LANES), lambda bi, i: (bi, 0, i, 0))]
        args += [bg, p, p, p, cw, o]
    else:
        in_specs.append(pl.BlockSpec((1, o.shape[1], tm, LANES), lambda bi, i: (bi, 0, i, 0)))
        args.append(o)
    in_specs += [pl.BlockSpec(w_out.shape, lambda bi, i: (0, 0)),
                 pl.BlockSpec((1, d), lambda bi, i: (0, 0)),
                 pl.BlockSpec((d, LANES), lambda bi, i: (0, 0)),
                 pl.BlockSpec((d, LANES), lambda bi, i: (0, 0))]
    args += [w_out, n2g.reshape(1, d), rw_hi, rw_lo]
    return pl.pallas_call(
        functools.partial(_outproj_kernel, conv=conv, tm=tm),
        grid=(b, t // tm),
        in_specs=in_specs,
        out_specs=[pl.BlockSpec((1, tm, d), xmap),
                   pl.BlockSpec((1, tm, d + AUG), xmap),
                   pl.BlockSpec((1, N_EXPERTS, tm), lambda bi, i: (bi, 0, i))],
        out_shape=[jax.ShapeDtypeStruct((b, t, d), F32),
                   jax.ShapeDtypeStruct((b, t, d + AUG), F32),
                   jax.ShapeDtypeStruct((b, N_EXPERTS, t), F32)],
        compiler_params=_cparams(("parallel", "arbitrary")),
        name="outproj",
    )(*args)


def _route_kernel(aff_ref, idx_ref, post_ref, offs_ref, pinc_ref, *, t, cap, capp, lc):
    nch = t // lc
    ne = N_EXPERTS
    aff = aff_ref[0]

    def search(i, lo):
        cand = lo | lax.shift_left(jnp.int32(1), 30 - i)
        cnt = jnp.sum(jnp.where(aff >= pltpu.bitcast(cand, F32), 1.0, 0.0), axis=1, keepdims=True)
        return jnp.where(cnt >= cap, cand, lo)

    thr_bits = lax.fori_loop(0, 31, search, jnp.zeros((ne, 1), I32))
    thr = pltpu.bitcast(thr_bits, F32)
    gt = aff > thr
    eq = aff == thr
    need = cap - jnp.sum(jnp.where(gt, 1.0, 0.0), axis=1, keepdims=True)

    r_i = lax.broadcasted_iota(I32, (lc, lc), 0)
    c_i = lax.broadcasted_iota(I32, (lc, lc), 1)
    upper = jnp.where(r_i < c_i, 1.0, 0.0).astype(BF16)

    def excl_prefix(x):
        outs, offs = [], []
        carry = jnp.zeros((ne, 1), F32)
        for c in range(nch):
            xc = x[:, c * lc:(c + 1) * lc]
            offs.append(carry)
            outs.append(_dot(xc.astype(BF16), upper) + carry)
            carry = carry + jnp.sum(xc, axis=1, keepdims=True)
        offs.append(carry)
        return jnp.concatenate(outs, axis=1), offs

    eqf = jnp.where(eq, 1.0, 0.0)
    eq_rank, _ = excl_prefix(eqf)
    sel = jnp.logical_or(gt, jnp.logical_and(eq, eq_rank < need))
    self_ = jnp.where(sel, 1.0, 0.0)
    pos, offs = excl_prefix(self_)

    lane = lax.broadcasted_iota(I32, (ne, LANES), 1)
    om = jnp.zeros((ne, LANES), F32)
    for c, o in enumerate(offs):
        om = jnp.where(lane == c, o, om)
    offs_ref[0] = om

    posm = jnp.where(sel, pos, -1.0)
    pad = jnp.full((LANES - ne, lc), -1.0, F32)
    for c in range(nch):
        blk = jnp.concatenate([posm[:, c * lc:(c + 1) * lc], pad], axis=0)
        post_ref[0, c * lc:(c + 1) * lc, :] = blk.T
    pinc = pos + self_
    nt = t // LANES
    for k in range(nt):
        pinc_ref[k] = pinc[:, k * LANES:(k + 1) * LANES]

    rows = BF16_ROWS
    sub = lax.broadcasted_iota(I32, (rows, LANES), 0).astype(F32)
    one = jnp.ones((rows, LANES), BF16)
    zero = jnp.zeros((rows, LANES), BF16)
    lane_c = lax.broadcasted_iota(I32, (capp, LANES), 1)
    unroll = min(8, nt)
    assert nt <= 256

    def per_expert(e, acc_m):
        cols = []
        for jb in range(capp // LANES):
            def tiles(kk, acc, jb=jb):
                for u in range(unroll):
                    rel = pinc_ref[kk * unroll + u, pl.ds(e, 1), :] - float(jb * LANES)
                    q = (jnp.clip(rel, -float(rows), float(LANES + rows)) - sub).astype(BF16)
                    acc = acc + jnp.concatenate(
                        [jnp.where(q <= float(rows * i), one, zero) for i in range(LANES // rows)], axis=0)
                return acc
            acc = lax.fori_loop(0, nt // unroll, tiles, jnp.zeros((LANES, LANES), BF16))
            cols.append(jnp.sum(acc.astype(F32), axis=1, keepdims=True))
        col = jnp.concatenate(cols, axis=0)
        return jnp.where(lane_c == e, col, acc_m)

    idx_m = lax.fori_loop(0, ne, per_expert, jnp.zeros((capp, LANES), F32))
    base = pl.program_id(0) * t
    idx_ref[0] = idx_m.T[0:ne, :].astype(I32) + base


def _route(aff_t, cap):
    b, ne, t = aff_t.shape
    lc = min(256, t)
    capp = max(cap, LANES)
    nch = t // lc
    return pl.pallas_call(
        functools.partial(_route_kernel, t=t, cap=cap, capp=capp, lc=lc),
        grid=(b,),
        in_specs=[pl.BlockSpec((1, ne, t), lambda bi: (bi, 0, 0))],
        out_specs=[pl.BlockSpec((1, ne, capp), lambda bi: (bi, 0, 0)),
                   pl.BlockSpec((1, t, LANES), lambda bi: (bi, 0, 0)),
                   pl.BlockSpec((1, ne, LANES), lambda bi: (bi, 0, 0))],
        out_shape=[jax.ShapeDtypeStruct((b, ne, capp), I32),
                   jax.ShapeDtypeStruct((b, t, LANES), F32),
                   jax.ShapeDtypeStruct((b, ne, LANES), F32)],
        scratch_shapes=[pltpu.VMEM((t // LANES, ne, LANES), F32)],
        compiler_params=_cparams(("arbitrary",)),
        name="route",
    )(aff_t)


def _ffn_kernel(idx_ref, haug_ref, wg_ref, wu_ref, wd_ref, y_ref, hbuf, w_in, w_dn, gsem, *, rows, d):
    e = pl.program_id(0)
    step = e * pl.num_programs(1) + pl.program_id(1)
    nsteps = pl.num_programs(0) * pl.num_programs(1)

    @pl.when(pl.program_id(1) == 0)
    def _():
        w_in[0] = wg_ref[0].astype(BF16)
        w_in[1] = wu_ref[0].astype(BF16)
        w_dn[...] = wd_ref[0].astype(BF16)

    def row_copy(src_row, dst_slot, dst_row):
        return pltpu.make_async_copy(haug_ref.at[pl.ds(src_row, 1)],
                                     hbuf.at[dst_slot, pl.ds(dst_row, 1)], gsem.at[dst_slot])

    def wait_rows(sl):
        pltpu.make_async_copy(haug_ref.at[pl.ds(0, rows)], hbuf.at[sl], gsem.at[sl]).wait()

    @pl.when(step == 0)
    def _():
        def body(r, carry):
            row_copy(idx_ref[r], 0, r).start()
            return carry
        lax.fori_loop(0, rows, body, 0)

    def run(slot):
        base = jnp.minimum(step + 1, nsteps - 1) * rows
        for r in range(rows):
            row_copy(idx_ref[base + r], 1 - slot, r).start()

        wait_rows(slot)
        hrow = hbuf[slot]
        hs = hrow[:, 0:d].astype(BF16)
        lane = lax.broadcasted_iota(I32, (rows, AUG), 1)
        gate = jnp.sum(jnp.where(lane == e, hrow[:, d:d + AUG], 0.0), axis=1, keepdims=True)
        a = _dot(hs, w_in[0])
        u = _dot(hs, w_in[1])
        hm = (a * jax.nn.sigmoid(a) * u).astype(BF16)
        y_ref[0] = (_dot(hm, w_dn[...]) * gate).astype(y_ref.dtype)

        @pl.when(step == nsteps - 1)
        def _():
            wait_rows(1 - slot)

    for parity in range(2):
        pl.when(lax.rem(step, 2) == parity)(functools.partial(run, parity))


def _ffn(idx_flat, haug, wg, wu, wd, rows):
    ne, d, f = wg.shape
    n_rows_total = idx_flat.shape[0]
    steps = n_rows_total // (ne * rows)
    grid_spec = pltpu.PrefetchScalarGridSpec(
        num_scalar_prefetch=1,
        grid=(ne, steps),
        in_specs=[pl.BlockSpec(memory_space=pl.ANY),
                  pl.BlockSpec((1, d, f), lambda e, s, idx: (e, 0, 0)),
                  pl.BlockSpec((1, d, f), lambda e, s, idx: (e, 0, 0)),
                  pl.BlockSpec((1, f, d), lambda e, s, idx: (e, 0, 0))],
        out_specs=pl.BlockSpec((1, rows, d), lambda e, s, idx: (e, s, 0)),
        scratch_shapes=[pltpu.VMEM((2, rows, d + AUG), F32),
                        pltpu.VMEM((2, d, f), BF16),
                        pltpu.VMEM((f, d), BF16),
                        pltpu.SemaphoreType.DMA((2,))],
    )
    return pl.pallas_call(
        functools.partial(_ffn_kernel, rows=rows, d=d),
        grid_spec=grid_spec,
        out_shape=jax.ShapeDtypeStruct((ne, steps * rows, d), BF16),
        compiler_params=_cparams(("arbitrary", "arbitrary")),
        name="ffn",
    )(idx_flat, haug, wg, wu, wd)


def _combine_kernel(ws_ref, nr_ref, x_ref, mod_ref, post_ref, fg_ref, y_ref, o_ref, stage, acc, sem,
                    *, cap, tm, final_norm):
    nchunk = pl.num_programs(1)
    step = pl.program_id(0) * nchunk + pl.program_id(1)
    nsteps = pl.num_programs(0) * nchunk
    slot = lax.rem(step, 2)
    ne = N_EXPERTS
    w = COMB_W
    per = LANES // w
    total_rows = y_ref.shape[1]
    lane_row = lax.broadcasted_iota(I32, (1, LANES), 1)

    def window(st, e, r):
        sample = lax.div(st, nchunk)
        nominal = ws_ref[st * ne + e] + r * w
        grow = pl.multiple_of(jnp.minimum(sample * cap + nominal, total_rows - w), BF16_ROWS)
        return nominal, grow, grow - sample * cap

    def copy(e, grow, sl, si):
        return pltpu.make_async_copy(y_ref.at[e, pl.ds(grow, w)], stage.at[sl, pl.ds(e * w, w)], sem.at[si])

    def fetch(st, r, sl, si):
        for e in range(ne):
            copy(e, window(st, e, r)[1], sl, si).start()

    def wait(sl, si):
        for e in range(ne):
            copy(e, 0, sl, si).wait()

    def gathered(st, r, sl):
        pt = post_ref[0]
        blocks = []
        for g in range(ne // per):
            tgt = jnp.zeros((1, LANES), F32)
            mine = jnp.zeros((tm, LANES), F32)
            for k in range(per):
                e = g * per + k
                nominal, _, first = window(st, e, r)
                inb = jnp.logical_and(lane_row >= k * w, lane_row < (k + 1) * w)
                slot = first - k * w + lane_row
                tgt = jnp.where(inb, jnp.where(slot >= nominal, slot, -2).astype(F32), tgt)
                mine = jnp.where(inb, pt[:, e:e + 1], mine)
            blocks.append(jnp.where(mine == tgt, 1.0, 0.0).astype(BF16))
        onehot = jnp.concatenate(blocks, axis=1)
        return _dot(onehot, stage[sl])

    @pl.when(step == 0)
    def _():
        fetch(0, 0, 0, 0)

    @pl.when(step + 1 < nsteps)
    def _():
        fetch(step + 1, 0, 1 - slot, 1 - slot)

    wait(slot, slot)
    acc[...] = gathered(step, 0, slot)

    def extra_round(r, carry):
        fetch(step, r, slot, 2)
        wait(slot, 2)
        acc[...] += gathered(step, r, slot)
        return carry

    lax.fori_loop(1, nr_ref[step], extra_round, 0)
    m = mod_ref[0]
    xn = x_ref[0] + m[5:6, :] * acc[...]
    if final_norm:
        ms = jnp.mean(xn * xn, axis=-1, keepdims=True)
        xn = xn * lax.rsqrt(ms + NORM_EPS) * fg_ref[...]
    o_ref[0] = xn


def _combine(ws_flat, nr_flat, x, mods, mod_row, post, y, cap, final_g=None):
    b, t, d = x.shape
    tm = min(ROW_TILE, t)
    ne = N_EXPERTS
    final_norm = final_g is not None
    fg = (final_g if final_norm else jnp.ones((d,), F32)).reshape(1, d)
    grid_spec = pltpu.PrefetchScalarGridSpec(
        num_scalar_prefetch=2,
        grid=(b, t // tm),
        in_specs=[pl.BlockSpec((1, tm, d), lambda bi, i, a, c: (bi, i, 0)),
                  pl.BlockSpec((1, N_MOD, d), lambda bi, i, a, c: (mod_row(bi), 0, 0)),
                  pl.BlockSpec((1, tm, LANES), lambda bi, i, a, c: (bi, i, 0)),
                  pl.BlockSpec((1, d), lambda bi, i, a, c: (0, 0)),
                  pl.BlockSpec(memory_space=pl.ANY)],
        out_specs=pl.BlockSpec((1, tm, d), lambda bi, i, a, c: (bi, i, 0)),
        scratch_shapes=[pltpu.VMEM((2, ne * COMB_W, d), BF16),
                        pltpu.VMEM((tm, d), F32),
                        pltpu.SemaphoreType.DMA((3,))],
    )
    return pl.pallas_call(
        functools.partial(_combine_kernel, cap=cap, tm=tm, final_norm=final_norm),
        grid_spec=grid_spec,
        out_shape=jax.ShapeDtypeStruct((b, t, d), F32),
        compiler_params=_cparams(("arbitrary", "arbitrary")),
        name="combine",
    )(ws_flat, nr_flat, x, mods, post, fg, y)


def _moe(xn, haug, aff_t, mods, mod_row, wg, wu, wd, rows_per_step, final_g=None):
    b, t, d = xn.shape
    ne = N_EXPERTS
    cap = CAPACITY_FACTOR * t // ne
    idx, post, offs = _route(aff_t, cap)
    idx_flat = jnp.transpose(idx[:, :, :cap], (1, 0, 2)).reshape(-1)
    y = _ffn(idx_flat, haug.reshape(b * t, d + AUG), wg, wu, wd, rows_per_step)
    tm = min(ROW_TILE, t)
    nchunk = t // tm
    lc = min(256, t)
    per = tm // lc
    offs_i = offs.astype(I32)[:, :, 0:t // lc + 1:per]
    start = jnp.transpose(offs_i[:, :, :nchunk], (0, 2, 1))
    end = jnp.transpose(offs_i[:, :, 1:], (0, 2, 1))
    ws = (start // BF16_ROWS) * BF16_ROWS
    nr = jnp.maximum(jnp.max((end - ws + COMB_W - 1) // COMB_W, axis=2), 1)
    return _combine(ws.reshape(-1), nr.reshape(-1), xn, mods, mod_row, post, y, cap, final_g)


def _split_hi_lo(w):
    hi = w.astype(BF16)
    lo = (w - hi.astype(F32)).astype(BF16)
    return hi, lo


def _router_weights(rw):
    d, ne = rw.shape
    pad = jnp.zeros((d, LANES - ne), F32)
    return _split_hi_lo(jnp.concatenate([rw, pad], axis=1))


def kernel(x, c, ctx, c_ctx, l0_ada_w, l0_ada_b, l0_norm1_g, l0_w_in, l0_conv_w, l0_lambda_q1, l0_lambda_k1, l0_lambda_q2, l0_lambda_k2, l0_subln_g, l0_w_out, l0_norm2_g, l0_router_w, l0_exp_w_gate, l0_exp_w_up, l0_exp_w_down, l1_ada_w, l1_ada_b, l1_norm1_g, l1_w_qkv, l1_sink, l1_w_out, l1_norm2_g, l1_router_w, l1_exp_w_gate, l1_exp_w_up, l1_exp_w_down, final_norm_g):
    b, t, d = x.shape
    nctx = ctx.shape[1]
    assert b + 1 <= MOD_ROWS
    lat_row = lambda bi: bi
    ctx_row = lambda bi: b

    cc = jnp.zeros((MOD_ROWS, d), F32).at[:b].set(c).at[b].set(c_ctx)
    mods0 = _ada(cc, l0_ada_w, l0_ada_b).reshape(MOD_ROWS, N_MOD, d)
    mods1 = _ada(cc, l1_ada_w, l1_ada_b).reshape(MOD_ROWS, N_MOD, d)
    tables = _rope_tables(t)
    qscale = LOG2E * DIFF_HEAD_DIM ** -0.5

    w_in = l0_w_in.astype(BF16)
    cd = CONV_DIM
    wvt = jnp.transpose(l0_w_in[:, 3 * cd + 1024:]).astype(BF16)
    outs0 = [("plain", 0, cd, False, 1.0),
             ("mul", cd, cd, False, 1.0),
             ("heads", 3 * cd, 512, True, qscale),
             ("heads", 3 * cd + 512, 512, True, 1.0)]
    w_main = w_in[:, :3 * cd + 1024]
    bg_l, cx_l, q_l, k_l, vt_l = _proj(x, mods0, lat_row, l0_norm1_g, w_main, outs0, tables, wvt)
    bg_c, cx_c, q_c, k_c, vt_c = _proj(ctx, mods0, ctx_row, l0_norm1_g, w_main, outs0, None, wvt)
    lams = (l0_lambda_q1, l0_lambda_k1, l0_lambda_q2, l0_lambda_k2)
    o_l = _diff_attn(q_l, [k_c, k_l], [vt_c, vt_l], lams, l0_subln_g)
    o_c = _diff_attn(q_c, [k_c], [vt_c], lams, l0_subln_g)

    w_out0 = l0_w_out.astype(BF16)
    rw0 = _router_weights(l0_router_w)
    wg0, wu0, wd0 = l0_exp_w_gate, l0_exp_w_up, l0_exp_w_down
    x1, haug, aff_t = _outproj(x, mods0, lat_row, w_out0, l0_norm2_g, *rw0, o_l,
                               conv_args=(bg_l, cx_l, l0_conv_w))
    x = _moe(x1, haug, aff_t, mods0, lat_row, wg0, wu0, wd0, CAPACITY_FACTOR * t // N_EXPERTS)
    c1, haug_c, aff_tc = _outproj(ctx, mods0, ctx_row, w_out0, l0_norm2_g, *rw0, o_c,
                                  conv_args=(bg_c, cx_c, l0_conv_w))
    ctx = _moe(c1, haug_c, aff_tc, mods0, ctx_row, wg0, wu0, wd0, b * (CAPACITY_FACTOR * nctx // N_EXPERTS))

    qw = SWA_HEADS * SWA_HEAD_DIM
    kvw = SWA_KV_HEADS * SWA_HEAD_DIM
    wq = l1_w_qkv[:, :qw]
    wk = l1_w_qkv[:, qw:qw + kvw].reshape(d, SWA_KV_HEADS, SWA_HEAD_DIM)
    wv = l1_w_qkv[:, qw + kvw:].reshape(d, SWA_KV_HEADS, SWA_HEAD_DIM)
    wk2 = jnp.concatenate([wk, wk], axis=2).reshape(d, 2 * kvw)
    wv2 = jnp.concatenate([wv, jnp.zeros_like(wv)], axis=2).reshape(d, 2 * kvw)
    w_qk = jnp.concatenate([wq, wk2], axis=1).astype(BF16)
    wvt1 = jnp.transpose(wv2).astype(BF16)
    outs1 = [("heads", 0, qw, True, LOG2E * SWA_HEAD_DIM ** -0.5),
             ("heads", qw, 2 * kvw, True, 1.0)]
    q1, k1, vt1 = _proj(x, mods1, lat_row, l1_norm1_g, w_qk, outs1, tables, wvt1)
    k1c, vt1c = _proj(ctx, mods1, ctx_row, l1_norm1_g, w_qk[:, qw:], [("heads", 0, 2 * kvw, False, 1.0)], None, wvt1)
    o1 = _swa_attn(l1_sink, q1, k1, vt1, k1c, vt1c)

    rw1 = _router_weights(l1_router_w)
    wg1, wu1, wd1 = l1_exp_w_gate, l1_exp_w_up, l1_exp_w_down
    x1, haug, aff_t = _outproj(x, mods1, lat_row, l1_w_out.astype(BF16), l1_norm2_g, *rw1, o1)
    return _moe(x1, haug, aff_t, mods1, lat_row, wg1, wu1, wd1, CAPACITY_FACTOR * t // N_EXPERTS,
                final_g=final_norm_g)
```

```python
import functools
import math

import jax
import jax.numpy as jnp
from jax import lax
from jax.experimental import pallas as pl
from jax.experimental.pallas import tpu as pltpu

F32 = jnp.float32
BF16 = jnp.bfloat16
I32 = jnp.int32

GRID_W = 64
ROPE_BASE = 10000.0
NORM_EPS = 1e-6
NEG_INF = -1e30
N_MOD = 6

CONV_DIM = 512
DIFF_HEADS = 4
DIFF_HEAD_DIM = 64
DIFF_V_DIM = 128
LAM_INIT0 = 0.8 - 0.6 * math.exp(-0.3 * 0)

SWA_HEADS = 16
SWA_KV_HEADS = 4
SWA_HEAD_DIM = 64
SWA_WINDOW = 128

N_EXPERTS = 16
CAPACITY_FACTOR = 2

LANES = 128
SUBLANES = 8
MOD_ROWS = 16
AUG = LANES
ROW_TILE = 256
PROJ_TILE = 512
ADA_TILE = 512
ROUTE_CHUNK = 256
COMB_W = 64
BF16_ROWS = 16
VT_ROWS = DIFF_V_DIM
SWA_KV_PER_STEP = 4
LOG2E = math.log2(math.e)
VMEM_LIMIT = 56 * 1024 * 1024


def _cparams(sem):
    return pltpu.CompilerParams(dimension_semantics=sem, vmem_limit_bytes=VMEM_LIMIT)


def _dot(a, b):
    return jnp.dot(a, b, preferred_element_type=F32)


def _dot_nt(a, b):
    return lax.dot_general(a, b, (((1,), (1,)), ((), ())), preferred_element_type=F32)


def _modulate(x, gain, shift, scale):
    ms = jnp.mean(x * x, axis=-1, keepdims=True)
    return x * lax.rsqrt(ms + NORM_EPS) * (gain * (1.0 + scale)) + shift


def _ada_kernel(c_ref, w_ref, b_ref, o_ref):
    c = c_ref[...]
    s = (c * jax.nn.sigmoid(c)).astype(BF16)
    o_ref[...] = _dot(s, w_ref[...].astype(BF16)) + b_ref[...]


def _ada(cc, w, b):
    d, n = w.shape
    tn = ADA_TILE
    return pl.pallas_call(
        _ada_kernel,
        grid=(n // tn,),
        in_specs=[pl.BlockSpec((MOD_ROWS, d), lambda j: (0, 0)),
                  pl.BlockSpec((d, tn), lambda j: (0, j)),
                  pl.BlockSpec((1, tn), lambda j: (0, j))],
        out_specs=pl.BlockSpec((MOD_ROWS, tn), lambda j: (0, j)),
        out_shape=jax.ShapeDtypeStruct((MOD_ROWS, n), F32),
        compiler_params=_cparams(("arbitrary",)),
        name="ada",
    )(cc, w, b.reshape(1, n))


def _rope_tables(t):
    n_freq = DIFF_HEAD_DIM // 4
    inv_freq = ROPE_BASE ** (-jnp.arange(n_freq, dtype=F32) / n_freq)
    pos = jnp.arange(t, dtype=I32)
    row = (pos // GRID_W).astype(F32)[:, None] * inv_freq
    col = (pos % GRID_W).astype(F32)[:, None] * inv_freq
    zeros = jnp.zeros_like(row)
    cos64 = jnp.concatenate([jnp.cos(row), jnp.cos(row), jnp.cos(col), jnp.cos(col)], axis=1)
    sa64 = jnp.concatenate([-jnp.sin(row), zeros, -jnp.sin(col), zeros], axis=1)
    sb64 = jnp.concatenate([zeros, jnp.sin(row), zeros, jnp.sin(col)], axis=1)
    rep = lambda a: jnp.concatenate([a, a], axis=1)
    return rep(cos64), rep(sa64), rep(sb64)


def _rope128(c, cos, sa, sb):
    return c * cos + pltpu.roll(c, LANES - 16, 1) * sa + pltpu.roll(c, 16, 1) * sb


def _proj_kernel(*refs, outs, rope, has_vt):
    x_ref, mod_ref, g_ref, w_ref = refs[:4]
    pos = 4
    if rope:
        cos_ref, sa_ref, sb_ref = refs[pos:pos + 3]
        pos += 3
    if has_vt:
        wvt_ref = refs[pos]
        pos += 1
    out_refs = refs[pos:]
    m = mod_ref[0]
    h = _modulate(x_ref[0], g_ref[...], m[0:1, :], m[1:2, :]).astype(BF16)
    y = _dot(h, w_ref[...])
    for (kind, c0, width, do_rope, scale), o_ref in zip(outs, out_refs):
        for j in range(width // LANES):
            c = y[:, c0 + j * LANES:c0 + (j + 1) * LANES]
            if kind == "mul":
                c = c * y[:, c0 + width + j * LANES:c0 + width + (j + 1) * LANES]
            if do_rope and rope:
                c = _rope128(c, cos_ref[...], sa_ref[...], sb_ref[...])
            if scale != 1.0:
                c = c * scale
            if kind == "heads":
                o_ref[0, j] = c.astype(o_ref.dtype)
            else:
                o_ref[0, :, j * LANES:(j + 1) * LANES] = c.astype(o_ref.dtype)
    if has_vt:
        vt = _dot_nt(wvt_ref[...], h)
        o_ref = out_refs[len(outs)]
        for j in range(vt.shape[0] // LANES):
            o_ref[0, j] = vt[j * LANES:(j + 1) * LANES, :].astype(o_ref.dtype)


def _proj(x, mods, mod_row, gain, w, outs, tables=None, wvt=None):
    b, t, d = x.shape
    tm = min(PROJ_TILE, t)
    n = w.shape[1]
    rope = tables is not None
    in_specs = [pl.BlockSpec((1, tm, d), lambda bi, i: (bi, i, 0)),
                pl.BlockSpec((1, N_MOD, d), lambda bi, i: (mod_row(bi), 0, 0)),
                pl.BlockSpec((1, d), lambda bi, i: (0, 0)),
                pl.BlockSpec((d, n), lambda bi, i: (0, 0))]
    args = [x, mods, gain.reshape(1, d), w]
    if rope:
        in_specs += [pl.BlockSpec((tm, LANES), lambda bi, i: (i, 0))] * 3
        args += list(tables)
    if wvt is not None:
        in_specs.append(pl.BlockSpec(wvt.shape, lambda bi, i: (0, 0)))
        args.append(wvt)
    out_specs, out_shapes = [], []
    for (kind, c0, width, do_rope, scale) in outs:
        if kind == "heads":
            nh = width // LANES
            out_specs.append(pl.BlockSpec((1, nh, tm, LANES), lambda bi, i: (bi, 0, i, 0)))
            out_shapes.append(jax.ShapeDtypeStruct((b, nh, t, LANES), BF16))
        else:
            out_specs.append(pl.BlockSpec((1, tm, width), lambda bi, i: (bi, i, 0)))
            out_shapes.append(jax.ShapeDtypeStruct((b, t, width), BF16))
    if wvt is not None:
        nh = wvt.shape[0] // LANES
        out_specs.append(pl.BlockSpec((1, nh, VT_ROWS, tm), lambda bi, i: (bi, 0, 0, i)))
        out_shapes.append(jax.ShapeDtypeStruct((b, nh, VT_ROWS, t), BF16))
    return pl.pallas_call(
        functools.partial(_proj_kernel, outs=tuple(outs), rope=rope, has_vt=wvt is not None),
        grid=(b, t // tm),
        in_specs=in_specs,
        out_specs=out_specs,
        out_shape=out_shapes,
        compiler_params=_cparams(("parallel", "arbitrary")),
        name="proj",
    )(*args)


def _diff_attn_kernel(*refs, nseg):
    q_ref = refs[0]
    k_refs = refs[1:1 + nseg]
    vt_refs = refs[1 + nseg:1 + 2 * nseg]
    lq1, lk1, lq2, lk2, g_ref, o_ref, s_even, m_even, s_odd, m_odd = refs[1 + 2 * nseg:]
    i = pl.program_id(1)
    offs = [0]
    for k in k_refs:
        offs.append(offs[-1] + k.shape[2])

    @pl.when(i == 0)
    def _():
        s_odd[...] = jnp.zeros(s_odd.shape, F32)
        m_odd[...] = jnp.zeros(m_odd.shape, F32)

    def body(s_w, m_w, s_r, m_r):
        q = q_ref[0, 0]
        lane = lax.broadcasted_iota(I32, q.shape, 1)
        zero = jnp.zeros_like(q)
        qs = (jnp.where(lane < DIFF_HEAD_DIM, q, zero), jnp.where(lane >= DIFF_HEAD_DIM, q, zero))
        for half, qh in enumerate(qs):
            m = None
            for si, k in enumerate(k_refs):
                s = _dot_nt(k[0, 0], qh)
                s_w[half, offs[si]:offs[si + 1], :] = s
                ms = jnp.max(s, axis=0, keepdims=True)
                m = ms if m is None else jnp.maximum(m, ms)
            m_w[half] = m

        lam = (jnp.exp(jnp.sum(lq1[...] * lk1[...], axis=1, keepdims=True))
               - jnp.exp(jnp.sum(lq2[...] * lk2[...], axis=1, keepdims=True)) + LAM_INIT0)
        probs, norms = [], []
        for half in range(2):
            p = [jnp.exp2(s_r[half, offs[si]:offs[si + 1], :] - m_r[half]) for si in range(nseg)]
            norms.append(functools.reduce(jnp.add, [jnp.sum(x, axis=0, keepdims=True) for x in p]))
            probs.append(p)
        c2 = lam * norms[0] / norms[1]
        ot = None
        for si in range(nseg):
            a = (probs[0][si] - probs[1][si] * c2).astype(BF16)
            part = _dot(vt_refs[si][0, 0], a)
            ot = part if ot is None else ot + part
        ot = ot * (1.0 / norms[0])
        msq = jnp.mean(ot * ot, axis=0, keepdims=True)
        on = ot * lax.rsqrt(msq + NORM_EPS) * (g_ref[...] * (1.0 - LAM_INIT0))
        o_ref[0, 0] = on.T.astype(o_ref.dtype)

    @pl.when(lax.rem(i, 2) == 0)
    def _():
        body(s_even, m_even, s_odd, m_odd)

    @pl.when(lax.rem(i, 2) == 1)
    def _():
        body(s_odd, m_odd, s_even, m_even)


def _diff_attn(q, ks, vts, lams, subln_g):
    b, nh, t, _ = q.shape
    tq = min(ROW_TILE, t)
    nq = t // tq
    nseg = len(ks)
    tk = sum(k.shape[2] for k in ks)
    ntiles = nh * nq

    def tile(s, lag):
        ts = jnp.clip(s - lag, 0, ntiles - 1)
        return ts // nq, ts % nq

    in_specs = [pl.BlockSpec((1, 1, tq, LANES), lambda bi, s: (bi, *tile(s, 0), 0))]
    for k in ks:
        in_specs.append(pl.BlockSpec((1, 1, k.shape[2], LANES), lambda bi, s: (bi, tile(s, 0)[0], 0, 0)))
    for vt in vts:
        in_specs.append(pl.BlockSpec((1, 1, VT_ROWS, vt.shape[3]), lambda bi, s: (bi, tile(s, 1)[0], 0, 0)))
    in_specs += [pl.BlockSpec((1, DIFF_HEAD_DIM), lambda bi, s: (0, 0))] * 4
    in_specs.append(pl.BlockSpec((DIFF_V_DIM, 1), lambda bi, s: (0, 0)))
    return pl.pallas_call(
        functools.partial(_diff_attn_kernel, nseg=nseg),
        grid=(b, ntiles + 1),
        in_specs=in_specs,
        out_specs=pl.BlockSpec((1, 1, tq, LANES), lambda bi, s: (bi, *tile(s, 1), 0)),
        out_shape=jax.ShapeDtypeStruct((b, nh, t, LANES), BF16),
        scratch_shapes=[pltpu.VMEM((2, tk, tq), F32), pltpu.VMEM((2, 1, tq), F32)] * 2,
        compiler_params=_cparams(("parallel", "arbitrary")),
        name="diff_attn",
    )(q, *ks, *vts, *[l.reshape(1, DIFF_HEAD_DIM) for l in lams], subln_g.reshape(DIFF_V_DIM, 1))


def _swa_kernel(*refs, t, tq, nq, nwin, kvps):
    sink_ref, q_ref = refs[0], refs[1]
    kw_refs = refs[2:2 + nwin]
    kc_ref = refs[2 + nwin]
    vw_refs = refs[3 + nwin:3 + 2 * nwin]
    vc_ref, o_ref, s_even, m_even, s_odd, m_odd = refs[3 + 2 * nwin:]
    n = pl.program_id(1)
    i = pl.program_id(2)
    c = kc_ref.shape[2]
    span = nwin * LANES
    group = SWA_HEADS // SWA_KV_HEADS

    @pl.when(i == 0)
    def _():
        s_odd[...] = jnp.zeros(s_odd.shape, F32)
        m_odd[...] = jnp.zeros(m_odd.shape, F32)

    def body(s_w, m_w, s_r, m_r):
        ti = jnp.minimum(i, nq - 1)
        kpos = (ti * tq - SWA_WINDOW) + lax.broadcasted_iota(I32, (span, tq), 0)
        qpos = ti * tq + lax.broadcasted_iota(I32, (span, tq), 1)
        ok = jnp.logical_and(jnp.logical_and(kpos >= 0, kpos < t), jnp.abs(qpos - kpos) <= SWA_WINDOW)
        bias = jnp.where(ok, 0.0, NEG_INF)
        lane = lax.broadcasted_iota(I32, (tq, LANES), 1)
        for kk in range(kvps):
            kwin = jnp.concatenate([r[0, kk] for r in kw_refs], axis=0)
            kc = kc_ref[0, kk]
            for j in range(group // 2):
                qc = q_ref[0, kk * (group // 2) + j]
                zero = jnp.zeros_like(qc)
                for half in range(2):
                    hh = kk * group + 2 * j + half
                    keep = (lane < SWA_HEAD_DIM) if half == 0 else (lane >= SWA_HEAD_DIM)
                    qz = jnp.where(keep, qc, zero)
                    s_c = _dot_nt(kc, qz)
                    s_l = _dot_nt(kwin, qz) + bias
                    sink = sink_ref[n * kvps * group + hh] * LOG2E
                    s_w[hh, 0:c, :] = s_c
                    s_w[hh, c:c + span, :] = s_l
                    m_w[hh] = jnp.maximum(jnp.maximum(jnp.max(s_c, axis=0, keepdims=True),
                                                      jnp.max(s_l, axis=0, keepdims=True)), sink)

        for kk in range(kvps):
            vwin = jnp.concatenate([r[0, kk] for r in vw_refs], axis=1)
            vc = vc_ref[0, kk]
            vwin = jnp.where(lax.broadcasted_iota(I32, vwin.shape, 0) == SWA_HEAD_DIM, jnp.ones_like(vwin), vwin)
            vc = jnp.where(lax.broadcasted_iota(I32, vc.shape, 0) == SWA_HEAD_DIM, jnp.ones_like(vc), vc)
            for j in range(group // 2):
                res = []
                for half in range(2):
                    hh = kk * group + 2 * j + half
                    m = m_r[hh]
                    p_c = jnp.exp2(s_r[hh, 0:c, :] - m).astype(BF16)
                    p_l = jnp.exp2(s_r[hh, c:c + span, :] - m).astype(BF16)
                    ot = _dot(vc, p_c) + _dot(vwin, p_l)
                    sink = sink_ref[n * kvps * group + hh] * LOG2E
                    den = ot[SWA_HEAD_DIM:SWA_HEAD_DIM + 1, :] + jnp.exp2(sink - m)
                    res.append(ot[0:SWA_HEAD_DIM, :] * (1.0 / den))
                o_ref[0, kk * (group // 2) + j] = jnp.concatenate(res, axis=0).T.astype(o_ref.dtype)

    @pl.when(lax.rem(i, 2) == 0)
    def _():
        body(s_even, m_even, s_odd, m_odd)

    @pl.when(lax.rem(i, 2) == 1)
    def _():
        body(s_odd, m_odd, s_even, m_even)


def _swa_attn(sink, q, kl, vtl, kc, vtc):
    b, nchunk, t, _ = q.shape
    nkv = kl.shape[1]
    cpk = nchunk // nkv
    c = kc.shape[2]
    tq = ROW_TILE
    nq = t // tq
    nblk = t // LANES
    bpt = tq // LANES
    nwin = bpt + 2 * (SWA_WINDOW // LANES)
    kvps = SWA_KV_PER_STEP

    def kblk(i, j, lag):
        ti = jnp.clip(i - lag, 0, nq - 1)
        return jnp.clip(ti * bpt - SWA_WINDOW // LANES + j, 0, nblk - 1)

    in_specs = [pl.BlockSpec(memory_space=pltpu.SMEM),
                pl.BlockSpec((1, kvps * cpk, tq, LANES), lambda bi, n, i: (bi, n, jnp.minimum(i, nq - 1), 0))]
    for j in range(nwin):
        in_specs.append(pl.BlockSpec((1, kvps, LANES, LANES), lambda bi, n, i, j=j: (bi, n, kblk(i, j, 0), 0)))
    in_specs.append(pl.BlockSpec((1, kvps, c, LANES), lambda bi, n, i: (bi, n, 0, 0)))
    for j in range(nwin):
        in_specs.append(pl.BlockSpec((1, kvps, LANES, LANES), lambda bi, n, i, j=j: (bi, n, 0, kblk(i, j, 1))))
    in_specs.append(pl.BlockSpec((1, kvps, LANES, c), lambda bi, n, i: (bi, n, 0, 0)))
    heads = kvps * SWA_HEADS // nkv
    scratch = [pltpu.VMEM((heads, c + nwin * LANES, tq), F32), pltpu.VMEM((heads, 1, tq), F32)] * 2
    return pl.pallas_call(
        functools.partial(_swa_kernel, t=t, tq=tq, nq=nq, nwin=nwin, kvps=kvps),
        grid=(b, nkv // kvps, nq + 1),
        in_specs=in_specs,
        out_specs=pl.BlockSpec((1, kvps * cpk, tq, LANES), lambda bi, n, i: (bi, n, jnp.maximum(i - 1, 0), 0)),
        out_shape=jax.ShapeDtypeStruct((b, nchunk, t, LANES), BF16),
        scratch_shapes=scratch,
        compiler_params=_cparams(("parallel", "arbitrary", "arbitrary")),
        name="swa_attn",
    )(sink, q, *([kl] * nwin), kc, *([vtl] * nwin), vtc)


def _outproj_kernel(*refs, conv, tm):
    if conv:
        (x_ref, mod_ref, bg_ref, p_ref, pprev_ref, pnext_ref, cw_ref, o_ref, w_ref,
         g2_ref, rwh_ref, rwl_ref, xn_ref, haug_ref, afft_ref) = refs
        i = pl.program_id(1)
        ni = pl.num_programs(1)
        p = p_ref[0].astype(F32)
        row = lax.broadcasted_iota(I32, p.shape, 0)
        halo_prev = jnp.where(i > 0, pprev_ref[0, BF16_ROWS - 1:BF16_ROWS, :].astype(F32), 0.0)
        halo_next = jnp.where(i < ni - 1, pnext_ref[0, 0:1, :].astype(F32), 0.0)
        p_prev = jnp.where(row == 0, halo_prev, pltpu.roll(p, 1, 0))
        p_next = jnp.where(row == tm - 1, halo_next, pltpu.roll(p, tm - 1, 0))
        cw = cw_ref[...]
        cv = p_prev * cw[0:1, :] + p * cw[1:2, :] + p_next * cw[2:3, :]
        u = (bg_ref[0].astype(F32) * cv).astype(BF16)
        lhs = jnp.concatenate([u] + [o_ref[0, j] for j in range(DIFF_HEADS)], axis=1)
    else:
        x_ref, mod_ref, o_ref, w_ref, g2_ref, rwh_ref, rwl_ref, xn_ref, haug_ref, afft_ref = refs
        lhs = jnp.concatenate([o_ref[0, j] for j in range(o_ref.shape[1])], axis=1)
    m = mod_ref[0]
    xn = x_ref[0] + m[2:3, :] * _dot(lhs, w_ref[...])
    xn_ref[0] = xn
    h = _modulate(xn, g2_ref[...], m[3:4, :], m[4:5, :])
    d = h.shape[1]
    haug_ref[0, :, 0:d] = h
    h_hi = h.astype(BF16)
    h_lo = (h - h_hi.astype(F32)).astype(BF16)
    logits = _dot(h_hi, rwh_ref[...]) + _dot(h_hi, rwl_ref[...]) + _dot(h_lo, rwh_ref[...])
    lane = lax.broadcasted_iota(I32, logits.shape, 1)
    logits = jnp.where(lane < N_EXPERTS, logits, NEG_INF)
    e = jnp.exp(logits - jnp.max(logits, axis=1, keepdims=True))
    aff = e / jnp.sum(e, axis=1, keepdims=True)
    haug_ref[0, :, d:d + AUG] = aff
    afft_ref[0] = aff.T[0:N_EXPERTS, :]


def _outproj(x, mods, mod_row, w_out, n2g, rw_hi, rw_lo, o, conv_args=None):
    b, t, d = x.shape
    tm = min(ROW_TILE, t)
    conv = conv_args is not None
    xmap = lambda bi, i: (bi, i, 0)
    in_specs = [pl.BlockSpec((1, tm, d), xmap),
                pl.BlockSpec((1, N_MOD, d), lambda bi, i: (mod_row(bi), 0, 0))]
    args = [x, mods]
    if conv:
        bg, p, cw = conv_args
        hb = tm // BF16_ROWS
        nhb = t // BF16_ROWS
        in_specs += [pl.BlockSpec((1, tm, CONV_DIM), xmap),
                     pl.BlockSpec((1, tm, CONV_DIM), xmap),
                     pl.BlockSpec((1, BF16_ROWS, CONV_DIM), lambda bi, i: (bi, jnp.maximum(i * hb - 1, 0), 0)),
                     pl.BlockSpec((1, BF16_ROWS, CONV_DIM), lambda bi, i: (bi, jnp.minimum((i + 1) * hb, nhb - 1), 0)),
                     pl.BlockSpec((3, CONV_DIM), lambda bi, i: (0, 0)),
                     pl.BlockSpec((1, DIFF_HEADS, tm, LANES), lambda bi, i: (bi, 0, i, 0))]
        args += [bg, p, p, p, cw, o]
    else:
        in_specs.append(pl.BlockSpec((1, o.shape[1], tm, LANES), lambda bi, i: (bi, 0, i, 0)))
        args.append(o)
    in_specs += [pl.BlockSpec(w_out.shape, lambda bi, i: (0, 0)),
                 pl.BlockSpec((1, d), lambda bi, i: (0, 0)),
                 pl.BlockSpec((d, LANES), lambda bi, i: (0, 0)),
                 pl.BlockSpec((d, LANES), lambda bi, i: (0, 0))]
    args += [w_out, n2g.reshape(1, d), rw_hi, rw_lo]
    return pl.pallas_call(
        functools.partial(_outproj_kernel, conv=conv, tm=tm),
        grid=(b, t // tm),
        in_specs=in_specs,
        out_specs=[pl.BlockSpec((1, tm, d), xmap),
                   pl.BlockSpec((1, tm, d + AUG), xmap),
                   pl.BlockSpec((1, N_EXPERTS, tm), lambda bi, i: (bi, 0, i))],
        out_shape=[jax.ShapeDtypeStruct((b, t, d), F32),
                   jax.ShapeDtypeStruct((b, t, d + AUG), F32),
                   jax.ShapeDtypeStruct((b, N_EXPERTS, t), F32)],
        compiler_params=_cparams(("parallel", "arbitrary")),
        name="outproj",
    )(*args)


def _route_kernel(aff_ref, idx_ref, post_ref, offs_ref, pinc_ref, tally_ref, *, t, cap, capp, lc):
    nch = t // lc
    ne = N_EXPERTS
    aff = aff_ref[0]

    def search(i, lo):
        cand = lo | lax.shift_left(jnp.int32(1), 30 - i)
        cnt = jnp.sum(jnp.where(aff >= pltpu.bitcast(cand, F32), 1.0, 0.0), axis=1, keepdims=True)
        return jnp.where(cnt >= cap, cand, lo)

    thr_bits = lax.fori_loop(0, 31, search, jnp.zeros((ne, 1), I32))
    thr = pltpu.bitcast(thr_bits, F32)
    gt = aff > thr
    eq = aff == thr
    need = cap - jnp.sum(jnp.where(gt, 1.0, 0.0), axis=1, keepdims=True)

    r_i = lax.broadcasted_iota(I32, (lc, lc), 0)
    c_i = lax.broadcasted_iota(I32, (lc, lc), 1)
    upper = jnp.where(r_i < c_i, 1.0, 0.0).astype(BF16)

    def excl_prefix(x):
        outs, offs = [], []
        carry = jnp.zeros((ne, 1), F32)
        for c in range(nch):
            xc = x[:, c * lc:(c + 1) * lc]
            offs.append(carry)
            outs.append(_dot(xc.astype(BF16), upper) + carry)
            carry = carry + jnp.sum(xc, axis=1, keepdims=True)
        offs.append(carry)
        return jnp.concatenate(outs, axis=1), offs

    eqf = jnp.where(eq, 1.0, 0.0)
    eq_rank, _ = excl_prefix(eqf)
    sel = jnp.logical_or(gt, jnp.logical_and(eq, eq_rank < need))
    self_ = jnp.where(sel, 1.0, 0.0)
    pos, offs = excl_prefix(self_)

    lane = lax.broadcasted_iota(I32, (ne, LANES), 1)
    om = jnp.zeros((ne, LANES), F32)
    for c, o in enumerate(offs):
        om = jnp.where(lane == c, o, om)
    offs_ref[0] = om

    posm = jnp.where(sel, pos, -1.0)
    pad = jnp.full((LANES - ne, lc), -1.0, F32)
    for c in range(nch):
        blk = jnp.concatenate([posm[:, c * lc:(c + 1) * lc], pad], axis=0)
        post_ref[0, c * lc:(c + 1) * lc, :] = blk.T
    pinc = pos + self_
    nt = t // LANES
    for k in range(nt):
        pinc_ref[k] = pinc[:, k * LANES:(k + 1) * LANES]

    rows = BF16_ROWS
    sub = lax.broadcasted_iota(I32, (rows, LANES), 0).astype(F32)
    one = jnp.ones((rows, LANES), BF16)
    zero = jnp.zeros((rows, LANES), BF16)
    lane_c = lax.broadcasted_iota(I32, (capp, LANES), 1)
    unroll = min(8, nt)
    assert nt <= 256

    def per_expert(e, carry):
        for jb in range(capp // LANES):
            def tiles(kk, acc, jb=jb):
                for u in range(unroll):
                    rel = pinc_ref[kk * unroll + u, pl.ds(e, 1), :] - float(jb * LANES)
                    q = (jnp.clip(rel, -float(rows), float(LANES + rows)) - sub).astype(BF16)
                    acc = acc + jnp.concatenate(
                        [jnp.where(q <= float(rows * i), one, zero) for i in range(LANES // rows)], axis=0)
                return acc
            tally_ref[e, jb * LANES:(jb + 1) * LANES, :] = lax.fori_loop(
                0, nt // unroll, tiles, jnp.zeros((LANES, LANES), BF16))
        return carry

    lax.fori_loop(0, ne, per_expert, 0)
    idx_m = jnp.zeros((capp, LANES), F32)
    for e in range(ne):
        col = jnp.sum(tally_ref[e].astype(F32), axis=1, keepdims=True)
        idx_m = jnp.where(lane_c == e, col, idx_m)
    base = pl.program_id(0) * t
    idx_ref[0] = idx_m.T[0:ne, :].astype(I32) + base


def _route(aff_t, cap):
    b, ne, t = aff_t.shape
    lc = min(ROUTE_CHUNK, t)
    capp = max(cap, LANES)
    return pl.pallas_call(
        functools.partial(_route_kernel, t=t, cap=cap, capp=capp, lc=lc),
        grid=(b,),
        in_specs=[pl.BlockSpec((1, ne, t), lambda bi: (bi, 0, 0))],
        out_specs=[pl.BlockSpec((1, ne, capp), lambda bi: (bi, 0, 0)),
                   pl.BlockSpec((1, t, LANES), lambda bi: (bi, 0, 0)),
                   pl.BlockSpec((1, ne, LANES), lambda bi: (bi, 0, 0))],
        out_shape=[jax.ShapeDtypeStruct((b, ne, capp), I32),
                   jax.ShapeDtypeStruct((b, t, LANES), F32),
                   jax.ShapeDtypeStruct((b, ne, LANES), F32)],
        scratch_shapes=[pltpu.VMEM((t // LANES, ne, LANES), F32),
                        pltpu.VMEM((ne, capp, LANES), BF16)],
        compiler_params=_cparams(("arbitrary",)),
        name="route",
    )(aff_t)


def _ffn_kernel(idx_ref, haug_ref, wg_ref, wu_ref, wd_ref, y_ref, hbuf, w_in, w_dn, gsem, *, rows, d):
    e = pl.program_id(0)
    step = e * pl.num_programs(1) + pl.program_id(1)
    nsteps = pl.num_programs(0) * pl.num_programs(1)

    @pl.when(pl.program_id(1) == 0)
    def _():
        w_in[0] = wg_ref[0].astype(BF16)
        w_in[1] = wu_ref[0].astype(BF16)
        w_dn[...] = wd_ref[0].astype(BF16)

    def row_copy(src_row, dst_slot, dst_row):
        return pltpu.make_async_copy(haug_ref.at[pl.ds(src_row, 1)],
                                     hbuf.at[dst_slot, pl.ds(dst_row, 1)], gsem.at[dst_slot])

    def wait_rows(sl):
        pltpu.make_async_copy(haug_ref.at[pl.ds(0, rows)], hbuf.at[sl], gsem.at[sl]).wait()

    @pl.when(step == 0)
    def _():
        def body(r, carry):
            row_copy(idx_ref[r], 0, r).start()
            return carry
        lax.fori_loop(0, rows, body, 0)

    def run(slot):
        base = jnp.minimum(step + 1, nsteps - 1) * rows
        for r in range(rows):
            row_copy(idx_ref[base + r], 1 - slot, r).start()

        wait_rows(slot)
        hrow = hbuf[slot]
        hs = hrow[:, 0:d].astype(BF16)
        lane = lax.broadcasted_iota(I32, (rows, AUG), 1)
        gate = jnp.sum(jnp.where(lane == e, hrow[:, d:d + AUG], 0.0), axis=1, keepdims=True)
        a = _dot(hs, w_in[0])
        u = _dot(hs, w_in[1])
        hm = (a * jax.nn.sigmoid(a) * u).astype(BF16)
        y_ref[0] = (_dot(hm, w_dn[...]) * gate).astype(y_ref.dtype)

        @pl.when(step == nsteps - 1)
        def _():
            wait_rows(1 - slot)

    for parity in range(2):
        pl.when(lax.rem(step, 2) == parity)(functools.partial(run, parity))


def _ffn(idx_flat, haug, wg, wu, wd, rows):
    ne, d, f = wg.shape
    n_rows_total = idx_flat.shape[0]
    steps = n_rows_total // (ne * rows)
    grid_spec = pltpu.PrefetchScalarGridSpec(
        num_scalar_prefetch=1,
        grid=(ne, steps),
        in_specs=[pl.BlockSpec(memory_space=pl.ANY),
                  pl.BlockSpec((1, d, f), lambda e, s, idx: (e, 0, 0)),
                  pl.BlockSpec((1, d, f), lambda e, s, idx: (e, 0, 0)),
                  pl.BlockSpec((1, f, d), lambda e, s, idx: (e, 0, 0))],
        out_specs=pl.BlockSpec((1, rows, d), lambda e, s, idx: (e, s, 0)),
        scratch_shapes=[pltpu.VMEM((2, rows, d + AUG), F32),
                        pltpu.VMEM((2, d, f), BF16),
                        pltpu.VMEM((f, d), BF16),
                        pltpu.SemaphoreType.DMA((2,))],
    )
    return pl.pallas_call(
        functools.partial(_ffn_kernel, rows=rows, d=d),
        grid_spec=grid_spec,
        out_shape=jax.ShapeDtypeStruct((ne, steps * rows, d), BF16),
        compiler_params=_cparams(("arbitrary", "arbitrary")),
        name="ffn",
    )(idx_flat, haug, wg, wu, wd)


def _combine_kernel(ws_ref, nr_ref, x_ref, mod_ref, post_ref, fg_ref, y_ref, o_ref, stage, acc, sem,
                    *, cap, tm, final_norm):
    nchunk = pl.num_programs(1)
    step = pl.program_id(0) * nchunk + pl.program_id(1)
    nsteps = pl.num_programs(0) * nchunk
    slot = lax.rem(step, 2)
    ne = N_EXPERTS
    w = COMB_W
    per = LANES // w
    total_rows = y_ref.shape[1]
    lane_row = lax.broadcasted_iota(I32, (1, LANES), 1)

    def window(st, e, r):
        sample = lax.div(st, nchunk)
        nominal = ws_ref[st * ne + e] + r * w
        grow = pl.multiple_of(jnp.minimum(sample * cap + nominal, total_rows - w), BF16_ROWS)
        return nominal, grow, grow - sample * cap

    def copy(e, grow, sl, si):
        return pltpu.make_async_copy(y_ref.at[e, pl.ds(grow, w)], stage.at[sl, pl.ds(e * w, w)], sem.at[si])

    def fetch(st, r, sl, si):
        for e in range(ne):
            copy(e, window(st, e, r)[1], sl, si).start()

    def wait(sl, si):
        for e in range(ne):
            copy(e, 0, sl, si).wait()

    def gathered(st, r, sl):
        pt = post_ref[0]
        blocks = []
        for g in range(ne // per):
            tgt = jnp.zeros((1, LANES), F32)
            mine = jnp.zeros((tm, LANES), F32)
            for k in range(per):
                e = g * per + k
                nominal, _, first = window(st, e, r)
                inb = jnp.logical_and(lane_row >= k * w, lane_row < (k + 1) * w)
                slot = first - k * w + lane_row
                tgt = jnp.where(inb, jnp.where(slot >= nominal, slot, -2).astype(F32), tgt)
                mine = jnp.where(inb, pt[:, e:e + 1], mine)
            blocks.append(jnp.where(mine == tgt, 1.0, 0.0).astype(BF16))
        onehot = jnp.concatenate(blocks, axis=1)
        return _dot(onehot, stage[sl])

    @pl.when(step == 0)
    def _():
        fetch(0, 0, 0, 0)

    @pl.when(step + 1 < nsteps)
    def _():
        fetch(step + 1, 0, 1 - slot, 1 - slot)

    wait(slot, slot)
    acc[...] = gathered(step, 0, slot)

    def extra_round(r, carry):
        fetch(step, r, slot, 2)
        wait(slot, 2)
        acc[...] += gathered(step, r, slot)
        return carry

    lax.fori_loop(1, nr_ref[step], extra_round, 0)
    m = mod_ref[0]
    xn = x_ref[0] + m[5:6, :] * acc[...]
    if final_norm:
        ms = jnp.mean(xn * xn, axis=-1, keepdims=True)
        xn = xn * lax.rsqrt(ms + NORM_EPS) * fg_ref[...]
    o_ref[0] = xn


def _combine(ws_flat, nr_flat, x, mods, mod_row, post, y, cap, final_g=None):
    b, t, d = x.shape
    tm = min(ROW_TILE, t)
    ne = N_EXPERTS
    final_norm = final_g is not None
    fg = (final_g if final_norm else jnp.ones((d,), F32)).reshape(1, d)
    grid_spec = pltpu.PrefetchScalarGridSpec(
        num_scalar_prefetch=2,
        grid=(b, t // tm),
        in_specs=[pl.BlockSpec((1, tm, d), lambda bi, i, a, c: (bi, i, 0)),
                  pl.BlockSpec((1, N_MOD, d), lambda bi, i, a, c: (mod_row(bi), 0, 0)),
                  pl.BlockSpec((1, tm, LANES), lambda bi, i, a, c: (bi, i, 0)),
                  pl.BlockSpec((1, d), lambda bi, i, a, c: (0, 0)),
                  pl.BlockSpec(memory_space=pl.ANY)],
        out_specs=pl.BlockSpec((1, tm, d), lambda bi, i, a, c: (bi, i, 0)),
        scratch_shapes=[pltpu.VMEM((2, ne * COMB_W, d), BF16),
                        pltpu.VMEM((tm, d), F32),
                        pltpu.SemaphoreType.DMA((3,))],
    )
    return pl.pallas_call(
        functools.partial(_combine_kernel, cap=cap, tm=tm, final_norm=final_norm),
        grid_spec=grid_spec,
        out_shape=jax.ShapeDtypeStruct((b, t, d), F32),
        compiler_params=_cparams(("arbitrary", "arbitrary")),
        name="combine",
    )(ws_flat, nr_flat, x, mods, post, fg, y)


def _moe(xn, haug, aff_t, mods, mod_row, wg, wu, wd, rows_per_step, final_g=None):
    b, t, d = xn.shape
    ne = N_EXPERTS
    cap = CAPACITY_FACTOR * t // ne
    idx, post, offs = _route(aff_t, cap)
    idx_flat = jnp.transpose(idx[:, :, :cap], (1, 0, 2)).reshape(-1)
    y = _ffn(idx_flat, haug.reshape(b * t, d + AUG), wg, wu, wd, rows_per_step)
    tm = min(ROW_TILE, t)
    nchunk = t // tm
    lc = min(ROUTE_CHUNK, t)
    per = tm // lc
    offs_i = offs.astype(I32)[:, :, 0:t // lc + 1:per]
    start = jnp.transpose(offs_i[:, :, :nchunk], (0, 2, 1))
    end = jnp.transpose(offs_i[:, :, 1:], (0, 2, 1))
    ws = (start // BF16_ROWS) * BF16_ROWS
    nr = jnp.maximum(jnp.max((end - ws + COMB_W - 1) // COMB_W, axis=2), 1)
    return _combine(ws.reshape(-1), nr.reshape(-1), xn, mods, mod_row, post, y, cap, final_g)


def _split_hi_lo(w):
    hi = w.astype(BF16)
    lo = (w - hi.astype(F32)).astype(BF16)
    return hi, lo


def _router_weights(rw):
    d, ne = rw.shape
    pad = jnp.zeros((d, LANES - ne), F32)
    return _split_hi_lo(jnp.concatenate([rw, pad], axis=1))


def kernel(x, c, ctx, c_ctx, l0_ada_w, l0_ada_b, l0_norm1_g, l0_w_in, l0_conv_w, l0_lambda_q1, l0_lambda_k1, l0_lambda_q2, l0_lambda_k2, l0_subln_g, l0_w_out, l0_norm2_g, l0_router_w, l0_exp_w_gate, l0_exp_w_up, l0_exp_w_down, l1_ada_w, l1_ada_b, l1_norm1_g, l1_w_qkv, l1_sink, l1_w_out, l1_norm2_g, l1_router_w, l1_exp_w_gate, l1_exp_w_up, l1_exp_w_down, final_norm_g):
    b, t, d = x.shape
    nctx = ctx.shape[1]
    assert b + 1 <= MOD_ROWS
    lat_row = lambda bi: bi
    ctx_row = lambda bi: b

    cc = jnp.zeros((MOD_ROWS, d), F32).at[:b].set(c).at[b].set(c_ctx)
    mods0 = _ada(cc, l0_ada_w, l0_ada_b).reshape(MOD_ROWS, N_MOD, d)
    mods1 = _ada(cc, l1_ada_w, l1_ada_b).reshape(MOD_ROWS, N_MOD, d)
    tables = _rope_tables(t)
    qscale = LOG2E * DIFF_HEAD_DIM ** -0.5

    w_in = l0_w_in.astype(BF16)
    cd = CONV_DIM
    qkw = DIFF_HEADS * 2 * DIFF_HEAD_DIM
    wvt = jnp.transpose(l0_w_in[:, 3 * cd + 2 * qkw:]).astype(BF16)
    outs0 = [("plain", 0, cd, False, 1.0),
             ("mul", cd, cd, False, 1.0),
             ("heads", 3 * cd, qkw, True, qscale),
             ("heads", 3 * cd + qkw, qkw, True, 1.0)]
    w_main = w_in[:, :3 * cd + 2 * qkw]
    bg_l, cx_l, q_l, k_l, vt_l = _proj(x, mods0, lat_row, l0_norm1_g, w_main, outs0, tables, wvt)
    bg_c, cx_c, q_c, k_c, vt_c = _proj(ctx, mods0, ctx_row, l0_norm1_g, w_main, outs0, None, wvt)
    lams = (l0_lambda_q1, l0_lambda_k1, l0_lambda_q2, l0_lambda_k2)
    o_l = _diff_attn(q_l, [k_c, k_l], [vt_c, vt_l], lams, l0_subln_g)
    o_c = _diff_attn(q_c, [k_c], [vt_c], lams, l0_subln_g)

    w_out0 = l0_w_out.astype(BF16)
    rw0 = _router_weights(l0_router_w)
    wg0, wu0, wd0 = l0_exp_w_gate, l0_exp_w_up, l0_exp_w_down
    x1, haug, aff_t = _outproj(x, mods0, lat_row, w_out0, l0_norm2_g, *rw0, o_l,
                               conv_args=(bg_l, cx_l, l0_conv_w))
    x = _moe(x1, haug, aff_t, mods0, lat_row, wg0, wu0, wd0, CAPACITY_FACTOR * t // N_EXPERTS)
    c1, haug_c, aff_tc = _outproj(ctx, mods0, ctx_row, w_out0, l0_norm2_g, *rw0, o_c,
                                  conv_args=(bg_c, cx_c, l0_conv_w))
    ctx = _moe(c1, haug_c, aff_tc, mods0, ctx_row, wg0, wu0, wd0, b * (CAPACITY_FACTOR * nctx // N_EXPERTS))

    qw = SWA_HEADS * SWA_HEAD_DIM
    kvw = SWA_KV_HEADS * SWA_HEAD_DIM
    wq = l1_w_qkv[:, :qw]
    wk = l1_w_qkv[:, qw:qw + kvw].reshape(d, SWA_KV_HEADS, SWA_HEAD_DIM)
    wv = l1_w_qkv[:, qw + kvw:].reshape(d, SWA_KV_HEADS, SWA_HEAD_DIM)
    wk2 = jnp.concatenate([wk, wk], axis=2).reshape(d, 2 * kvw)
    wv2 = jnp.concatenate([wv, jnp.zeros_like(wv)], axis=2).reshape(d, 2 * kvw)
    w_qk = jnp.concatenate([wq, wk2], axis=1).astype(BF16)
    wvt1 = jnp.transpose(wv2).astype(BF16)
    outs1 = [("heads", 0, qw, True, LOG2E * SWA_HEAD_DIM ** -0.5),
             ("heads", qw, 2 * kvw, True, 1.0)]
    q1, k1, vt1 = _proj(x, mods1, lat_row, l1_norm1_g, w_qk, outs1, tables, wvt1)
    k1c, vt1c = _proj(ctx, mods1, ctx_row, l1_norm1_g, w_qk[:, qw:], [("heads", 0, 2 * kvw, False, 1.0)], None, wvt1)
    o1 = _swa_attn(l1_sink, q1, k1, vt1, k1c, vt1c)

    rw1 = _router_weights(l1_router_w)
    wg1, wu1, wd1 = l1_exp_w_gate, l1_exp_w_up, l1_exp_w_down
    x1, haug, aff_t = _outproj(x, mods1, lat_row, l1_w_out.astype(BF16), l1_norm2_g, *rw1, o1)
    return _moe(x1, haug, aff_t, mods1, lat_row, wg1, wu1, wd1, CAPACITY_FACTOR * t // N_EXPERTS,
                final_g=final_norm_g)
```

```python
import functools
import math

import jax
import jax.numpy as jnp
from jax import lax
from jax.experimental import pallas as pl
from jax.experimental.pallas import tpu as pltpu

F32 = jnp.float32
BF16 = jnp.bfloat16
I32 = jnp.int32

GRID_W = 64
ROPE_BASE = 10000.0
NORM_EPS = 1e-6
NEG_INF = -1e30
N_MOD = 6

CONV_DIM = 512
DIFF_HEADS = 4
DIFF_HEAD_DIM = 64
DIFF_V_DIM = 128
LAM_INIT0 = 0.8 - 0.6 * math.exp(-0.3 * 0)

SWA_HEADS = 16
SWA_KV_HEADS = 4
SWA_HEAD_DIM = 64
SWA_WINDOW = 128

N_EXPERTS = 16
CAPACITY_FACTOR = 2

LANES = 128
SUBLANES = 8
MOD_ROWS = 16
AUG = LANES
ROW_TILE = 256
PROJ_TILE = 1024
ADA_TILE = 512
ROUTE_CHUNK = 256
COMB_W = 64
BF16_ROWS = 16
VT_ROWS = DIFF_V_DIM
SWA_KV_PER_STEP = 4
LOG2E = math.log2(math.e)
VMEM_LIMIT = 56 * 1024 * 1024


def _cparams(sem):
    return pltpu.CompilerParams(dimension_semantics=sem, vmem_limit_bytes=VMEM_LIMIT)


def _dot(a, b):
    return jnp.dot(a, b, preferred_element_type=F32)


def _dot_nt(a, b):
    return lax.dot_general(a, b, (((1,), (1,)), ((), ())), preferred_element_type=F32)


def _modulate(x, gain, shift, scale):
    ms = jnp.mean(x * x, axis=-1, keepdims=True)
    return x * lax.rsqrt(ms + NORM_EPS) * (gain * (1.0 + scale)) + shift


def _ada_kernel(c_ref, w_ref, b_ref, o_ref):
    c = c_ref[...]
    s = (c * jax.nn.sigmoid(c)).astype(BF16)
    o_ref[...] = _dot(s, w_ref[...].astype(BF16)) + b_ref[...]


def _ada(cc, w, b):
    d, n = w.shape
    tn = ADA_TILE
    return pl.pallas_call(
        _ada_kernel,
        grid=(n // tn,),
        in_specs=[pl.BlockSpec((MOD_ROWS, d), lambda j: (0, 0)),
                  pl.BlockSpec((d, tn), lambda j: (0, j)),
                  pl.BlockSpec((1, tn), lambda j: (0, j))],
        out_specs=pl.BlockSpec((MOD_ROWS, tn), lambda j: (0, j)),
        out_shape=jax.ShapeDtypeStruct((MOD_ROWS, n), F32),
        compiler_params=_cparams(("arbitrary",)),
        name="ada",
    )(cc, w, b.reshape(1, n))


def _rope_tables(t):
    n_freq = DIFF_HEAD_DIM // 4
    inv_freq = ROPE_BASE ** (-jnp.arange(n_freq, dtype=F32) / n_freq)
    pos = jnp.arange(t, dtype=I32)
    row = (pos // GRID_W).astype(F32)[:, None] * inv_freq
    col = (pos % GRID_W).astype(F32)[:, None] * inv_freq
    zeros = jnp.zeros_like(row)
    cos64 = jnp.concatenate([jnp.cos(row), jnp.cos(row), jnp.cos(col), jnp.cos(col)], axis=1)
    sa64 = jnp.concatenate([-jnp.sin(row), zeros, -jnp.sin(col), zeros], axis=1)
    sb64 = jnp.concatenate([zeros, jnp.sin(row), zeros, jnp.sin(col)], axis=1)
    rep = lambda a: jnp.concatenate([a, a], axis=1)
    return rep(cos64), rep(sa64), rep(sb64)


def _rope128(c, cos, sa, sb):
    return c * cos + pltpu.roll(c, LANES - 16, 1) * sa + pltpu.roll(c, 16, 1) * sb


def _proj_kernel(*refs, outs, rope, has_vt):
    x_ref, mod_ref, g_ref, w_ref = refs[:4]
    pos = 4
    if rope:
        cos_ref, sa_ref, sb_ref = refs[pos:pos + 3]
        pos += 3
    if has_vt:
        wvt_ref = refs[pos]
        pos += 1
    out_refs = refs[pos:]
    m = mod_ref[0]
    h = _modulate(x_ref[0], g_ref[...], m[0:1, :], m[1:2, :]).astype(BF16)
    y = _dot(h, w_ref[...])
    for (kind, c0, width, do_rope, scale), o_ref in zip(outs, out_refs):
        for j in range(width // LANES):
            c = y[:, c0 + j * LANES:c0 + (j + 1) * LANES]
            if kind == "mul":
                c = c * y[:, c0 + width + j * LANES:c0 + width + (j + 1) * LANES]
            if do_rope and rope:
                c = _rope128(c, cos_ref[...], sa_ref[...], sb_ref[...])
            if scale != 1.0:
                c = c * scale
            if kind == "heads":
                o_ref[0, j] = c.astype(o_ref.dtype)
            else:
                o_ref[0, :, j * LANES:(j + 1) * LANES] = c.astype(o_ref.dtype)
    if has_vt:
        vt = _dot_nt(wvt_ref[...], h)
        o_ref = out_refs[len(outs)]
        for j in range(vt.shape[0] // LANES):
            o_ref[0, j] = vt[j * LANES:(j + 1) * LANES, :].astype(o_ref.dtype)


def _proj(x, mods, mod_row, gain, w, outs, tables=None, wvt=None):
    b, t, d = x.shape
    tm = min(PROJ_TILE, t)
    n = w.shape[1]
    rope = tables is not None
    in_specs = [pl.BlockSpec((1, tm, d), lambda bi, i: (bi, i, 0)),
                pl.BlockSpec((1, N_MOD, d), lambda bi, i: (mod_row(bi), 0, 0)),
                pl.BlockSpec((1, d), lambda bi, i: (0, 0)),
                pl.BlockSpec((d, n), lambda bi, i: (0, 0))]
    args = [x, mods, gain.reshape(1, d), w]
    if rope:
        in_specs += [pl.BlockSpec((tm, LANES), lambda bi, i: (i, 0))] * 3
        args += list(tables)
    if wvt is not None:
        in_specs.append(pl.BlockSpec(wvt.shape, lambda bi, i: (0, 0)))
        args.append(wvt)
    out_specs, out_shapes = [], []
    for (kind, c0, width, do_rope, scale) in outs:
        if kind == "heads":
            nh = width // LANES
            out_specs.append(pl.BlockSpec((1, nh, tm, LANES), lambda bi, i: (bi, 0, i, 0)))
            out_shapes.append(jax.ShapeDtypeStruct((b, nh, t, LANES), BF16))
        else:
            out_specs.append(pl.BlockSpec((1, tm, width), lambda bi, i: (bi, i, 0)))
            out_shapes.append(jax.ShapeDtypeStruct((b, t, width), BF16))
    if wvt is not None:
        nh = wvt.shape[0] // LANES
        out_specs.append(pl.BlockSpec((1, nh, VT_ROWS, tm), lambda bi, i: (bi, 0, 0, i)))
        out_shapes.append(jax.ShapeDtypeStruct((b, nh, VT_ROWS, t), BF16))
    return pl.pallas_call(
        functools.partial(_proj_kernel, outs=tuple(outs), rope=rope, has_vt=wvt is not None),
        grid=(b, t // tm),
        in_specs=in_specs,
        out_specs=out_specs,
        out_shape=out_shapes,
        compiler_params=_cparams(("parallel", "arbitrary")),
        name="proj",
    )(*args)


def _diff_attn_kernel(*refs, nseg):
    q_ref = refs[0]
    k_refs = refs[1:1 + nseg]
    vt_refs = refs[1 + nseg:1 + 2 * nseg]
    lq1, lk1, lq2, lk2, g_ref, o_ref, s_even, m_even, s_odd, m_odd = refs[1 + 2 * nseg:]
    i = pl.program_id(1)
    offs = [0]
    for k in k_refs:
        offs.append(offs[-1] + k.shape[2])

    @pl.when(i == 0)
    def _():
        s_odd[...] = jnp.zeros(s_odd.shape, F32)
        m_odd[...] = jnp.zeros(m_odd.shape, F32)

    def body(s_w, m_w, s_r, m_r):
        q = q_ref[0, 0]
        lane = lax.broadcasted_iota(I32, q.shape, 1)
        zero = jnp.zeros_like(q)
        qs = (jnp.where(lane < DIFF_HEAD_DIM, q, zero), jnp.where(lane >= DIFF_HEAD_DIM, q, zero))
        for half, qh in enumerate(qs):
            m = None
            for si, k in enumerate(k_refs):
                s = _dot_nt(k[0, 0], qh)
                s_w[half, offs[si]:offs[si + 1], :] = s
                ms = jnp.max(s, axis=0, keepdims=True)
                m = ms if m is None else jnp.maximum(m, ms)
            m_w[half] = m

        lam = (jnp.exp(jnp.sum(lq1[...] * lk1[...], axis=1, keepdims=True))
               - jnp.exp(jnp.sum(lq2[...] * lk2[...], axis=1, keepdims=True)) + LAM_INIT0)
        probs, norms = [], []
        for half in range(2):
            p = [jnp.exp2(s_r[half, offs[si]:offs[si + 1], :] - m_r[half]) for si in range(nseg)]
            norms.append(functools.reduce(jnp.add, [jnp.sum(x, axis=0, keepdims=True) for x in p]))
            probs.append(p)
        c2 = lam * norms[0] / norms[1]
        ot = None
        for si in range(nseg):
            a = (probs[0][si] - probs[1][si] * c2).astype(BF16)
            part = _dot(vt_refs[si][0, 0], a)
            ot = part if ot is None else ot + part
        ot = ot * (1.0 / norms[0])
        msq = jnp.mean(ot * ot, axis=0, keepdims=True)
        on = ot * lax.rsqrt(msq + NORM_EPS) * (g_ref[...] * (1.0 - LAM_INIT0))
        o_ref[0, 0] = on.T.astype(o_ref.dtype)

    @pl.when(lax.rem(i, 2) == 0)
    def _():
        body(s_even, m_even, s_odd, m_odd)

    @pl.when(lax.rem(i, 2) == 1)
    def _():
        body(s_odd, m_odd, s_even, m_even)


def _diff_attn(q, ks, vts, lams, subln_g):
    b, nh, t, _ = q.shape
    tq = min(ROW_TILE, t)
    nq = t // tq
    nseg = len(ks)
    tk = sum(k.shape[2] for k in ks)
    ntiles = nh * nq

    def tile(s, lag):
        ts = jnp.clip(s - lag, 0, ntiles - 1)
        return ts // nq, ts % nq

    in_specs = [pl.BlockSpec((1, 1, tq, LANES), lambda bi, s: (bi, *tile(s, 0), 0))]
    for k in ks:
        in_specs.append(pl.BlockSpec((1, 1, k.shape[2], LANES), lambda bi, s: (bi, tile(s, 0)[0], 0, 0)))
    for vt in vts:
        in_specs.append(pl.BlockSpec((1, 1, VT_ROWS, vt.shape[3]), lambda bi, s: (bi, tile(s, 1)[0], 0, 0)))
    in_specs += [pl.BlockSpec((1, DIFF_HEAD_DIM), lambda bi, s: (0, 0))] * 4
    in_specs.append(pl.BlockSpec((DIFF_V_DIM, 1), lambda bi, s: (0, 0)))
    return pl.pallas_call(
        functools.partial(_diff_attn_kernel, nseg=nseg),
        grid=(b, ntiles + 1),
        in_specs=in_specs,
        out_specs=pl.BlockSpec((1, 1, tq, LANES), lambda bi, s: (bi, *tile(s, 1), 0)),
        out_shape=jax.ShapeDtypeStruct((b, nh, t, LANES), BF16),
        scratch_shapes=[pltpu.VMEM((2, tk, tq), F32), pltpu.VMEM((2, 1, tq), F32)] * 2,
        compiler_params=_cparams(("parallel", "arbitrary")),
        name="diff_attn",
    )(q, *ks, *vts, *[l.reshape(1, DIFF_HEAD_DIM) for l in lams], subln_g.reshape(DIFF_V_DIM, 1))


def _swa_kernel(*refs, t, tq, nq, nwin, kvps):
    sink_ref, q_ref = refs[0], refs[1]
    kw_refs = refs[2:2 + nwin]
    kc_ref = refs[2 + nwin]
    vw_refs = refs[3 + nwin:3 + 2 * nwin]
    vc_ref, o_ref, s_even, m_even, s_odd, m_odd = refs[3 + 2 * nwin:]
    n = pl.program_id(1)
    i = pl.program_id(2)
    c = kc_ref.shape[2]
    span = nwin * LANES
    group = SWA_HEADS // SWA_KV_HEADS

    @pl.when(i == 0)
    def _():
        s_odd[...] = jnp.zeros(s_odd.shape, F32)
        m_odd[...] = jnp.zeros(m_odd.shape, F32)

    def body(s_w, m_w, s_r, m_r):
        ti = jnp.minimum(i, nq - 1)
        kpos = (ti * tq - SWA_WINDOW) + lax.broadcasted_iota(I32, (span, tq), 0)
        qpos = ti * tq + lax.broadcasted_iota(I32, (span, tq), 1)
        ok = jnp.logical_and(jnp.logical_and(kpos >= 0, kpos < t), jnp.abs(qpos - kpos) <= SWA_WINDOW)
        bias = jnp.where(ok, 0.0, NEG_INF)
        lane = lax.broadcasted_iota(I32, (tq, LANES), 1)
        for kk in range(kvps):
            kwin = jnp.concatenate([r[0, kk] for r in kw_refs], axis=0)
            kc = kc_ref[0, kk]
            for j in range(group // 2):
                qc = q_ref[0, kk * (group // 2) + j]
                zero = jnp.zeros_like(qc)
                for half in range(2):
                    hh = kk * group + 2 * j + half
                    keep = (lane < SWA_HEAD_DIM) if half == 0 else (lane >= SWA_HEAD_DIM)
                    qz = jnp.where(keep, qc, zero)
                    s_c = _dot_nt(kc, qz)
                    s_l = _dot_nt(kwin, qz) + bias
                    sink = sink_ref[n * kvps * group + hh] * LOG2E
                    s_w[hh, 0:c, :] = s_c
                    s_w[hh, c:c + span, :] = s_l
                    m_w[hh] = jnp.maximum(jnp.maximum(jnp.max(s_c, axis=0, keepdims=True),
                                                      jnp.max(s_l, axis=0, keepdims=True)), sink)

        for kk in range(kvps):
            vwin = jnp.concatenate([r[0, kk] for r in vw_refs], axis=1)
            vc = vc_ref[0, kk]
            vwin = jnp.where(lax.broadcasted_iota(I32, vwin.shape, 0) == SWA_HEAD_DIM, jnp.ones_like(vwin), vwin)
            vc = jnp.where(lax.broadcasted_iota(I32, vc.shape, 0) == SWA_HEAD_DIM, jnp.ones_like(vc), vc)
            for j in range(group // 2):
                res = []
                for half in range(2):
                    hh = kk * group + 2 * j + half
                    m = m_r[hh]
                    p_c = jnp.exp2(s_r[hh, 0:c, :] - m).astype(BF16)
                    p_l = jnp.exp2(s_r[hh, c:c + span, :] - m).astype(BF16)
                    ot = _dot(vc, p_c) + _dot(vwin, p_l)
                    sink = sink_ref[n * kvps * group + hh] * LOG2E
                    den = ot[SWA_HEAD_DIM:SWA_HEAD_DIM + 1, :] + jnp.exp2(sink - m)
                    res.append(ot[0:SWA_HEAD_DIM, :] * (1.0 / den))
                o_ref[0, kk * (group // 2) + j] = jnp.concatenate(res, axis=0).T.astype(o_ref.dtype)

    @pl.when(lax.rem(i, 2) == 0)
    def _():
        body(s_even, m_even, s_odd, m_odd)

    @pl.when(lax.rem(i, 2) == 1)
    def _():
        body(s_odd, m_odd, s_even, m_even)


def _swa_attn(sink, q, kl, vtl, kc, vtc):
    b, nchunk, t, _ = q.shape
    nkv = kl.shape[1]
    cpk = nchunk // nkv
    c = kc.shape[2]
    tq = ROW_TILE
    nq = t // tq
    nblk = t // LANES
    bpt = tq // LANES
    nwin = bpt + 2 * (SWA_WINDOW // LANES)
    kvps = SWA_KV_PER_STEP

    def kblk(i, j, lag):
        ti = jnp.clip(i - lag, 0, nq - 1)
        return jnp.clip(ti * bpt - SWA_WINDOW // LANES + j, 0, nblk - 1)

    in_specs = [pl.BlockSpec(memory_space=pltpu.SMEM),
                pl.BlockSpec((1, kvps * cpk, tq, LANES), lambda bi, n, i: (bi, n, jnp.minimum(i, nq - 1), 0))]
    for j in range(nwin):
        in_specs.append(pl.BlockSpec((1, kvps, LANES, LANES), lambda bi, n, i, j=j: (bi, n, kblk(i, j, 0), 0)))
    in_specs.append(pl.BlockSpec((1, kvps, c, LANES), lambda bi, n, i: (bi, n, 0, 0)))
    for j in range(nwin):
        in_specs.append(pl.BlockSpec((1, kvps, LANES, LANES), lambda bi, n, i, j=j: (bi, n, 0, kblk(i, j, 1))))
    in_specs.append(pl.BlockSpec((1, kvps, LANES, c), lambda bi, n, i: (bi, n, 0, 0)))
    heads = kvps * SWA_HEADS // nkv
    scratch = [pltpu.VMEM((heads, c + nwin * LANES, tq), F32), pltpu.VMEM((heads, 1, tq), F32)] * 2
    return pl.pallas_call(
        functools.partial(_swa_kernel, t=t, tq=tq, nq=nq, nwin=nwin, kvps=kvps),
        grid=(b, nkv // kvps, nq + 1),
        in_specs=in_specs,
        out_specs=pl.BlockSpec((1, kvps * cpk, tq, LANES), lambda bi, n, i: (bi, n, jnp.maximum(i - 1, 0), 0)),
        out_shape=jax.ShapeDtypeStruct((b, nchunk, t, LANES), BF16),
        scratch_shapes=scratch,
        compiler_params=_cparams(("parallel", "arbitrary", "arbitrary")),
        name="swa_attn",
    )(sink, q, *([kl] * nwin), kc, *([vtl] * nwin), vtc)


def _outproj_kernel(*refs, conv, tm):
    if conv:
        (x_ref, mod_ref, bg_ref, p_ref, pprev_ref, pnext_ref, cw_ref, o_ref, w_ref,
         g2_ref, rwh_ref, rwl_ref, xn_ref, haug_ref, afft_ref) = refs
        i = pl.program_id(1)
        ni = pl.num_programs(1)
        p = p_ref[0].astype(F32)
        row = lax.broadcasted_iota(I32, p.shape, 0)
        halo_prev = jnp.where(i > 0, pprev_ref[0, BF16_ROWS - 1:BF16_ROWS, :].astype(F32), 0.0)
        halo_next = jnp.where(i < ni - 1, pnext_ref[0, 0:1, :].astype(F32), 0.0)
        p_prev = jnp.where(row == 0, halo_prev, pltpu.roll(p, 1, 0))
        p_next = jnp.where(row == tm - 1, halo_next, pltpu.roll(p, tm - 1, 0))
        cw = cw_ref[...]
        cv = p_prev * cw[0:1, :] + p * cw[1:2, :] + p_next * cw[2:3, :]
        u = (bg_ref[0].astype(F32) * cv).astype(BF16)
        lhs = jnp.concatenate([u] + [o_ref[0, j] for j in range(DIFF_HEADS)], axis=1)
    else:
        x_ref, mod_ref, o_ref, w_ref, g2_ref, rwh_ref, rwl_ref, xn_ref, haug_ref, afft_ref = refs
        lhs = jnp.concatenate([o_ref[0, j] for j in range(o_ref.shape[1])], axis=1)
    m = mod_ref[0]
    xn = x_ref[0] + m[2:3, :] * _dot(lhs, w_ref[...])
    xn_ref[0] = xn
    h = _modulate(xn, g2_ref[...], m[3:4, :], m[4:5, :])
    d = h.shape[1]
    haug_ref[0, :, 0:d] = h
    h_hi = h.astype(BF16)
    h_lo = (h - h_hi.astype(F32)).astype(BF16)
    logits = _dot(h_hi, rwh_ref[...]) + _dot(h_hi, rwl_ref[...]) + _dot(h_lo, rwh_ref[...])
    lane = lax.broadcasted_iota(I32, logits.shape, 1)
    logits = jnp.where(lane < N_EXPERTS, logits, NEG_INF)
    e = jnp.exp(logits - jnp.max(logits, axis=1, keepdims=True))
    aff = e / jnp.sum(e, axis=1, keepdims=True)
    haug_ref[0, :, d:d + AUG] = aff
    afft_ref[0] = aff.T[0:N_EXPERTS, :]


def _outproj(x, mods, mod_row, w_out, n2g, rw_hi, rw_lo, o, conv_args=None):
    b, t, d = x.shape
    tm = min(ROW_TILE, t)
    conv = conv_args is not None
    xmap = lambda bi, i: (bi, i, 0)
    in_specs = [pl.BlockSpec((1, tm, d), xmap),
                pl.BlockSpec((1, N_MOD, d), lambda bi, i: (mod_row(bi), 0, 0))]
    args = [x, mods]
    if conv:
        bg, p, cw = conv_args
        hb = tm // BF16_ROWS
        nhb = t // BF16_ROWS
        in_specs += [pl.BlockSpec((1, tm, CONV_DIM), xmap),
                     pl.BlockSpec((1, tm, CONV_DIM), xmap),
                     pl.BlockSpec((1, BF16_ROWS, CONV_DIM), lambda bi, i: (bi, jnp.maximum(i * hb - 1, 0), 0)),
                     pl.BlockSpec((1, BF16_ROWS, CONV_DIM), lambda bi, i: (bi, jnp.minimum((i + 1) * hb, nhb - 1), 0)),
                     pl.BlockSpec((3, CONV_DIM), lambda bi, i: (0, 0)),
                     pl.BlockSpec((1, DIFF_HEADS, tm, LANES), lambda bi, i: (bi, 0, i, 0))]
        args += [bg, p, p, p, cw, o]
    else:
        in_specs.append(pl.BlockSpec((1, o.shape[1], tm, LANES), lambda bi, i: (bi, 0, i, 0)))
        args.append(o)
    in_specs += [pl.BlockSpec(w_out.shape, lambda bi, i: (0, 0)),
                 pl.BlockSpec((1, d), lambda bi, i: (0, 0)),
                 pl.BlockSpec((d, LANES), lambda bi, i: (0, 0)),
                 pl.BlockSpec((d, LANES), lambda bi, i: (0, 0))]
    args += [w_out, n2g.reshape(1, d), rw_hi, rw_lo]
    return pl.pallas_call(
        functools.partial(_outproj_kernel, conv=conv, tm=tm),
        grid=(b, t // tm),
        in_specs=in_specs,
        out_specs=[pl.BlockSpec((1, tm, d), xmap),
                   pl.BlockSpec((1, tm, d + AUG), xmap),
                   pl.BlockSpec((1, N_EXPERTS, tm), lambda bi, i: (bi, 0, i))],
        out_shape=[jax.ShapeDtypeStruct((b, t, d), F32),
                   jax.ShapeDtypeStruct((b, t, d + AUG), F32),
                   jax.ShapeDtypeStruct((b, N_EXPERTS, t), F32)],
        compiler_params=_cparams(("parallel", "arbitrary")),
        name="outproj",
    )(*args)


def _route_kernel(aff_ref, idx_ref, post_ref, offs_ref, pinc_ref, tally_ref, *, t, cap, capp, lc):
    nch = t // lc
    ne = N_EXPERTS
    aff = aff_ref[0]

    def search(i, lo):
        cand = lo | lax.shift_left(jnp.int32(1), 30 - i)
        cnt = jnp.sum(jnp.where(aff >= pltpu.bitcast(cand, F32), 1.0, 0.0), axis=1, keepdims=True)
        return jnp.where(cnt >= cap, cand, lo)

    thr_bits = lax.fori_loop(0, 31, search, jnp.zeros((ne, 1), I32))
    thr = pltpu.bitcast(thr_bits, F32)
    gt = aff > thr
    eq = aff == thr
    need = cap - jnp.sum(jnp.where(gt, 1.0, 0.0), axis=1, keepdims=True)

    r_i = lax.broadcasted_iota(I32, (lc, lc), 0)
    c_i = lax.broadcasted_iota(I32, (lc, lc), 1)
    upper = jnp.where(r_i < c_i, 1.0, 0.0).astype(BF16)

    def excl_prefix(x):
        outs, offs = [], []
        carry = jnp.zeros((ne, 1), F32)
        for c in range(nch):
            xc = x[:, c * lc:(c + 1) * lc]
            offs.append(carry)
            outs.append(_dot(xc.astype(BF16), upper) + carry)
            carry = carry + jnp.sum(xc, axis=1, keepdims=True)
        offs.append(carry)
        return jnp.concatenate(outs, axis=1), offs

    eqf = jnp.where(eq, 1.0, 0.0)
    eq_rank, _ = excl_prefix(eqf)
    sel = jnp.logical_or(gt, jnp.logical_and(eq, eq_rank < need))
    self_ = jnp.where(sel, 1.0, 0.0)
    pos, offs = excl_prefix(self_)

    lane = lax.broadcasted_iota(I32, (ne, LANES), 1)
    om = jnp.zeros((ne, LANES), F32)
    for c, o in enumerate(offs):
        om = jnp.where(lane == c, o, om)
    offs_ref[0] = om

    posm = jnp.where(sel, pos, -1.0)
    pad = jnp.full((LANES - ne, lc), -1.0, F32)
    for c in range(nch):
        blk = jnp.concatenate([posm[:, c * lc:(c + 1) * lc], pad], axis=0)
        post_ref[0, c * lc:(c + 1) * lc, :] = blk.T
    pinc = pos + self_
    nt = t // LANES
    for k in range(nt):
        pinc_ref[k] = pinc[:, k * LANES:(k + 1) * LANES]

    rows = BF16_ROWS
    sub = lax.broadcasted_iota(I32, (rows, LANES), 0).astype(F32)
    one = jnp.ones((rows, LANES), BF16)
    zero = jnp.zeros((rows, LANES), BF16)
    lane_c = lax.broadcasted_iota(I32, (capp, LANES), 1)
    unroll = min(8, nt)
    assert nt <= 256

    def per_expert(e, carry):
        for jb in range(capp // LANES):
            def tiles(kk, acc, jb=jb):
                for u in range(unroll):
                    rel = pinc_ref[kk * unroll + u, pl.ds(e, 1), :] - float(jb * LANES)
                    q = (jnp.clip(rel, -float(rows), float(LANES + rows)) - sub).astype(BF16)
                    acc = acc + jnp.concatenate(
                        [jnp.where(q <= float(rows * i), one, zero) for i in range(LANES // rows)], axis=0)
                return acc
            tally_ref[e, jb * LANES:(jb + 1) * LANES, :] = lax.fori_loop(
                0, nt // unroll, tiles, jnp.zeros((LANES, LANES), BF16))
        return carry

    lax.fori_loop(0, ne, per_expert, 0)
    idx_m = jnp.zeros((capp, LANES), F32)
    for e in range(ne):
        col = jnp.sum(tally_ref[e].astype(F32), axis=1, keepdims=True)
        idx_m = jnp.where(lane_c == e, col, idx_m)
    base = pl.program_id(0) * t
    idx_ref[0] = idx_m.T[0:ne, :].astype(I32) + base


def _route(aff_t, cap):
    b, ne, t = aff_t.shape
    lc = min(ROUTE_CHUNK, t)
    capp = max(cap, LANES)
    return pl.pallas_call(
        functools.partial(_route_kernel, t=t, cap=cap, capp=capp, lc=lc),
        grid=(b,),
        in_specs=[pl.BlockSpec((1, ne, t), lambda bi: (bi, 0, 0))],
        out_specs=[pl.BlockSpec((1, ne, capp), lambda bi: (bi, 0, 0)),
                   pl.BlockSpec((1, t, LANES), lambda bi: (bi, 0, 0)),
                   pl.BlockSpec((1, ne, LANES), lambda bi: (bi, 0, 0))],
        out_shape=[jax.ShapeDtypeStruct((b, ne, capp), I32),
                   jax.ShapeDtypeStruct((b, t, LANES), F32),
                   jax.ShapeDtypeStruct((b, ne, LANES), F32)],
        scratch_shapes=[pltpu.VMEM((t // LANES, ne, LANES), F32),
                        pltpu.VMEM((ne, capp, LANES), BF16)],
        compiler_params=_cparams(("arbitrary",)),
        name="route",
    )(aff_t)


def _ffn_kernel(idx_ref, haug_ref, wg_ref, wu_ref, wd_ref, y_ref, hbuf, w_in, w_dn, gsem, *, rows, d):
    e = pl.program_id(0)
    step = e * pl.num_programs(1) + pl.program_id(1)
    nsteps = pl.num_programs(0) * pl.num_programs(1)

    @pl.when(pl.program_id(1) == 0)
    def _():
        w_in[0] = wg_ref[0].astype(BF16)
        w_in[1] = wu_ref[0].astype(BF16)
        w_dn[...] = wd_ref[0].astype(BF16)

    def row_copy(src_row, dst_slot, dst_row):
        return pltpu.make_async_copy(haug_ref.at[pl.ds(src_row, 1)],
                                     hbuf.at[dst_slot, pl.ds(dst_row, 1)], gsem.at[dst_slot])

    def wait_rows(sl):
        pltpu.make_async_copy(haug_ref.at[pl.ds(0, rows)], hbuf.at[sl], gsem.at[sl]).wait()

    @pl.when(step == 0)
    def _():
        def body(r, carry):
            row_copy(idx_ref[r], 0, r).start()
            return carry
        lax.fori_loop(0, rows, body, 0)

    def run(slot):
        base = jnp.minimum(step + 1, nsteps - 1) * rows
        for r in range(rows):
            row_copy(idx_ref[base + r], 1 - slot, r).start()

        wait_rows(slot)
        hrow = hbuf[slot]
        hs = hrow[:, 0:d].astype(BF16)
        lane = lax.broadcasted_iota(I32, (rows, AUG), 1)
        gate = jnp.sum(jnp.where(lane == e, hrow[:, d:d + AUG], 0.0), axis=1, keepdims=True)
        a = _dot(hs, w_in[0])
        u = _dot(hs, w_in[1])
        hm = (a * jax.nn.sigmoid(a) * u).astype(BF16)
        y_ref[0] = (_dot(hm, w_dn[...]) * gate).astype(y_ref.dtype)

        @pl.when(step == nsteps - 1)
        def _():
            wait_rows(1 - slot)

    for parity in range(2):
        pl.when(lax.rem(step, 2) == parity)(functools.partial(run, parity))


def _ffn(idx_flat, haug, wg, wu, wd, rows):
    ne, d, f = wg.shape
    n_rows_total = idx_flat.shape[0]
    steps = n_rows_total // (ne * rows)
    grid_spec = pltpu.PrefetchScalarGridSpec(
        num_scalar_prefetch=1,
        grid=(ne, steps),
        in_specs=[pl.BlockSpec(memory_space=pl.ANY),
                  pl.BlockSpec((1, d, f), lambda e, s, idx: (e, 0, 0)),
                  pl.BlockSpec((1, d, f), lambda e, s, idx: (e, 0, 0)),
                  pl.BlockSpec((1, f, d), lambda e, s, idx: (e, 0, 0))],
        out_specs=pl.BlockSpec((1, rows, d), lambda e, s, idx: (e, s, 0)),
        scratch_shapes=[pltpu.VMEM((2, rows, d + AUG), F32),
                        pltpu.VMEM((2, d, f), BF16),
                        pltpu.VMEM((f, d), BF16),
                        pltpu.SemaphoreType.DMA((2,))],
    )
    return pl.pallas_call(
        functools.partial(_ffn_kernel, rows=rows, d=d),
        grid_spec=grid_spec,
        out_shape=jax.ShapeDtypeStruct((ne, steps * rows, d), BF16),
        compiler_params=_cparams(("arbitrary", "arbitrary")),
        name="ffn",
    )(idx_flat, haug, wg, wu, wd)


def _combine_kernel(ws_ref, nr_ref, x_ref, mod_ref, post_ref, fg_ref, y_ref, o_ref, stage, acc, sem,
                    *, cap, tm, final_norm):
    nchunk = pl.num_programs(1)
    step = pl.program_id(0) * nchunk + pl.program_id(1)
    nsteps = pl.num_programs(0) * nchunk
    slot = lax.rem(step, 2)
    ne = N_EXPERTS
    w = COMB_W
    per = LANES // w
    total_rows = y_ref.shape[1]
    lane_row = lax.broadcasted_iota(I32, (1, LANES), 1)

    def window(st, e, r):
        sample = lax.div(st, nchunk)
        nominal = ws_ref[st * ne + e] + r * w
        grow = pl.multiple_of(jnp.minimum(sample * cap + nominal, total_rows - w), BF16_ROWS)
        return nominal, grow, grow - sample * cap

    def copy(e, grow, sl, si):
        return pltpu.make_async_copy(y_ref.at[e, pl.ds(grow, w)], stage.at[sl, pl.ds(e * w, w)], sem.at[si])

    def fetch(st, r, sl, si):
        for e in range(ne):
            copy(e, window(st, e, r)[1], sl, si).start()

    def wait(sl, si):
        for e in range(ne):
            copy(e, 0, sl, si).wait()

    def gathered(st, r, sl):
        pt = post_ref[0]
        blocks = []
        for g in range(ne // per):
            tgt = jnp.zeros((1, LANES), F32)
            mine = jnp.zeros((tm, LANES), F32)
            for k in range(per):
                e = g * per + k
                nominal, _, first = window(st, e, r)
                inb = jnp.logical_and(lane_row >= k * w, lane_row < (k + 1) * w)
                slot = first - k * w + lane_row
                tgt = jnp.where(inb, jnp.where(slot >= nominal, slot, -2).astype(F32), tgt)
                mine = jnp.where(inb, pt[:, e:e + 1], mine)
            blocks.append(jnp.where(mine == tgt, 1.0, 0.0).astype(BF16))
        onehot = jnp.concatenate(blocks, axis=1)
        return _dot(onehot, stage[sl])

    @pl.when(step == 0)
    def _():
        fetch(0, 0, 0, 0)

    @pl.when(step + 1 < nsteps)
    def _():
        fetch(step + 1, 0, 1 - slot, 1 - slot)

    wait(slot, slot)
    acc[...] = gathered(step, 0, slot)

    def extra_round(r, carry):
        fetch(step, r, slot, 2)
        wait(slot, 2)
        acc[...] += gathered(step, r, slot)
        return carry

    lax.fori_loop(1, nr_ref[step], extra_round, 0)
    m = mod_ref[0]
    xn = x_ref[0] + m[5:6, :] * acc[...]
    if final_norm:
        ms = jnp.mean(xn * xn, axis=-1, keepdims=True)
        xn = xn * lax.rsqrt(ms + NORM_EPS) * fg_ref[...]
    o_ref[0] = xn


def _combine(ws_flat, nr_flat, x, mods, mod_row, post, y, cap, final_g=None):
    b, t, d = x.shape
    tm = min(ROW_TILE, t)
    ne = N_EXPERTS
    final_norm = final_g is not None
    fg = (final_g if final_norm else jnp.ones((d,), F32)).reshape(1, d)
    grid_spec = pltpu.PrefetchScalarGridSpec(
        num_scalar_prefetch=2,
        grid=(b, t // tm),
        in_specs=[pl.BlockSpec((1, tm, d), lambda bi, i, a, c: (bi, i, 0)),
                  pl.BlockSpec((1, N_MOD, d), lambda bi, i, a, c: (mod_row(bi), 0, 0)),
                  pl.BlockSpec((1, tm, LANES), lambda bi, i, a, c: (bi, i, 0)),
                  pl.BlockSpec((1, d), lambda bi, i, a, c: (0, 0)),
                  pl.BlockSpec(memory_space=pl.ANY)],
        out_specs=pl.BlockSpec((1, tm, d), lambda bi, i, a, c: (bi, i, 0)),
        scratch_shapes=[pltpu.VMEM((2, ne * COMB_W, d), BF16),
                        pltpu.VMEM((tm, d), F32),
                        pltpu.SemaphoreType.DMA((3,))],
    )
    return pl.pallas_call(
        functools.partial(_combine_kernel, cap=cap, tm=tm, final_norm=final_norm),
        grid_spec=grid_spec,
        out_shape=jax.ShapeDtypeStruct((b, t, d), F32),
        compiler_params=_cparams(("arbitrary", "arbitrary")),
        name="combine",
    )(ws_flat, nr_flat, x, mods, post, fg, y)


def _moe(xn, haug, aff_t, mods, mod_row, wg, wu, wd, rows_per_step, final_g=None):
    b, t, d = xn.shape
    ne = N_EXPERTS
    cap = CAPACITY_FACTOR * t // ne
    idx, post, offs = _route(aff_t, cap)
    idx_flat = jnp.transpose(idx[:, :, :cap], (1, 0, 2)).reshape(-1)
    y = _ffn(idx_flat, haug.reshape(b * t, d + AUG), wg, wu, wd, rows_per_step)
    tm = min(ROW_TILE, t)
    nchunk = t // tm
    lc = min(ROUTE_CHUNK, t)
    per = tm // lc
    offs_i = offs.astype(I32)[:, :, 0:t // lc + 1:per]
    start = jnp.transpose(offs_i[:, :, :nchunk], (0, 2, 1))
    end = jnp.transpose(offs_i[:, :, 1:], (0, 2, 1))
    ws = (start // BF16_ROWS) * BF16_ROWS
    nr = jnp.maximum(jnp.max((end - ws + COMB_W - 1) // COMB_W, axis=2), 1)
    return _combine(ws.reshape(-1), nr.reshape(-1), xn, mods, mod_row, post, y, cap, final_g)


def _split_hi_lo(w):
    hi = w.astype(BF16)
    lo = (w - hi.astype(F32)).astype(BF16)
    return hi, lo


def _router_weights(rw):
    d, ne = rw.shape
    pad = jnp.zeros((d, LANES - ne), F32)
    return _split_hi_lo(jnp.concatenate([rw, pad], axis=1))


def kernel(x, c, ctx, c_ctx, l0_ada_w, l0_ada_b, l0_norm1_g, l0_w_in, l0_conv_w, l0_lambda_q1, l0_lambda_k1, l0_lambda_q2, l0_lambda_k2, l0_subln_g, l0_w_out, l0_norm2_g, l0_router_w, l0_exp_w_gate, l0_exp_w_up, l0_exp_w_down, l1_ada_w, l1_ada_b, l1_norm1_g, l1_w_qkv, l1_sink, l1_w_out, l1_norm2_g, l1_router_w, l1_exp_w_gate, l1_exp_w_up, l1_exp_w_down, final_norm_g):
    b, t, d = x.shape
    nctx = ctx.shape[1]
    assert b + 1 <= MOD_ROWS
    lat_row = lambda bi: bi
    ctx_row = lambda bi: b

    cc = jnp.zeros((MOD_ROWS, d), F32).at[:b].set(c).at[b].set(c_ctx)
    mods0 = _ada(cc, l0_ada_w, l0_ada_b).reshape(MOD_ROWS, N_MOD, d)
    mods1 = _ada(cc, l1_ada_w, l1_ada_b).reshape(MOD_ROWS, N_MOD, d)
    tables = _rope_tables(t)
    qscale = LOG2E * DIFF_HEAD_DIM ** -0.5

    w_in = l0_w_in.astype(BF16)
    cd = CONV_DIM
    qkw = DIFF_HEADS * 2 * DIFF_HEAD_DIM
    wvt = jnp.transpose(l0_w_in[:, 3 * cd + 2 * qkw:]).astype(BF16)
    outs0 = [("plain", 0, cd, False, 1.0),
             ("mul", cd, cd, False, 1.0),
             ("heads", 3 * cd, qkw, True, qscale),
             ("heads", 3 * cd + qkw, qkw, True, 1.0)]
    w_main = w_in[:, :3 * cd + 2 * qkw]
    bg_l, cx_l, q_l, k_l, vt_l = _proj(x, mods0, lat_row, l0_norm1_g, w_main, outs0, tables, wvt)
    bg_c, cx_c, q_c, k_c, vt_c = _proj(ctx, mods0, ctx_row, l0_norm1_g, w_main, outs0, None, wvt)
    lams = (l0_lambda_q1, l0_lambda_k1, l0_lambda_q2, l0_lambda_k2)
    o_l = _diff_attn(q_l, [k_c, k_l], [vt_c, vt_l], lams, l0_subln_g)
    o_c = _diff_attn(q_c, [k_c], [vt_c], lams, l0_subln_g)

    w_out0 = l0_w_out.astype(BF16)
    rw0 = _router_weights(l0_router_w)
    wg0, wu0, wd0 = l0_exp_w_gate, l0_exp_w_up, l0_exp_w_down
    x1, haug, aff_t = _outproj(x, mods0, lat_row, w_out0, l0_norm2_g, *rw0, o_l,
                               conv_args=(bg_l, cx_l, l0_conv_w))
    x = _moe(x1, haug, aff_t, mods0, lat_row, wg0, wu0, wd0, CAPACITY_FACTOR * t // N_EXPERTS)
    c1, haug_c, aff_tc = _outproj(ctx, mods0, ctx_row, w_out0, l0_norm2_g, *rw0, o_c,
                                  conv_args=(bg_c, cx_c, l0_conv_w))
    ctx = _moe(c1, haug_c, aff_tc, mods0, ctx_row, wg0, wu0, wd0, b * (CAPACITY_FACTOR * nctx // N_EXPERTS))

    qw = SWA_HEADS * SWA_HEAD_DIM
    kvw = SWA_KV_HEADS * SWA_HEAD_DIM
    wq = l1_w_qkv[:, :qw]
    wk = l1_w_qkv[:, qw:qw + kvw].reshape(d, SWA_KV_HEADS, SWA_HEAD_DIM)
    wv = l1_w_qkv[:, qw + kvw:].reshape(d, SWA_KV_HEADS, SWA_HEAD_DIM)
    wk2 = jnp.concatenate([wk, wk], axis=2).reshape(d, 2 * kvw)
    wv2 = jnp.concatenate([wv, jnp.zeros_like(wv)], axis=2).reshape(d, 2 * kvw)
    w_qk = jnp.concatenate([wq, wk2], axis=1).astype(BF16)
    wvt1 = jnp.transpose(wv2).astype(BF16)
    outs1 = [("heads", 0, qw, True, LOG2E * SWA_HEAD_DIM ** -0.5),
             ("heads", qw, 2 * kvw, True, 1.0)]
    q1, k1, vt1 = _proj(x, mods1, lat_row, l1_norm1_g, w_qk, outs1, tables, wvt1)
    k1c, vt1c = _proj(ctx, mods1, ctx_row, l1_norm1_g, w_qk[:, qw:], [("heads", 0, 2 * kvw, False, 1.0)], None, wvt1)
    o1 = _swa_attn(l1_sink, q1, k1, vt1, k1c, vt1c)

    rw1 = _router_weights(l1_router_w)
    wg1, wu1, wd1 = l1_exp_w_gate, l1_exp_w_up, l1_exp_w_down
    x1, haug, aff_t = _outproj(x, mods1, lat_row, l1_w_out.astype(BF16), l1_norm2_g, *rw1, o1)
    return _moe(x1, haug, aff_t, mods1, lat_row, wg1, wu1, wd1, CAPACITY_FACTOR * t // N_EXPERTS,
                final_g=final_norm_g)
```

```python
import functools
import math

import jax
import jax.numpy as jnp
from jax import lax
from jax.experimental import pallas as pl
from jax.experimental.pallas import tpu as pltpu

F32 = jnp.float32
BF16 = jnp.bfloat16
I32 = jnp.int32

GRID_W = 64
ROPE_BASE = 10000.0
NORM_EPS = 1e-6
NEG_INF = -1e30
N_MOD = 6

CONV_DIM = 512
DIFF_HEADS = 4
DIFF_HEAD_DIM = 64
DIFF_V_DIM = 128
LAM_INIT0 = 0.8 - 0.6 * math.exp(-0.3 * 0)

SWA_HEADS = 16
SWA_KV_HEADS = 4
SWA_HEAD_DIM = 64
SWA_WINDOW = 128

N_EXPERTS = 16
CAPACITY_FACTOR = 2

LANES = 128
SUBLANES = 8
MOD_ROWS = 16
AUG = LANES
ROW_TILE = 256
PROJ_TILE = 1024
ADA_TILE = 512
FFN_SAMPLES_PER_STEP = 2
ROUTE_CHUNK = 256
COMB_W = 64
BF16_ROWS = 16
VT_ROWS = DIFF_V_DIM
SWA_KV_PER_STEP = 4
LOG2E = math.log2(math.e)
VMEM_LIMIT = 56 * 1024 * 1024


def _cparams(sem):
    return pltpu.CompilerParams(dimension_semantics=sem, vmem_limit_bytes=VMEM_LIMIT)


def _dot(a, b):
    return jnp.dot(a, b, preferred_element_type=F32)


def _dot_nt(a, b):
    return lax.dot_general(a, b, (((1,), (1,)), ((), ())), preferred_element_type=F32)


def _modulate(x, gain, shift, scale):
    ms = jnp.mean(x * x, axis=-1, keepdims=True)
    return x * lax.rsqrt(ms + NORM_EPS) * (gain * (1.0 + scale)) + shift


def _ada_kernel(c_ref, w_ref, b_ref, o_ref):
    c = c_ref[...]
    s = (c * jax.nn.sigmoid(c)).astype(BF16)
    o_ref[...] = _dot(s, w_ref[...].astype(BF16)) + b_ref[...]


def _ada(cc, w, b):
    d, n = w.shape
    tn = ADA_TILE
    return pl.pallas_call(
        _ada_kernel,
        grid=(n // tn,),
        in_specs=[pl.BlockSpec((MOD_ROWS, d), lambda j: (0, 0)),
                  pl.BlockSpec((d, tn), lambda j: (0, j)),
                  pl.BlockSpec((1, tn), lambda j: (0, j))],
        out_specs=pl.BlockSpec((MOD_ROWS, tn), lambda j: (0, j)),
        out_shape=jax.ShapeDtypeStruct((MOD_ROWS, n), F32),
        compiler_params=_cparams(("arbitrary",)),
        name="ada",
    )(cc, w, b.reshape(1, n))


def _rope_tables(t):
    n_freq = DIFF_HEAD_DIM // 4
    inv_freq = ROPE_BASE ** (-jnp.arange(n_freq, dtype=F32) / n_freq)
    pos = jnp.arange(t, dtype=I32)
    row = (pos // GRID_W).astype(F32)[:, None] * inv_freq
    col = (pos % GRID_W).astype(F32)[:, None] * inv_freq
    zeros = jnp.zeros_like(row)
    cos64 = jnp.concatenate([jnp.cos(row), jnp.cos(row), jnp.cos(col), jnp.cos(col)], axis=1)
    sa64 = jnp.concatenate([-jnp.sin(row), zeros, -jnp.sin(col), zeros], axis=1)
    sb64 = jnp.concatenate([zeros, jnp.sin(row), zeros, jnp.sin(col)], axis=1)
    rep = lambda a: jnp.concatenate([a, a], axis=1)
    return rep(cos64), rep(sa64), rep(sb64)


def _rope128(c, cos, sa, sb):
    return c * cos + pltpu.roll(c, LANES - 16, 1) * sa + pltpu.roll(c, 16, 1) * sb


def _proj_kernel(*refs, outs, rope, has_vt):
    x_ref, mod_ref, g_ref, w_ref = refs[:4]
    pos = 4
    if rope:
        cos_ref, sa_ref, sb_ref = refs[pos:pos + 3]
        pos += 3
    if has_vt:
        wvt_ref = refs[pos]
        pos += 1
    out_refs = refs[pos:]
    m = mod_ref[0]
    h = _modulate(x_ref[0], g_ref[...], m[0:1, :], m[1:2, :]).astype(BF16)
    y = _dot(h, w_ref[...])
    for (kind, c0, width, do_rope, scale), o_ref in zip(outs, out_refs):
        for j in range(width // LANES):
            c = y[:, c0 + j * LANES:c0 + (j + 1) * LANES]
            if kind == "mul":
                c = c * y[:, c0 + width + j * LANES:c0 + width + (j + 1) * LANES]
            if do_rope and rope:
                c = _rope128(c, cos_ref[...], sa_ref[...], sb_ref[...])
            if scale != 1.0:
                c = c * scale
            if kind == "heads":
                o_ref[0, j] = c.astype(o_ref.dtype)
            else:
                o_ref[0, :, j * LANES:(j + 1) * LANES] = c.astype(o_ref.dtype)
    if has_vt:
        vt = _dot_nt(wvt_ref[...], h)
        o_ref = out_refs[len(outs)]
        for j in range(vt.shape[0] // LANES):
            o_ref[0, j] = vt[j * LANES:(j + 1) * LANES, :].astype(o_ref.dtype)


def _proj(x, mods, mod_row, gain, w, outs, tables=None, wvt=None):
    b, t, d = x.shape
    tm = min(PROJ_TILE, t)
    n = w.shape[1]
    rope = tables is not None
    in_specs = [pl.BlockSpec((1, tm, d), lambda bi, i: (bi, i, 0)),
                pl.BlockSpec((1, N_MOD, d), lambda bi, i: (mod_row(bi), 0, 0)),
                pl.BlockSpec((1, d), lambda bi, i: (0, 0)),
                pl.BlockSpec((d, n), lambda bi, i: (0, 0))]
    args = [x, mods, gain.reshape(1, d), w]
    if rope:
        in_specs += [pl.BlockSpec((tm, LANES), lambda bi, i: (i, 0))] * 3
        args += list(tables)
    if wvt is not None:
        in_specs.append(pl.BlockSpec(wvt.shape, lambda bi, i: (0, 0)))
        args.append(wvt)
    out_specs, out_shapes = [], []
    for (kind, c0, width, do_rope, scale) in outs:
        if kind == "heads":
            nh = width // LANES
            out_specs.append(pl.BlockSpec((1, nh, tm, LANES), lambda bi, i: (bi, 0, i, 0)))
            out_shapes.append(jax.ShapeDtypeStruct((b, nh, t, LANES), BF16))
        else:
            out_specs.append(pl.BlockSpec((1, tm, width), lambda bi, i: (bi, i, 0)))
            out_shapes.append(jax.ShapeDtypeStruct((b, t, width), BF16))
    if wvt is not None:
        nh = wvt.shape[0] // LANES
        out_specs.append(pl.BlockSpec((1, nh, VT_ROWS, tm), lambda bi, i: (bi, 0, 0, i)))
        out_shapes.append(jax.ShapeDtypeStruct((b, nh, VT_ROWS, t), BF16))
    return pl.pallas_call(
        functools.partial(_proj_kernel, outs=tuple(outs), rope=rope, has_vt=wvt is not None),
        grid=(b, t // tm),
        in_specs=in_specs,
        out_specs=out_specs,
        out_shape=out_shapes,
        compiler_params=_cparams(("parallel", "arbitrary")),
        name="proj",
    )(*args)


def _diff_attn_kernel(*refs, nseg):
    q_ref = refs[0]
    k_refs = refs[1:1 + nseg]
    vt_refs = refs[1 + nseg:1 + 2 * nseg]
    lq1, lk1, lq2, lk2, g_ref, o_ref, s_even, m_even, s_odd, m_odd = refs[1 + 2 * nseg:]
    i = pl.program_id(1)
    offs = [0]
    for k in k_refs:
        offs.append(offs[-1] + k.shape[2])

    @pl.when(i == 0)
    def _():
        s_odd[...] = jnp.zeros(s_odd.shape, F32)
        m_odd[...] = jnp.zeros(m_odd.shape, F32)

    def body(s_w, m_w, s_r, m_r):
        q = q_ref[0, 0]
        lane = lax.broadcasted_iota(I32, q.shape, 1)
        zero = jnp.zeros_like(q)
        qs = (jnp.where(lane < DIFF_HEAD_DIM, q, zero), jnp.where(lane >= DIFF_HEAD_DIM, q, zero))
        for half, qh in enumerate(qs):
            m = None
            for si, k in enumerate(k_refs):
                s = _dot_nt(k[0, 0], qh)
                s_w[half, offs[si]:offs[si + 1], :] = s
                ms = jnp.max(s, axis=0, keepdims=True)
                m = ms if m is None else jnp.maximum(m, ms)
            m_w[half] = m

        lam = (jnp.exp(jnp.sum(lq1[...] * lk1[...], axis=1, keepdims=True))
               - jnp.exp(jnp.sum(lq2[...] * lk2[...], axis=1, keepdims=True)) + LAM_INIT0)
        probs, norms = [], []
        for half in range(2):
            p = [jnp.exp2(s_r[half, offs[si]:offs[si + 1], :] - m_r[half]) for si in range(nseg)]
            norms.append(functools.reduce(jnp.add, [jnp.sum(x, axis=0, keepdims=True) for x in p]))
            probs.append(p)
        c2 = lam * norms[0] / norms[1]
        ot = None
        for si in range(nseg):
            a = (probs[0][si] - probs[1][si] * c2).astype(BF16)
            part = _dot(vt_refs[si][0, 0], a)
            ot = part if ot is None else ot + part
        ot = ot * (1.0 / norms[0])
        msq = jnp.mean(ot * ot, axis=0, keepdims=True)
        on = ot * lax.rsqrt(msq + NORM_EPS) * (g_ref[...] * (1.0 - LAM_INIT0))
        o_ref[0, 0] = on.T.astype(o_ref.dtype)

    @pl.when(lax.rem(i, 2) == 0)
    def _():
        body(s_even, m_even, s_odd, m_odd)

    @pl.when(lax.rem(i, 2) == 1)
    def _():
        body(s_odd, m_odd, s_even, m_even)


def _diff_attn(q, ks, vts, lams, subln_g):
    b, nh, t, _ = q.shape
    tq = min(ROW_TILE, t)
    nq = t // tq
    nseg = len(ks)
    tk = sum(k.shape[2] for k in ks)
    ntiles = nh * nq

    def tile(s, lag):
        ts = jnp.clip(s - lag, 0, ntiles - 1)
        return ts // nq, ts % nq

    in_specs = [pl.BlockSpec((1, 1, tq, LANES), lambda bi, s: (bi, *tile(s, 0), 0))]
    for k in ks:
        in_specs.append(pl.BlockSpec((1, 1, k.shape[2], LANES), lambda bi, s: (bi, tile(s, 0)[0], 0, 0)))
    for vt in vts:
        in_specs.append(pl.BlockSpec((1, 1, VT_ROWS, vt.shape[3]), lambda bi, s: (bi, tile(s, 1)[0], 0, 0)))
    in_specs += [pl.BlockSpec((1, DIFF_HEAD_DIM), lambda bi, s: (0, 0))] * 4
    in_specs.append(pl.BlockSpec((DIFF_V_DIM, 1), lambda bi, s: (0, 0)))
    return pl.pallas_call(
        functools.partial(_diff_attn_kernel, nseg=nseg),
        grid=(b, ntiles + 1),
        in_specs=in_specs,
        out_specs=pl.BlockSpec((1, 1, tq, LANES), lambda bi, s: (bi, *tile(s, 1), 0)),
        out_shape=jax.ShapeDtypeStruct((b, nh, t, LANES), BF16),
        scratch_shapes=[pltpu.VMEM((2, tk, tq), F32), pltpu.VMEM((2, 1, tq), F32)] * 2,
        compiler_params=_cparams(("parallel", "arbitrary")),
        name="diff_attn",
    )(q, *ks, *vts, *[l.reshape(1, DIFF_HEAD_DIM) for l in lams], subln_g.reshape(DIFF_V_DIM, 1))


def _swa_kernel(*refs, t, tq, nq, nwin, kvps):
    sink_ref, q_ref = refs[0], refs[1]
    kw_refs = refs[2:2 + nwin]
    kc_ref = refs[2 + nwin]
    vw_refs = refs[3 + nwin:3 + 2 * nwin]
    vc_ref, o_ref, s_even, m_even, s_odd, m_odd = refs[3 + 2 * nwin:]
    n = pl.program_id(1)
    i = pl.program_id(2)
    c = kc_ref.shape[2]
    span = nwin * LANES
    group = SWA_HEADS // SWA_KV_HEADS

    @pl.when(i == 0)
    def _():
        s_odd[...] = jnp.zeros(s_odd.shape, F32)
        m_odd[...] = jnp.zeros(m_odd.shape, F32)

    def body(s_w, m_w, s_r, m_r):
        ti = jnp.minimum(i, nq - 1)
        kpos = (ti * tq - SWA_WINDOW) + lax.broadcasted_iota(I32, (span, tq), 0)
        qpos = ti * tq + lax.broadcasted_iota(I32, (span, tq), 1)
        ok = jnp.logical_and(jnp.logical_and(kpos >= 0, kpos < t), jnp.abs(qpos - kpos) <= SWA_WINDOW)
        bias = jnp.where(ok, 0.0, NEG_INF)
        lane = lax.broadcasted_iota(I32, (tq, LANES), 1)
        for kk in range(kvps):
            kwin = jnp.concatenate([r[0, kk] for r in kw_refs], axis=0)
            kc = kc_ref[0, kk]
            for j in range(group // 2):
                qc = q_ref[0, kk * (group // 2) + j]
                zero = jnp.zeros_like(qc)
                for half in range(2):
                    hh = kk * group + 2 * j + half
                    keep = (lane < SWA_HEAD_DIM) if half == 0 else (lane >= SWA_HEAD_DIM)
                    qz = jnp.where(keep, qc, zero)
                    s_c = _dot_nt(kc, qz)
                    s_l = _dot_nt(kwin, qz) + bias
                    sink = sink_ref[n * kvps * group + hh] * LOG2E
                    s_w[hh, 0:c, :] = s_c
                    s_w[hh, c:c + span, :] = s_l
                    m_w[hh] = jnp.maximum(jnp.maximum(jnp.max(s_c, axis=0, keepdims=True),
                                                      jnp.max(s_l, axis=0, keepdims=True)), sink)

        for kk in range(kvps):
            vwin = jnp.concatenate([r[0, kk] for r in vw_refs], axis=1)
            vc = vc_ref[0, kk]
            vwin = jnp.where(lax.broadcasted_iota(I32, vwin.shape, 0) == SWA_HEAD_DIM, jnp.ones_like(vwin), vwin)
            vc = jnp.where(lax.broadcasted_iota(I32, vc.shape, 0) == SWA_HEAD_DIM, jnp.ones_like(vc), vc)
            for j in range(group // 2):
                res = []
                for half in range(2):
                    hh = kk * group + 2 * j + half
                    m = m_r[hh]
                    p_c = jnp.exp2(s_r[hh, 0:c, :] - m).astype(BF16)
                    p_l = jnp.exp2(s_r[hh, c:c + span, :] - m).astype(BF16)
                    ot = _dot(vc, p_c) + _dot(vwin, p_l)
                    sink = sink_ref[n * kvps * group + hh] * LOG2E
                    den = ot[SWA_HEAD_DIM:SWA_HEAD_DIM + 1, :] + jnp.exp2(sink - m)
                    res.append(ot[0:SWA_HEAD_DIM, :] * (1.0 / den))
                o_ref[0, kk * (group // 2) + j] = jnp.concatenate(res, axis=0).T.astype(o_ref.dtype)

    @pl.when(lax.rem(i, 2) == 0)
    def _():
        body(s_even, m_even, s_odd, m_odd)

    @pl.when(lax.rem(i, 2) == 1)
    def _():
        body(s_odd, m_odd, s_even, m_even)


def _swa_attn(sink, q, kl, vtl, kc, vtc):
    b, nchunk, t, _ = q.shape
    nkv = kl.shape[1]
    cpk = nchunk // nkv
    c = kc.shape[2]
    tq = ROW_TILE
    nq = t // tq
    nblk = t // LANES
    bpt = tq // LANES
    nwin = bpt + 2 * (SWA_WINDOW // LANES)
    kvps = SWA_KV_PER_STEP

    def kblk(i, j, lag):
        ti = jnp.clip(i - lag, 0, nq - 1)
        return jnp.clip(ti * bpt - SWA_WINDOW // LANES + j, 0, nblk - 1)

    in_specs = [pl.BlockSpec(memory_space=pltpu.SMEM),
                pl.BlockSpec((1, kvps * cpk, tq, LANES), lambda bi, n, i: (bi, n, jnp.minimum(i, nq - 1), 0))]
    for j in range(nwin):
        in_specs.append(pl.BlockSpec((1, kvps, LANES, LANES), lambda bi, n, i, j=j: (bi, n, kblk(i, j, 0), 0)))
    in_specs.append(pl.BlockSpec((1, kvps, c, LANES), lambda bi, n, i: (bi, n, 0, 0)))
    for j in range(nwin):
        in_specs.append(pl.BlockSpec((1, kvps, LANES, LANES), lambda bi, n, i, j=j: (bi, n, 0, kblk(i, j, 1))))
    in_specs.append(pl.BlockSpec((1, kvps, LANES, c), lambda bi, n, i: (bi, n, 0, 0)))
    heads = kvps * SWA_HEADS // nkv
    scratch = [pltpu.VMEM((heads, c + nwin * LANES, tq), F32), pltpu.VMEM((heads, 1, tq), F32)] * 2
    return pl.pallas_call(
        functools.partial(_swa_kernel, t=t, tq=tq, nq=nq, nwin=nwin, kvps=kvps),
        grid=(b, nkv // kvps, nq + 1),
        in_specs=in_specs,
        out_specs=pl.BlockSpec((1, kvps * cpk, tq, LANES), lambda bi, n, i: (bi, n, jnp.maximum(i - 1, 0), 0)),
        out_shape=jax.ShapeDtypeStruct((b, nchunk, t, LANES), BF16),
        scratch_shapes=scratch,
        compiler_params=_cparams(("parallel", "arbitrary", "arbitrary")),
        name="swa_attn",
    )(sink, q, *([kl] * nwin), kc, *([vtl] * nwin), vtc)


def _outproj_kernel(*refs, conv, tm):
    if conv:
        (x_ref, mod_ref, bg_ref, p_ref, pprev_ref, pnext_ref, cw_ref, o_ref, w_ref,
         g2_ref, rwh_ref, rwl_ref, xn_ref, haug_ref, afft_ref) = refs
        i = pl.program_id(1)
        ni = pl.num_programs(1)
        p = p_ref[0].astype(F32)
        row = lax.broadcasted_iota(I32, p.shape, 0)
        halo_prev = jnp.where(i > 0, pprev_ref[0, BF16_ROWS - 1:BF16_ROWS, :].astype(F32), 0.0)
        halo_next = jnp.where(i < ni - 1, pnext_ref[0, 0:1, :].astype(F32), 0.0)
        p_prev = jnp.where(row == 0, halo_prev, pltpu.roll(p, 1, 0))
        p_next = jnp.where(row == tm - 1, halo_next, pltpu.roll(p, tm - 1, 0))
        cw = cw_ref[...]
        cv = p_prev * cw[0:1, :] + p * cw[1:2, :] + p_next * cw[2:3, :]
        u = (bg_ref[0].astype(F32) * cv).astype(BF16)
        lhs = jnp.concatenate([u] + [o_ref[0, j] for j in range(DIFF_HEADS)], axis=1)
    else:
        x_ref, mod_ref, o_ref, w_ref, g2_ref, rwh_ref, rwl_ref, xn_ref, haug_ref, afft_ref = refs
        lhs = jnp.concatenate([o_ref[0, j] for j in range(o_ref.shape[1])], axis=1)
    m = mod_ref[0]
    xn = x_ref[0] + m[2:3, :] * _dot(lhs, w_ref[...])
    xn_ref[0] = xn
    h = _modulate(xn, g2_ref[...], m[3:4, :], m[4:5, :])
    d = h.shape[1]
    haug_ref[0, :, 0:d] = h
    h_hi = h.astype(BF16)
    h_lo = (h - h_hi.astype(F32)).astype(BF16)
    logits = _dot(h_hi, rwh_ref[...]) + _dot(h_hi, rwl_ref[...]) + _dot(h_lo, rwh_ref[...])
    lane = lax.broadcasted_iota(I32, logits.shape, 1)
    logits = jnp.where(lane < N_EXPERTS, logits, NEG_INF)
    e = jnp.exp(logits - jnp.max(logits, axis=1, keepdims=True))
    aff = e / jnp.sum(e, axis=1, keepdims=True)
    haug_ref[0, :, d:d + AUG] = aff
    afft_ref[0] = aff.T[0:N_EXPERTS, :]


def _outproj(x, mods, mod_row, w_out, n2g, rw_hi, rw_lo, o, conv_args=None):
    b, t, d = x.shape
    tm = min(ROW_TILE, t)
    conv = conv_args is not None
    xmap = lambda bi, i: (bi, i, 0)
    in_specs = [pl.BlockSpec((1, tm, d), xmap),
                pl.BlockSpec((1, N_MOD, d), lambda bi, i: (mod_row(bi), 0, 0))]
    args = [x, mods]
    if conv:
        bg, p, cw = conv_args
        hb = tm // BF16_ROWS
        nhb = t // BF16_ROWS
        in_specs += [pl.BlockSpec((1, tm, CONV_DIM), xmap),
                     pl.BlockSpec((1, tm, CONV_DIM), xmap),
                     pl.BlockSpec((1, BF16_ROWS, CONV_DIM), lambda bi, i: (bi, jnp.maximum(i * hb - 1, 0), 0)),
                     pl.BlockSpec((1, BF16_ROWS, CONV_DIM), lambda bi, i: (bi, jnp.minimum((i + 1) * hb, nhb - 1), 0)),
                     pl.BlockSpec((3, CONV_DIM), lambda bi, i: (0, 0)),
                     pl.BlockSpec((1, DIFF_HEADS, tm, LANES), lambda bi, i: (bi, 0, i, 0))]
        args += [bg, p, p, p, cw, o]
    else:
        in_specs.append(pl.BlockSpec((1, o.shape[1], tm, LANES), lambda bi, i: (bi, 0, i, 0)))
        args.append(o)
    in_specs += [pl.BlockSpec(w_out.shape, lambda bi, i: (0, 0)),
                 pl.BlockSpec((1, d), lambda bi, i: (0, 0)),
                 pl.BlockSpec((d, LANES), lambda bi, i: (0, 0)),
                 pl.BlockSpec((d, LANES), lambda bi, i: (0, 0))]
    args += [w_out, n2g.reshape(1, d), rw_hi, rw_lo]
    return pl.pallas_call(
        functools.partial(_outproj_kernel, conv=conv, tm=tm),
        grid=(b, t // tm),
        in_specs=in_specs,
        out_specs=[pl.BlockSpec((1, tm, d), xmap),
                   pl.BlockSpec((1, tm, d + AUG), xmap),
                   pl.BlockSpec((1, N_EXPERTS, tm), lambda bi, i: (bi, 0, i))],
        out_shape=[jax.ShapeDtypeStruct((b, t, d), F32),
                   jax.ShapeDtypeStruct((b, t, d + AUG), F32),
                   jax.ShapeDtypeStruct((b, N_EXPERTS, t), F32)],
        compiler_params=_cparams(("parallel", "arbitrary")),
        name="outproj",
    )(*args)


def _route_kernel(aff_ref, idx_ref, post_ref, offs_ref, pinc_ref, tally_ref, *, t, cap, capp, lc):
    nch = t // lc
    ne = N_EXPERTS
    aff = aff_ref[0]

    def search(i, lo):
        cand = lo | lax.shift_left(jnp.int32(1), 30 - i)
        cnt = jnp.sum(jnp.where(aff >= pltpu.bitcast(cand, F32), 1.0, 0.0), axis=1, keepdims=True)
        return jnp.where(cnt >= cap, cand, lo)

    thr_bits = lax.fori_loop(0, 31, search, jnp.zeros((ne, 1), I32))
    thr = pltpu.bitcast(thr_bits, F32)
    gt = aff > thr
    eq = aff == thr
    need = cap - jnp.sum(jnp.where(gt, 1.0, 0.0), axis=1, keepdims=True)

    r_i = lax.broadcasted_iota(I32, (lc, lc), 0)
    c_i = lax.broadcasted_iota(I32, (lc, lc), 1)
    upper = jnp.where(r_i < c_i, 1.0, 0.0).astype(BF16)

    def excl_prefix(x):
        outs, offs = [], []
        carry = jnp.zeros((ne, 1), F32)
        for c in range(nch):
            xc = x[:, c * lc:(c + 1) * lc]
            offs.append(carry)
            outs.append(_dot(xc.astype(BF16), upper) + carry)
            carry = carry + jnp.sum(xc, axis=1, keepdims=True)
        offs.append(carry)
        return jnp.concatenate(outs, axis=1), offs

    eqf = jnp.where(eq, 1.0, 0.0)
    eq_rank, _ = excl_prefix(eqf)
    sel = jnp.logical_or(gt, jnp.logical_and(eq, eq_rank < need))
    self_ = jnp.where(sel, 1.0, 0.0)
    pos, offs = excl_prefix(self_)

    lane = lax.broadcasted_iota(I32, (ne, LANES), 1)
    om = jnp.zeros((ne, LANES), F32)
    for c, o in enumerate(offs):
        om = jnp.where(lane == c, o, om)
    offs_ref[0] = om

    posm = jnp.where(sel, pos, -1.0)
    pad = jnp.full((LANES - ne, lc), -1.0, F32)
    for c in range(nch):
        blk = jnp.concatenate([posm[:, c * lc:(c + 1) * lc], pad], axis=0)
        post_ref[0, c * lc:(c + 1) * lc, :] = blk.T
    pinc = pos + self_
    nt = t // LANES
    for k in range(nt):
        pinc_ref[k] = pinc[:, k * LANES:(k + 1) * LANES]

    rows = BF16_ROWS
    sub = lax.broadcasted_iota(I32, (rows, LANES), 0).astype(F32)
    one = jnp.ones((rows, LANES), BF16)
    zero = jnp.zeros((rows, LANES), BF16)
    lane_c = lax.broadcasted_iota(I32, (capp, LANES), 1)
    unroll = min(8, nt)
    assert nt <= 256

    def per_expert(e, carry):
        for jb in range(capp // LANES):
            def tiles(kk, acc, jb=jb):
                for u in range(unroll):
                    rel = pinc_ref[kk * unroll + u, pl.ds(e, 1), :] - float(jb * LANES)
                    q = (jnp.clip(rel, -float(rows), float(LANES + rows)) - sub).astype(BF16)
                    acc = acc + jnp.concatenate(
                        [jnp.where(q <= float(rows * i), one, zero) for i in range(LANES // rows)], axis=0)
                return acc
            tally_ref[e, jb * LANES:(jb + 1) * LANES, :] = lax.fori_loop(
                0, nt // unroll, tiles, jnp.zeros((LANES, LANES), BF16))
        return carry

    lax.fori_loop(0, ne, per_expert, 0)
    idx_m = jnp.zeros((capp, LANES), F32)
    for e in range(ne):
        col = jnp.sum(tally_ref[e].astype(F32), axis=1, keepdims=True)
        idx_m = jnp.where(lane_c == e, col, idx_m)
    base = pl.program_id(0) * t
    idx_ref[0] = idx_m.T[0:ne, :].astype(I32) + base


def _route(aff_t, cap):
    b, ne, t = aff_t.shape
    lc = min(ROUTE_CHUNK, t)
    capp = max(cap, LANES)
    return pl.pallas_call(
        functools.partial(_route_kernel, t=t, cap=cap, capp=capp, lc=lc),
        grid=(b,),
        in_specs=[pl.BlockSpec((1, ne, t), lambda bi: (bi, 0, 0))],
        out_specs=[pl.BlockSpec((1, ne, capp), lambda bi: (bi, 0, 0)),
                   pl.BlockSpec((1, t, LANES), lambda bi: (bi, 0, 0)),
                   pl.BlockSpec((1, ne, LANES), lambda bi: (bi, 0, 0))],
        out_shape=[jax.ShapeDtypeStruct((b, ne, capp), I32),
                   jax.ShapeDtypeStruct((b, t, LANES), F32),
                   jax.ShapeDtypeStruct((b, ne, LANES), F32)],
        scratch_shapes=[pltpu.VMEM((t // LANES, ne, LANES), F32),
                        pltpu.VMEM((ne, capp, LANES), BF16)],
        compiler_params=_cparams(("arbitrary",)),
        name="route",
    )(aff_t)


def _ffn_kernel(idx_ref, haug_ref, wg_ref, wu_ref, wd_ref, y_ref, hbuf, w_in, w_dn, gsem, *, rows, d):
    e = pl.program_id(0)
    step = e * pl.num_programs(1) + pl.program_id(1)
    nsteps = pl.num_programs(0) * pl.num_programs(1)

    @pl.when(pl.program_id(1) == 0)
    def _():
        w_in[0] = wg_ref[0].astype(BF16)
        w_in[1] = wu_ref[0].astype(BF16)
        w_dn[...] = wd_ref[0].astype(BF16)

    def row_copy(src_row, dst_slot, dst_row):
        return pltpu.make_async_copy(haug_ref.at[pl.ds(src_row, 1)],
                                     hbuf.at[dst_slot, pl.ds(dst_row, 1)], gsem.at[dst_slot])

    def wait_rows(sl):
        pltpu.make_async_copy(haug_ref.at[pl.ds(0, rows)], hbuf.at[sl], gsem.at[sl]).wait()

    @pl.when(step == 0)
    def _():
        def body(r, carry):
            row_copy(idx_ref[r], 0, r).start()
            return carry
        lax.fori_loop(0, rows, body, 0)

    def run(slot):
        base = jnp.minimum(step + 1, nsteps - 1) * rows
        for r in range(rows):
            row_copy(idx_ref[base + r], 1 - slot, r).start()

        wait_rows(slot)
        hrow = hbuf[slot]
        hs = hrow[:, 0:d].astype(BF16)
        lane = lax.broadcasted_iota(I32, (rows, AUG), 1)
        gate = jnp.sum(jnp.where(lane == e, hrow[:, d:d + AUG], 0.0), axis=1, keepdims=True)
        a = _dot(hs, w_in[0])
        u = _dot(hs, w_in[1])
        hm = (a * jax.nn.sigmoid(a) * u).astype(BF16)
        y_ref[0] = (_dot(hm, w_dn[...]) * gate).astype(y_ref.dtype)

        @pl.when(step == nsteps - 1)
        def _():
            wait_rows(1 - slot)

    for parity in range(2):
        pl.when(lax.rem(step, 2) == parity)(functools.partial(run, parity))


def _ffn(idx_flat, haug, wg, wu, wd, rows):
    ne, d, f = wg.shape
    n_rows_total = idx_flat.shape[0]
    steps = n_rows_total // (ne * rows)
    grid_spec = pltpu.PrefetchScalarGridSpec(
        num_scalar_prefetch=1,
        grid=(ne, steps),
        in_specs=[pl.BlockSpec(memory_space=pl.ANY),
                  pl.BlockSpec((1, d, f), lambda e, s, idx: (e, 0, 0)),
                  pl.BlockSpec((1, d, f), lambda e, s, idx: (e, 0, 0)),
                  pl.BlockSpec((1, f, d), lambda e, s, idx: (e, 0, 0))],
        out_specs=pl.BlockSpec((1, rows, d), lambda e, s, idx: (e, s, 0)),
        scratch_shapes=[pltpu.VMEM((2, rows, d + AUG), F32),
                        pltpu.VMEM((2, d, f), BF16),
                        pltpu.VMEM((f, d), BF16),
                        pltpu.SemaphoreType.DMA((2,))],
    )
    return pl.pallas_call(
        functools.partial(_ffn_kernel, rows=rows, d=d),
        grid_spec=grid_spec,
        out_shape=jax.ShapeDtypeStruct((ne, steps * rows, d), BF16),
        compiler_params=_cparams(("arbitrary", "arbitrary")),
        name="ffn",
    )(idx_flat, haug, wg, wu, wd)


def _combine_kernel(ws_ref, nr_ref, x_ref, mod_ref, post_ref, fg_ref, y_ref, o_ref, stage, acc, sem,
                    *, cap, tm, final_norm):
    nchunk = pl.num_programs(1)
    step = pl.program_id(0) * nchunk + pl.program_id(1)
    nsteps = pl.num_programs(0) * nchunk
    slot = lax.rem(step, 2)
    ne = N_EXPERTS
    w = COMB_W
    per = LANES // w
    total_rows = y_ref.shape[1]
    lane_row = lax.broadcasted_iota(I32, (1, LANES), 1)

    def window(st, e, r):
        sample = lax.div(st, nchunk)
        nominal = ws_ref[st * ne + e] + r * w
        grow = pl.multiple_of(jnp.minimum(sample * cap + nominal, total_rows - w), BF16_ROWS)
        return nominal, grow, grow - sample * cap

    def copy(e, grow, sl, si):
        return pltpu.make_async_copy(y_ref.at[e, pl.ds(grow, w)], stage.at[sl, pl.ds(e * w, w)], sem.at[si])

    def fetch(st, r, sl, si):
        for e in range(ne):
            copy(e, window(st, e, r)[1], sl, si).start()

    def wait(sl, si):
        for e in range(ne):
            copy(e, 0, sl, si).wait()

    def gathered(st, r, sl):
        pt = post_ref[0]
        blocks = []
        for g in range(ne // per):
            tgt = jnp.zeros((1, LANES), F32)
            mine = jnp.zeros((tm, LANES), F32)
            for k in range(per):
                e = g * per + k
                nominal, _, first = window(st, e, r)
                inb = jnp.logical_and(lane_row >= k * w, lane_row < (k + 1) * w)
                slot = first - k * w + lane_row
                tgt = jnp.where(inb, jnp.where(slot >= nominal, slot, -2).astype(F32), tgt)
                mine = jnp.where(inb, pt[:, e:e + 1], mine)
            blocks.append(jnp.where(mine == tgt, 1.0, 0.0).astype(BF16))
        onehot = jnp.concatenate(blocks, axis=1)
        return _dot(onehot, stage[sl])

    @pl.when(step == 0)
    def _():
        fetch(0, 0, 0, 0)

    @pl.when(step + 1 < nsteps)
    def _():
        fetch(step + 1, 0, 1 - slot, 1 - slot)

    wait(slot, slot)
    acc[...] = gathered(step, 0, slot)

    def extra_round(r, carry):
        fetch(step, r, slot, 2)
        wait(slot, 2)
        acc[...] += gathered(step, r, slot)
        return carry

    lax.fori_loop(1, nr_ref[step], extra_round, 0)
    m = mod_ref[0]
    xn = x_ref[0] + m[5:6, :] * acc[...]
    if final_norm:
        ms = jnp.mean(xn * xn, axis=-1, keepdims=True)
        xn = xn * lax.rsqrt(ms + NORM_EPS) * fg_ref[...]
    o_ref[0] = xn


def _combine(ws_flat, nr_flat, x, mods, mod_row, post, y, cap, final_g=None):
    b, t, d = x.shape
    tm = min(ROW_TILE, t)
    ne = N_EXPERTS
    final_norm = final_g is not None
    fg = (final_g if final_norm else jnp.ones((d,), F32)).reshape(1, d)
    grid_spec = pltpu.PrefetchScalarGridSpec(
        num_scalar_prefetch=2,
        grid=(b, t // tm),
        in_specs=[pl.BlockSpec((1, tm, d), lambda bi, i, a, c: (bi, i, 0)),
                  pl.BlockSpec((1, N_MOD, d), lambda bi, i, a, c: (mod_row(bi), 0, 0)),
                  pl.BlockSpec((1, tm, LANES), lambda bi, i, a, c: (bi, i, 0)),
                  pl.BlockSpec((1, d), lambda bi, i, a, c: (0, 0)),
                  pl.BlockSpec(memory_space=pl.ANY)],
        out_specs=pl.BlockSpec((1, tm, d), lambda bi, i, a, c: (bi, i, 0)),
        scratch_shapes=[pltpu.VMEM((2, ne * COMB_W, d), BF16),
                        pltpu.VMEM((tm, d), F32),
                        pltpu.SemaphoreType.DMA((3,))],
    )
    return pl.pallas_call(
        functools.partial(_combine_kernel, cap=cap, tm=tm, final_norm=final_norm),
        grid_spec=grid_spec,
        out_shape=jax.ShapeDtypeStruct((b, t, d), F32),
        compiler_params=_cparams(("arbitrary", "arbitrary")),
        name="combine",
    )(ws_flat, nr_flat, x, mods, post, fg, y)


def _moe(xn, haug, aff_t, mods, mod_row, wg, wu, wd, rows_per_step, final_g=None):
    b, t, d = xn.shape
    ne = N_EXPERTS
    cap = CAPACITY_FACTOR * t // ne
    idx, post, offs = _route(aff_t, cap)
    idx_flat = jnp.transpose(idx[:, :, :cap], (1, 0, 2)).reshape(-1)
    y = _ffn(idx_flat, haug.reshape(b * t, d + AUG), wg, wu, wd, rows_per_step)
    tm = min(ROW_TILE, t)
    nchunk = t // tm
    lc = min(ROUTE_CHUNK, t)
    per = tm // lc
    offs_i = offs.astype(I32)[:, :, 0:t // lc + 1:per]
    start = jnp.transpose(offs_i[:, :, :nchunk], (0, 2, 1))
    end = jnp.transpose(offs_i[:, :, 1:], (0, 2, 1))
    ws = (start // BF16_ROWS) * BF16_ROWS
    nr = jnp.maximum(jnp.max((end - ws + COMB_W - 1) // COMB_W, axis=2), 1)
    return _combine(ws.reshape(-1), nr.reshape(-1), xn, mods, mod_row, post, y, cap, final_g)


def _split_hi_lo(w):
    hi = w.astype(BF16)
    lo = (w - hi.astype(F32)).astype(BF16)
    return hi, lo


def _router_weights(rw):
    d, ne = rw.shape
    pad = jnp.zeros((d, LANES - ne), F32)
    return _split_hi_lo(jnp.concatenate([rw, pad], axis=1))


def kernel(x, c, ctx, c_ctx, l0_ada_w, l0_ada_b, l0_norm1_g, l0_w_in, l0_conv_w, l0_lambda_q1, l0_lambda_k1, l0_lambda_q2, l0_lambda_k2, l0_subln_g, l0_w_out, l0_norm2_g, l0_router_w, l0_exp_w_gate, l0_exp_w_up, l0_exp_w_down, l1_ada_w, l1_ada_b, l1_norm1_g, l1_w_qkv, l1_sink, l1_w_out, l1_norm2_g, l1_router_w, l1_exp_w_gate, l1_exp_w_up, l1_exp_w_down, final_norm_g):
    b, t, d = x.shape
    nctx = ctx.shape[1]
    assert b + 1 <= MOD_ROWS
    lat_row = lambda bi: bi
    ctx_row = lambda bi: b

    cc = jnp.zeros((MOD_ROWS, d), F32).at[:b].set(c).at[b].set(c_ctx)
    mods0 = _ada(cc, l0_ada_w, l0_ada_b).reshape(MOD_ROWS, N_MOD, d)
    mods1 = _ada(cc, l1_ada_w, l1_ada_b).reshape(MOD_ROWS, N_MOD, d)
    tables = _rope_tables(t)
    qscale = LOG2E * DIFF_HEAD_DIM ** -0.5

    w_in = l0_w_in.astype(BF16)
    cd = CONV_DIM
    qkw = DIFF_HEADS * 2 * DIFF_HEAD_DIM
    wvt = jnp.transpose(l0_w_in[:, 3 * cd + 2 * qkw:]).astype(BF16)
    outs0 = [("plain", 0, cd, False, 1.0),
             ("mul", cd, cd, False, 1.0),
             ("heads", 3 * cd, qkw, True, qscale),
             ("heads", 3 * cd + qkw, qkw, True, 1.0)]
    w_main = w_in[:, :3 * cd + 2 * qkw]
    bg_l, cx_l, q_l, k_l, vt_l = _proj(x, mods0, lat_row, l0_norm1_g, w_main, outs0, tables, wvt)
    bg_c, cx_c, q_c, k_c, vt_c = _proj(ctx, mods0, ctx_row, l0_norm1_g, w_main, outs0, None, wvt)
    lams = (l0_lambda_q1, l0_lambda_k1, l0_lambda_q2, l0_lambda_k2)
    o_l = _diff_attn(q_l, [k_c, k_l], [vt_c, vt_l], lams, l0_subln_g)
    o_c = _diff_attn(q_c, [k_c], [vt_c], lams, l0_subln_g)

    w_out0 = l0_w_out.astype(BF16)
    rw0 = _router_weights(l0_router_w)
    wg0, wu0, wd0 = l0_exp_w_gate, l0_exp_w_up, l0_exp_w_down
    x1, haug, aff_t = _outproj(x, mods0, lat_row, w_out0, l0_norm2_g, *rw0, o_l,
                               conv_args=(bg_l, cx_l, l0_conv_w))
    x = _moe(x1, haug, aff_t, mods0, lat_row, wg0, wu0, wd0, FFN_SAMPLES_PER_STEP * CAPACITY_FACTOR * t // N_EXPERTS)
    c1, haug_c, aff_tc = _outproj(ctx, mods0, ctx_row, w_out0, l0_norm2_g, *rw0, o_c,
                                  conv_args=(bg_c, cx_c, l0_conv_w))
    ctx = _moe(c1, haug_c, aff_tc, mods0, ctx_row, wg0, wu0, wd0, b * (CAPACITY_FACTOR * nctx // N_EXPERTS))

    qw = SWA_HEADS * SWA_HEAD_DIM
    kvw = SWA_KV_HEADS * SWA_HEAD_DIM
    wq = l1_w_qkv[:, :qw]
    wk = l1_w_qkv[:, qw:qw + kvw].reshape(d, SWA_KV_HEADS, SWA_HEAD_DIM)
    wv = l1_w_qkv[:, qw + kvw:].reshape(d, SWA_KV_HEADS, SWA_HEAD_DIM)
    wk2 = jnp.concatenate([wk, wk], axis=2).reshape(d, 2 * kvw)
    wv2 = jnp.concatenate([wv, jnp.zeros_like(wv)], axis=2).reshape(d, 2 * kvw)
    w_qk = jnp.concatenate([wq, wk2], axis=1).astype(BF16)
    wvt1 = jnp.transpose(wv2).astype(BF16)
    outs1 = [("heads", 0, qw, True, LOG2E * SWA_HEAD_DIM ** -0.5),
             ("heads", qw, 2 * kvw, True, 1.0)]
    q1, k1, vt1 = _proj(x, mods1, lat_row, l1_norm1_g, w_qk, outs1, tables, wvt1)
    k1c, vt1c = _proj(ctx, mods1, ctx_row, l1_norm1_g, w_qk[:, qw:], [("heads", 0, 2 * kvw, False, 1.0)], None, wvt1)
    o1 = _swa_attn(l1_sink, q1, k1, vt1, k1c, vt1c)

    rw1 = _router_weights(l1_router_w)
    wg1, wu1, wd1 = l1_exp_w_gate, l1_exp_w_up, l1_exp_w_down
    x1, haug, aff_t = _outproj(x, mods1, lat_row, l1_w_out.astype(BF16), l1_norm2_g, *rw1, o1)
    return _moe(x1, haug, aff_t, mods1, lat_row, wg1, wu1, wd1, FFN_SAMPLES_PER_STEP * CAPACITY_FACTOR * t // N_EXPERTS,
                final_g=final_norm_g)
```
